```python
import math
import jax, jax.numpy as jnp
from jax import lax
import numpy as np

D_MODEL = 2048
BATCH = 1
SEQ = 8192
DEPTH = 1

MLA_HEADS = 8
MLA_Q_RANK = 768
MLA_KV_RANK = 512
MLA_NOPE = 128
MLA_ROPE = 64
MLA_V = 128
ROPE_THETA = 10000.0
Q_BLOCK = 128
NSA_HEADS = 8
NSA_GROUPS = 2
NSA_HPG = NSA_HEADS // NSA_GROUPS
NSA_DK = 192
NSA_DV = 128
CMP_LEN = 32
CMP_STRIDE = 16
CMP_HIDDEN = 256
SLC_LEN = 64
SLC_TOPN = 16
WINDOW = 512
N_BRANCH = 3
NSA_Q_BLOCK = 64
REL_BUCKETS = 32
REL_MAX_DIST = 128
D_FF = 5632
EPS = 1e-6

MIX_WIDTH = MLA_HEADS * MLA_V + NSA_HEADS * NSA_DV
IN_SPLITS = (
    MLA_Q_RANK,
    MLA_KV_RANK,
    MLA_ROPE,
    NSA_HEADS * NSA_DK,
    N_BRANCH * NSA_GROUPS * NSA_DK,
    N_BRANCH * NSA_GROUPS * NSA_DV,
    NSA_HEADS * N_BRANCH,
)
D_IN = sum(IN_SPLITS)

kernel_name = 'hybrid_mla_nsa_macaron_adaln'


def rms_norm(x, g):
    x32 = x.astype(jnp.float32)
    y = x32 * lax.rsqrt(jnp.mean(x32 * x32, axis=-1, keepdims=True) + EPS)
    return (y * g.astype(jnp.float32)).astype(x.dtype)


def modulate(h, shift, scale):
    return h * (1.0 + scale[:, None, :]) + shift[:, None, :]


def swiglu(h, w_gate, w_up, w_down):
    return (jax.nn.silu(h @ w_gate) * (h @ w_up)) @ w_down


def split_cols(h, sizes):
    return jnp.split(h, np.cumsum(sizes)[:-1].tolist(), axis=-1)


def rope(t, pos):
    d = t.shape[-1]
    inv = ROPE_THETA ** (-jnp.arange(0, d, 2, dtype=jnp.float32) / d)
    ang = pos.astype(jnp.float32)[:, None] * inv[None, :]
    cos = jnp.cos(ang)[None, :, None, :].astype(t.dtype)
    sin = jnp.sin(ang)[None, :, None, :].astype(t.dtype)
    t1, t2 = t[..., : d // 2], t[..., d // 2:]
    return jnp.concatenate([t1 * cos - t2 * sin, t1 * sin + t2 * cos], axis=-1)


def masked_softmax(logits, mask):
    logits = jnp.where(mask, logits.astype(jnp.float32), -jnp.inf)
    m = jnp.max(logits, axis=-1, keepdims=True)
    m = jnp.where(jnp.isfinite(m), m, 0.0)
    p = jnp.exp(logits - m)
    return p / jnp.maximum(jnp.sum(p, axis=-1, keepdims=True), 1e-30)


def t5_bucket(dist):
    n = jnp.maximum(dist, 0)
    max_exact = REL_BUCKETS // 2
    nf = jnp.maximum(n, 1).astype(jnp.float32)
    large = max_exact + (jnp.log(nf / max_exact) / math.log(REL_MAX_DIST / max_exact)
                         * (REL_BUCKETS - max_exact)).astype(jnp.int32)
    large = jnp.minimum(large, REL_BUCKETS - 1)
    return jnp.where(n < max_exact, n, large)


def mla_group(cq, ckv, kr, pos, q_norm, w_uq, kv_norm, w_ukv):
    B, S, _ = cq.shape
    dqk = MLA_NOPE + MLA_ROPE
    q = (rms_norm(cq, q_norm) @ w_uq).reshape(B, S, MLA_HEADS, dqk)
    q = jnp.concatenate([q[..., :MLA_NOPE], rope(q[..., MLA_NOPE:], pos)], axis=-1)
    kv = (rms_norm(ckv, kv_norm) @ w_ukv).reshape(B, S, MLA_HEADS, MLA_NOPE + MLA_V)
    k_nope, v = kv[..., :MLA_NOPE], kv[..., MLA_NOPE:]
    k_rope = rope(kr[:, :, None, :], pos)
    k = jnp.concatenate([k_nope, jnp.broadcast_to(k_rope, (B, S, MLA_HEADS, MLA_ROPE))], axis=-1)
    scale = dqk ** -0.5
    nb = S // Q_BLOCK
    qb = q.reshape(B, nb, Q_BLOCK, MLA_HEADS, dqk).transpose(1, 0, 2, 3, 4)

    def block(args):
        qi, i = args
        tq = i * Q_BLOCK + jnp.arange(Q_BLOCK)
        s = jnp.einsum('bqhd,bkhd->bhqk', qi, k) * scale
        p = masked_softmax(s, pos[None, :] <= tq[:, None])
        return jnp.einsum('bhqk,bkhd->bqhd', p.astype(v.dtype), v)

    o = lax.map(block, (qb, jnp.arange(nb)))
    return o.transpose(1, 0, 2, 3, 4).reshape(B, S, MLA_HEADS * MLA_V)


def compress(t, pe, w1, w2):
    B, S, G, d = t.shape
    n_cmp = (S - CMP_LEN) // CMP_STRIDE + 1
    idx = jnp.arange(n_cmp)[:, None] * CMP_STRIDE + jnp.arange(CMP_LEN)[None, :]
    blocks = t[:, idx] + pe[None, None, :, None, :]
    flat = blocks.transpose(0, 1, 3, 2, 4).reshape(B, n_cmp, G, CMP_LEN * d)
    return jax.nn.silu(flat @ w1) @ w2


def nsa_group(q, k3, v3, gates, pe_cmp_k, w_cmp_k1, w_cmp_k2, pe_cmp_v, w_cmp_v1, w_cmp_v2, rel_bias):
    B, S = q.shape[:2]
    G, HPG, NQ = NSA_GROUPS, NSA_HPG, NSA_Q_BLOCK
    kc = compress(k3[:, :, 0], pe_cmp_k, w_cmp_k1, w_cmp_k2)
    vc = compress(v3[:, :, 0], pe_cmp_v, w_cmp_v1, w_cmp_v2)
    n_cmp = kc.shape[1]
    n_slc = S // SLC_LEN
    n_top = min(SLC_TOPN, n_slc)
    c_start = jnp.arange(n_cmp) * CMP_STRIDE
    cmp_end = c_start + CMP_LEN - 1
    s_start = jnp.arange(n_slc) * SLC_LEN
    j_ids = jnp.arange(n_slc)
    overlap = ((c_start[:, None] < s_start[None, :] + SLC_LEN)
               & (c_start[:, None] + CMP_LEN > s_start[None, :])).astype(jnp.float32)
    ks_blk = k3[:, :, 1].reshape(B, n_slc, SLC_LEN, G, NSA_DK).transpose(0, 3, 1, 2, 4)
    vs_blk = v3[:, :, 1].reshape(B, n_slc, SLC_LEN, G, NSA_DV).transpose(0, 3, 1, 2, 4)
    kw_pad = jnp.pad(k3[:, :, 2], ((0, 0), (WINDOW, 0), (0, 0), (0, 0)))
    vw_pad = jnp.pad(v3[:, :, 2], ((0, 0), (WINDOW, 0), (0, 0), (0, 0)))
    rb_group = rel_bias.reshape(REL_BUCKETS, G, HPG)
    scale = NSA_DK ** -0.5
    nb = S // NQ
    qb = q.reshape(B, nb, NQ, G, HPG, NSA_DK).transpose(1, 0, 2, 3, 4, 5)
    gb = gates.reshape(B, nb, NQ, G, HPG, N_BRANCH).transpose(1, 0, 2, 3, 4, 5)
    b_idx = jnp.arange(B)[:, None, None, None]
    g_idx = jnp.arange(G)[None, None, :, None]

    def head_bias(bucket):
        nq, nk = bucket.shape
        return rel_bias[bucket].reshape(nq, nk, G, HPG).transpose(2, 3, 0, 1)

    def step(args):
        qi, gi, i = args
        tq = i * NQ + jnp.arange(NQ)
        s_c = (jnp.einsum('bqghd,bngd->bghqn', qi, kc) * scale
               + head_bias(t5_bucket(tq[:, None] - cmp_end[None, :])))
        p_c = masked_softmax(s_c, cmp_end[None, :] <= tq[:, None])
        o_c = jnp.einsum('bghqn,bngd->bqghd', p_c.astype(vc.dtype), vc)
        imp = jnp.einsum('bghqn,nj->bqgj', p_c, overlap)
        valid = (s_start[None, :] <= tq[:, None])[None, :, None, :]
        cur = (tq // SLC_LEN)[:, None]
        forced = ((j_ids[None, :] == 0)
                  | ((j_ids[None, :] <= cur) & (j_ids[None, :] >= cur - 1)))[None, :, None, :]
        score = jnp.where(valid, jnp.where(forced, jnp.inf, imp), -jnp.inf)
        top_val, top_idx = lax.top_k(score, n_top)
        k_sel = ks_blk[b_idx, g_idx, top_idx].reshape(B, NQ, G, n_top * SLC_LEN, NSA_DK)
        v_sel = vs_blk[b_idx, g_idx, top_idx].reshape(B, NQ, G, n_top * SLC_LEN, NSA_DV)
        tok = (top_idx[..., None] * SLC_LEN + jnp.arange(SLC_LEN)).reshape(B, NQ, G, n_top * SLC_LEN)
        tq_b = tq[None, :, None, None]
        m_s = jnp.repeat(top_val > -jnp.inf, SLC_LEN, axis=-1) & (tok <= tq_b)
        bias_s = rb_group[t5_bucket(tq_b - tok), g_idx].transpose(0, 2, 4, 1, 3)
        s_s = jnp.einsum('bqghd,bqgkd->bghqk', qi, k_sel) * scale + bias_s
        p_s = masked_softmax(s_s, m_s.transpose(0, 2, 1, 3)[:, :, None])
        o_s = jnp.einsum('bghqk,bqgkd->bqghd', p_s.astype(v_sel.dtype), v_sel)
        kw = lax.dynamic_slice_in_dim(kw_pad, i * NQ, WINDOW + NQ, axis=1)
        vw = lax.dynamic_slice_in_dim(vw_pad, i * NQ, WINDOW + NQ, axis=1)
        kpos = i * NQ - WINDOW + jnp.arange(WINDOW + NQ)
        dist = tq[:, None] - kpos[None, :]
        m_w = (kpos[None, :] >= 0) & (dist >= 0) & (dist < WINDOW)
        s_w = jnp.einsum('bqghd,bkgd->bghqk', qi, kw) * scale + head_bias(t5_bucket(dist))
        p_w = masked_softmax(s_w, m_w)
        o_w = jnp.einsum('bghqk,bkgd->bqghd', p_w.astype(vw.dtype), vw)
        o = gi[..., 0:1] * o_c + gi[..., 1:2] * o_s + gi[..., 2:3] * o_w
        return o.reshape(B, NQ, NSA_HEADS * NSA_DV)

    o = lax.map(step, (qb, gb, jnp.arange(nb)))
    return o.transpose(1, 0, 2, 3).reshape(B, S, NSA_HEADS * NSA_DV)


def token_mixing(h, w_in, mla_q_norm, w_uq, mla_kv_norm, w_ukv, pe_cmp_k, w_cmp_k1, w_cmp_k2,
                 pe_cmp_v, w_cmp_v1, w_cmp_v2, rel_bias, w_out):
    B, S, _ = h.shape
    cq, ckv, kr, q_n, k_n, v_n, g_n = split_cols(h @ w_in, IN_SPLITS)
    pos = jnp.arange(S)
    o_mla = mla_group(cq, ckv, kr, pos, mla_q_norm, w_uq, mla_kv_norm, w_ukv)
    o_nsa = nsa_group(q_n.reshape(B, S, NSA_HEADS, NSA_DK),
                      k_n.reshape(B, S, N_BRANCH, NSA_GROUPS, NSA_DK),
                      v_n.reshape(B, S, N_BRANCH, NSA_GROUPS, NSA_DV),
                      jax.nn.sigmoid(g_n.reshape(B, S, NSA_HEADS, N_BRANCH)),
                      pe_cmp_k, w_cmp_k1, w_cmp_k2, pe_cmp_v, w_cmp_v1, w_cmp_v2, rel_bias)
    return jnp.concatenate([o_mla, o_nsa], axis=-1) @ w_out


def setup_inputs(seed: int = 0) -> dict:
    key = jax.random.key(seed)
    keys = iter(jax.random.split(key, 40))
    L, D = DEPTH, D_MODEL

    def nrm(shape, s=1.0):
        return jax.random.normal(next(keys), shape, jnp.float32) * s

    def w(shape, fan_in, s=1.0):
        return nrm(shape, s * fan_in ** -0.5)

    def gain(shape):
        return 1.0 + nrm(shape, 0.02)

    return {
        'x': nrm((BATCH, SEQ, D)),
        'c': nrm((BATCH, D)),
        'w_ada': w((L, D, 9 * D), D, 0.5),
        'b_ada': nrm((L, 9 * D), 0.02),
        'norm_ffn1': gain((L, D)),
        'w1_gate': w((L, D, D_FF), D),
        'w1_up': w((L, D, D_FF), D),
        'w1_down': w((L, D_FF, D), D_FF),
        'norm_mix': gain((L, D)),
        'w_in': w((L, D, D_IN), D),
        'mla_q_norm': gain((L, MLA_Q_RANK)),
        'w_uq': w((L, MLA_Q_RANK, MLA_HEADS * (MLA_NOPE + MLA_ROPE)), MLA_Q_RANK),
        'mla_kv_norm': gain((L, MLA_KV_RANK)),
        'w_ukv': w((L, MLA_KV_RANK, MLA_HEADS * (MLA_NOPE + MLA_V)), MLA_KV_RANK),
        'pe_cmp_k': nrm((L, CMP_LEN, NSA_DK), 0.1),
        'w_cmp_k1': w((L, CMP_LEN * NSA_DK, CMP_HIDDEN), CMP_LEN * NSA_DK),
        'w_cmp_k2': w((L, CMP_HIDDEN, NSA_DK), CMP_HIDDEN),
        'pe_cmp_v': nrm((L, CMP_LEN, NSA_DV), 0.1),
        'w_cmp_v1': w((L, CMP_LEN * NSA_DV, CMP_HIDDEN), CMP_LEN * NSA_DV),
        'w_cmp_v2': w((L, CMP_HIDDEN, NSA_DV), CMP_HIDDEN),
        'rel_bias': nrm((REL_BUCKETS, NSA_HEADS), 0.1),
        'w_out': w((L, MIX_WIDTH, D), MIX_WIDTH),
        'norm_ffn2': gain((L, D)),
        'w2_gate': w((L, D, D_FF), D),
        'w2_up': w((L, D, D_FF), D),
        'w2_down': w((L, D_FF, D), D_FF),
        'norm_final': gain((D,)),
    }


def reference(x, c, w_ada, b_ada, norm_ffn1, w1_gate, w1_up, w1_down, norm_mix, w_in,
              mla_q_norm, w_uq, mla_kv_norm, w_ukv, pe_cmp_k, w_cmp_k1, w_cmp_k2,
              pe_cmp_v, w_cmp_v1, w_cmp_v2, rel_bias, w_out, norm_ffn2, w2_gate, w2_up,
              w2_down, norm_final):
    for l in range(DEPTH):
        mod = jax.nn.silu(c) @ w_ada[l] + b_ada[l]
        sh1, sc1, g1, sh2, sc2, g2, sh3, sc3, g3 = jnp.split(mod, 9, axis=-1)
        h = modulate(rms_norm(x, norm_ffn1[l]), sh1, sc1)
        x = x + 0.5 * g1[:, None, :] * swiglu(h, w1_gate[l], w1_up[l], w1_down[l])
        h = modulate(rms_norm(x, norm_mix[l]), sh2, sc2)
        x = x + g2[:, None, :] * token_mixing(h, w_in[l], mla_q_norm[l], w_uq[l], mla_kv_norm[l],
                                              w_ukv[l], pe_cmp_k[l], w_cmp_k1[l], w_cmp_k2[l],
                                              pe_cmp_v[l], w_cmp_v1[l], w_cmp_v2[l], rel_bias,
                                              w_out[l])
        h = modulate(rms_norm(x, norm_ffn2[l]), sh3, sc3)
        x = x + 0.5 * g3[:, None, :] * swiglu(h, w2_gate[l], w2_up[l], w2_down[l])
    return rms_norm(x, norm_final)
```

```python
import functools
import math

import numpy as np
import jax
import jax.numpy as jnp
from jax import lax
from jax.experimental import pallas as pl
from jax.experimental.pallas import tpu as pltpu

D_MODEL = 2048
D_FF = 5632
EPS = 1e-6
MLA_HEADS = 8
MLA_Q_RANK = 768
MLA_KV_RANK = 512
MLA_NOPE = 128
MLA_ROPE = 64
MLA_V = 128
ROPE_THETA = 10000.0
NSA_HEADS = 8
NSA_GROUPS = 2
NSA_HPG = NSA_HEADS // NSA_GROUPS
NSA_DK = 192
NSA_DV = 128
CMP_LEN = 32
CMP_STRIDE = 16
CMP_HIDDEN = 256
SLC_LEN = 64
SLC_SHIFT = 6
SLC_TOPN = 16
WINDOW = 512
N_BRANCH = 3
REL_BUCKETS = 32
REL_MAX_DIST = 128

LANES = 128
VMEM_LIMIT = 56 * 1024 * 1024

DKP = 256
ATT_TILE = 256
CMP_PAD = 128
N_SLC_PAD = 128

_BF = jnp.bfloat16
_F32 = jnp.float32
_NEG_INF = float("-inf")

COL_CQ = 0
COL_GATE = 768
COL_KR = 896
COL_CKV = 1024
COL_KR_ROT = 1536
COL_QN = 2048
COL_KN = COL_QN + NSA_HEADS * DKP
COL_VN = COL_KN + N_BRANCH * NSA_GROUPS * DKP
D_IN_PAD = COL_VN + N_BRANCH * NSA_GROUPS * NSA_DV


def _dot(a, b):
    return jnp.dot(a, b, preferred_element_type=_F32)


def _dot_nt(a, b):
    return lax.dot_general(a, b, (((1,), (1,)), ((), ())), preferred_element_type=_F32)


def _rms(x, gain):
    return x * lax.rsqrt(jnp.mean(x * x, axis=-1, keepdims=True) + EPS) * gain


def _params(semantics):
    return pltpu.CompilerParams(dimension_semantics=semantics, vmem_limit_bytes=VMEM_LIMIT)


ADA_ROWS = 256


def _ada_kernel(c_ref, w_ref, b_ref, o_ref):
    tn = o_ref.shape[1]

    def body(k, acc):
        r = pl.multiple_of(k * ADA_ROWS, ADA_ROWS)
        c = c_ref[pl.ds(r, ADA_ROWS), :]
        s = c * jax.nn.sigmoid(c)
        prod = w_ref[pl.ds(r, ADA_ROWS), :] * s
        return acc + jnp.sum(prod.reshape(ADA_ROWS // 8, 8, tn), axis=0)

    acc = lax.fori_loop(0, D_MODEL // ADA_ROWS, body, jnp.zeros((8, tn), _F32))
    o_ref[...] = jnp.sum(acc, axis=0, keepdims=True) + b_ref[...]


def _ada_mod(c, w_ada, b_ada):
    n = w_ada.shape[1]
    tn = 1024
    return pl.pallas_call(
        _ada_kernel,
        out_shape=jax.ShapeDtypeStruct((1, n), _F32),
        grid=(n // tn,),
        in_specs=[
            pl.BlockSpec((D_MODEL, 1), lambda j: (0, 0)),
            pl.BlockSpec((D_MODEL, tn), lambda j: (0, j)),
            pl.BlockSpec((1, tn), lambda j: (0, j)),
        ],
        out_specs=pl.BlockSpec((1, tn), lambda j: (0, j)),
        compiler_params=_params(("arbitrary",)),
        name="ada_mod",
    )(c.reshape(D_MODEL, 1), w_ada, b_ada.reshape(1, n))


def _ffn_kernel(x_ref, mod_ref, gain_ref, wg_ref, wu_ref, wd_ref, fin_ref, o_ref, h_ref,
                *, mod_row, final_norm):
    j = pl.program_id(1)

    @pl.when(j == 0)
    def _():
        y = _rms(x_ref[...], gain_ref[...])
        h = y * (1.0 + mod_ref[mod_row + 1:mod_row + 2, :]) + mod_ref[mod_row:mod_row + 1, :]
        h_ref[...] = h.astype(_BF)
        o_ref[...] = jnp.zeros_like(o_ref)

    h = h_ref[...]
    g = _dot(h, wg_ref[...])
    u = _dot(h, wu_ref[...])
    a = (g * jax.nn.sigmoid(g)) * u
    o_ref[...] += _dot(a.astype(_BF), wd_ref[...])

    @pl.when(j == pl.num_programs(1) - 1)
    def _():
        x2 = x_ref[...] + (0.5 * mod_ref[mod_row + 2:mod_row + 3, :]) * o_ref[...]
        if final_norm:
            x2 = _rms(x2, fin_ref[...])
        o_ref[...] = x2


def _ffn(x, mod, gain, wg, wu, wd, fin, *, mod_row, final_norm, tm=512, tf=512):
    s = x.shape[0]
    kern = functools.partial(_ffn_kernel, mod_row=mod_row, final_norm=final_norm)
    return pl.pallas_call(
        kern,
        out_shape=jax.ShapeDtypeStruct((s, D_MODEL), _F32),
        grid=(s // tm, D_FF // tf),
        in_specs=[
            pl.BlockSpec((tm, D_MODEL), lambda i, j: (i, 0)),
            pl.BlockSpec((9, D_MODEL), lambda i, j: (0, 0)),
            pl.BlockSpec((1, D_MODEL), lambda i, j: (0, 0)),
            pl.BlockSpec((D_MODEL, tf), lambda i, j: (0, j)),
            pl.BlockSpec((D_MODEL, tf), lambda i, j: (0, j)),
            pl.BlockSpec((tf, D_MODEL), lambda i, j: (j, 0)),
            pl.BlockSpec((1, D_MODEL), lambda i, j: (0, 0)),
        ],
        out_specs=pl.BlockSpec((tm, D_MODEL), lambda i, j: (i, 0)),
        scratch_shapes=[pltpu.VMEM((tm, D_MODEL), _BF)],
        compiler_params=_params(("parallel", "arbitrary")),
        name="ffn_final" if final_norm else "ffn",
    )(x, mod, gain, wg, wu, wd, fin)


def _inproj_kernel(x_ref, mod_ref, gain_ref, w_ref, cs_ref, o_ref, h_ref, *, mod_row):
    @pl.when(pl.program_id(1) == 0)
    def _():
        y = _rms(x_ref[...], gain_ref[...])
        h = y * (1.0 + mod_ref[mod_row + 1:mod_row + 2, :]) + mod_ref[mod_row:mod_row + 1, :]
        h_ref[...] = h.astype(_BF)

    o_ref[...] = (_dot(h_ref[...], w_ref[...]) * cs_ref[...]).astype(o_ref.dtype)


def _inproj(x, mod, gain, w_pad, colscale, *, mod_row, tm=512, tn=1280):
    s = x.shape[0]
    n = w_pad.shape[1]
    return pl.pallas_call(
        functools.partial(_inproj_kernel, mod_row=mod_row),
        out_shape=jax.ShapeDtypeStruct((s, n), _BF),
        grid=(s // tm, n // tn),
        in_specs=[
            pl.BlockSpec((tm, D_MODEL), lambda i, j: (i, 0)),
            pl.BlockSpec((9, D_MODEL), lambda i, j: (0, 0)),
            pl.BlockSpec((1, D_MODEL), lambda i, j: (0, 0)),
            pl.BlockSpec((D_MODEL, tn), lambda i, j: (0, j)),
            pl.BlockSpec((1, tn), lambda i, j: (0, j)),
        ],
        out_specs=pl.BlockSpec((tm, tn), lambda i, j: (i, j)),
        scratch_shapes=[pltpu.VMEM((tm, D_MODEL), _BF)],
        compiler_params=_params(("parallel", "arbitrary")),
        name="inproj",
    )(x, mod, gain, w_pad, colscale)


def _mla_prep_kernel(cq_ref, ckv_ref, kr_ref, krr_ref, gq_ref, gkv_ref, wq_ref, wkv_ref,
                     cos_ref, sin_ref, q_ref, k_ref, v_ref):
    scale = (MLA_NOPE + MLA_ROPE) ** -0.5
    cos = cos_ref[...]
    sin = sin_ref[...]
    hq = _rms(cq_ref[...].astype(_F32), gq_ref[...]).astype(_BF)
    qa = _dot(hq, wq_ref[...])
    nh = MLA_HEADS * LANES
    for h in range(MLA_HEADS):
        lo = h * LANES
        nope = qa[:, lo:lo + LANES]
        a = qa[:, nh + lo:nh + lo + LANES]
        b = qa[:, 2 * nh + lo:2 * nh + lo + LANES]
        q_ref[:, h * DKP:h * DKP + LANES] = (nope * scale).astype(_BF)
        q_ref[:, h * DKP + LANES:(h + 1) * DKP] = ((a * cos + b * sin) * scale).astype(_BF)
    hkv = _rms(ckv_ref[...].astype(_F32), gkv_ref[...]).astype(_BF)
    kv = _dot(hkv, wkv_ref[...])
    k_rope = (kr_ref[...].astype(_F32) * cos + krr_ref[...].astype(_F32) * sin).astype(_BF)
    for h in range(MLA_HEADS):
        lo = h * (MLA_NOPE + MLA_V)
        k_ref[:, h * DKP:h * DKP + LANES] = kv[:, lo:lo + MLA_NOPE].astype(_BF)
        k_ref[:, h * DKP + LANES:(h + 1) * DKP] = k_rope
        v_ref[:, h * MLA_V:(h + 1) * MLA_V] = kv[:, lo + MLA_NOPE:lo + MLA_NOPE + MLA_V].astype(_BF)


def _mla_prep(proj, gq, gkv, wq, wkv, cos, sin, *, tm=512):
    s = proj.shape[0]
    const = lambda i: (0, 0)
    return pl.pallas_call(
        _mla_prep_kernel,
        out_shape=(
            jax.ShapeDtypeStruct((s, MLA_HEADS * DKP), _BF),
            jax.ShapeDtypeStruct((s, MLA_HEADS * DKP), _BF),
            jax.ShapeDtypeStruct((s, MLA_HEADS * MLA_V), _BF),
        ),
        grid=(s // tm,),
        in_specs=[
            pl.BlockSpec((tm, MLA_Q_RANK), lambda i: (i, COL_CQ // MLA_Q_RANK)),
            pl.BlockSpec((tm, MLA_KV_RANK), lambda i: (i, COL_CKV // MLA_KV_RANK)),
            pl.BlockSpec((tm, LANES), lambda i: (i, COL_KR // LANES)),
            pl.BlockSpec((tm, LANES), lambda i: (i, COL_KR_ROT // LANES)),
            pl.BlockSpec((1, MLA_Q_RANK), const),
            pl.BlockSpec((1, MLA_KV_RANK), const),
            pl.BlockSpec(wq.shape, const),
            pl.BlockSpec(wkv.shape, const),
            pl.BlockSpec((tm, LANES), lambda i: (i, 0)),
            pl.BlockSpec((tm, LANES), lambda i: (i, 0)),
        ],
        out_specs=(
            pl.BlockSpec((tm, MLA_HEADS * DKP), lambda i: (i, 0)),
            pl.BlockSpec((tm, MLA_HEADS * DKP), lambda i: (i, 0)),
            pl.BlockSpec((tm, MLA_HEADS * MLA_V), lambda i: (i, 0)),
        ),
        compiler_params=_params(("parallel",)),
        name="mla_prep",
    )(proj, proj, proj, proj, gq, gkv, wq, wkv, cos, sin)


def _online_update(s, v, m, l, acc):
    m_new = jnp.maximum(m, jnp.max(s, axis=1, keepdims=True))
    alpha = jnp.exp(m - m_new)
    p = jnp.exp(s - m_new)
    l = alpha * l + jnp.sum(p, axis=1, keepdims=True)
    acc = alpha * acc + _dot(p.astype(_BF), v)
    return m_new, l, acc


def _mla_attn_kernel(q_ref, k_ref, v_ref, o_ref):
    t = q_ref.shape[0]
    i = pl.program_id(1)
    q = q_ref[...]

    def tile(j, carry, diagonal):
        r = pl.multiple_of(j * t, t)
        s = _dot_nt(q, k_ref[pl.ds(r, t), :])
        if diagonal:
            row = lax.broadcasted_iota(jnp.int32, (t, t), 0)
            col = lax.broadcasted_iota(jnp.int32, (t, t), 1)
            s = jnp.where(col <= row, s, _NEG_INF)
        return _online_update(s, v_ref[pl.ds(r, t), :], *carry)

    init = (jnp.full((t, 1), -1e30, _F32), jnp.zeros((t, 1), _F32), jnp.zeros((t, MLA_V), _F32))
    carry = lax.fori_loop(0, i, lambda j, c: tile(j, c, False), init)
    _, l, acc = tile(i, carry, True)
    o_ref[...] = (acc / l).astype(o_ref.dtype)


def _mla_attn(q, k, v):
    s = q.shape[0]
    t = ATT_TILE
    return pl.pallas_call(
        _mla_attn_kernel,
        out_shape=jax.ShapeDtypeStruct((s, MLA_HEADS * MLA_V), _BF),
        grid=(MLA_HEADS, s // t),
        in_specs=[
            pl.BlockSpec((t, DKP), lambda h, i: (i, h)),
            pl.BlockSpec((s, DKP), lambda h, i: (0, h)),
            pl.BlockSpec((s, MLA_V), lambda h, i: (0, h)),
        ],
        out_specs=pl.BlockSpec((t, MLA_V), lambda h, i: (i, h)),
        compiler_params=_params(("parallel", "arbitrary")),
        name="mla_attn",
    )(q, k, v)


def _t5_bucket_np(dist):
    n = np.maximum(dist, 0)
    max_exact = REL_BUCKETS // 2
    nf = np.maximum(n, 1).astype(np.float32)
    large = max_exact + (np.log(nf / np.float32(max_exact)) / np.float32(math.log(REL_MAX_DIST / max_exact))
                         * np.float32(REL_BUCKETS - max_exact)).astype(np.int32)
    large = np.minimum(large, REL_BUCKETS - 1)
    return np.where(n < max_exact, n, large).astype(np.int32)


def _bias_table_kernel(rb_ref, bk_ref, o_ref):
    h = pl.program_id(0)
    bk = bk_ref[0]
    base = rb_ref[REL_BUCKETS - 1, h]
    acc = jnp.zeros(bk.shape, _F32)
    for b in range(REL_BUCKETS - 1):
        acc = jnp.where(bk == b, rb_ref[b, h] - base, acc)
    o_ref[0, 0] = acc


def _bias_tables(rel_bias, buckets):
    nt, r, c = buckets.shape
    return pl.pallas_call(
        _bias_table_kernel,
        out_shape=jax.ShapeDtypeStruct((NSA_HEADS, nt, r, c), _F32),
        grid=(NSA_HEADS, nt),
        in_specs=[
            pl.BlockSpec(memory_space=pltpu.SMEM),
            pl.BlockSpec((1, r, c), lambda h, t: (t, 0, 0)),
        ],
        out_specs=pl.BlockSpec((1, 1, r, c), lambda h, t: (h, t, 0, 0)),
        compiler_params=_params(("arbitrary", "arbitrary")),
        name="bias_tables",
    )(rel_bias, jnp.asarray(buckets))


def _near_buckets(t):
    r = np.arange(t)[:, None]
    c = np.arange(t)[None, :]
    return np.stack([_t5_bucket_np(r - c), _t5_bucket_np(t + r - c)])


def _cmp_buckets(t):
    r = np.arange(t)[:, None]
    c = np.arange(CMP_PAD)[None, :]
    return _t5_bucket_np(r - (CMP_LEN - 1) - t + CMP_STRIDE * (CMP_PAD - c))[None]


def _compress_kernel(ch_ref, pe_ref, w1_ref, w2_ref, o_ref):
    half = w1_ref.shape[0] // 2
    ch = ch_ref[0]
    a = _dot(ch, w1_ref[:half, :])
    b = _dot(ch, w1_ref[half:, :])
    n = ch.shape[0]
    b_next = pltpu.roll(b, n - 1, axis=0)
    c0 = _dot(pe_ref[...], w1_ref[...])[0:1, :]
    hid = a + b_next + c0
    hid = hid * jax.nn.sigmoid(hid)
    o_ref[0] = _dot(hid.astype(_BF), w2_ref[...]).astype(o_ref.dtype)


def _compress(chunks, pe_flat, w1, w2):
    g, n, kd = chunks.shape
    d = w2.shape[1]
    return pl.pallas_call(
        _compress_kernel,
        out_shape=jax.ShapeDtypeStruct((g, n, d), _BF),
        grid=(g,),
        in_specs=[
            pl.BlockSpec((1, n, kd), lambda i: (i, 0, 0)),
            pl.BlockSpec(pe_flat.shape, lambda i: (0, 0)),
            pl.BlockSpec(w1.shape, lambda i: (0, 0)),
            pl.BlockSpec(w2.shape, lambda i: (0, 0)),
        ],
        out_specs=pl.BlockSpec((1, n, d), lambda i: (i, 0, 0)),
        compiler_params=_params(("arbitrary",)),
        name="nsa_compress",
    )(chunks, pe_flat, w1, w2)


def _nsa_cmp_kernel(q_ref, kc_ref, vc_ref, ov_ref, tc_ref, o_ref, sel_ref):
    t = q_ref.shape[0]
    nc = kc_ref.shape[1] - CMP_PAD
    per_tile = t // CMP_STRIDE
    i = pl.program_id(1)
    thr = (i + 1) * per_tile - CMP_PAD
    near0 = pl.multiple_of((i + 1) * per_tile, per_tile)

    kc_far = kc_ref[0, CMP_PAD:, :]
    vc_far = vc_ref[0, CMP_PAD:, :]
    ov_far = ov_ref[CMP_PAD:, :]
    kc_near = kc_ref[0, pl.ds(near0, CMP_PAD), :]
    vc_near = vc_ref[0, pl.ds(near0, CMP_PAD), :]
    ov_near = ov_ref[pl.ds(near0, CMP_PAD), :]

    far_mask = lax.broadcasted_iota(jnp.int32, (t, nc), 1) < thr
    r = lax.broadcasted_iota(jnp.int32, (t, CMP_PAD), 0)
    c = lax.broadcasted_iota(jnp.int32, (t, CMP_PAD), 1)
    near_mask = (CMP_STRIDE * c <= r - (CMP_LEN - 1) - t + CMP_STRIDE * CMP_PAD) & (c >= -thr)

    imp = jnp.zeros((t, N_SLC_PAD), _F32)
    for h in range(NSA_HPG):
        q = q_ref[:, h * DKP:(h + 1) * DKP]
        sf = jnp.where(far_mask, _dot_nt(q, kc_far), _NEG_INF)
        sn = jnp.where(near_mask, _dot_nt(q, kc_near) + tc_ref[h, 0], _NEG_INF)
        m = jnp.maximum(jnp.max(sf, axis=1, keepdims=True), jnp.max(sn, axis=1, keepdims=True))
        m = jnp.where(m == _NEG_INF, 0.0, m)
        pf = jnp.exp(sf - m)
        pn = jnp.exp(sn - m)
        den = jnp.sum(pf, axis=1, keepdims=True) + jnp.sum(pn, axis=1, keepdims=True)
        inv = 1.0 / jnp.maximum(den, 1e-30)
        pf = pf * inv
        pn = pn * inv
        pf_hi = pf.astype(_BF)
        pn_hi = pn.astype(_BF)
        o_ref[:, h * NSA_DV:(h + 1) * NSA_DV] = (_dot(pf_hi, vc_far) + _dot(pn_hi, vc_near)).astype(o_ref.dtype)
        pf_lo = (pf - pf_hi.astype(_F32)).astype(_BF)
        pn_lo = (pn - pn_hi.astype(_F32)).astype(_BF)
        imp = imp + (_dot(pf_hi, ov_far) + _dot(pn_hi, ov_near)) + (_dot(pf_lo, ov_far) + _dot(pn_lo, ov_near))

    tq = i * t + lax.broadcasted_iota(jnp.int32, (t, N_SLC_PAD), 0)
    jj = lax.broadcasted_iota(jnp.int32, (t, N_SLC_PAD), 1)
    cur = tq >> SLC_SHIFT
    valid = jj * SLC_LEN <= tq
    forced = (jj == 0) | ((jj <= cur) & (jj >= cur - 1))
    work = jnp.where(valid, jnp.where(forced, jnp.inf, imp), _NEG_INF)
    lane = jj.astype(_F32)
    sel = jnp.zeros((t, N_SLC_PAD), _F32)
    for _ in range(SLC_TOPN):
        mx = jnp.max(work, axis=1, keepdims=True)
        first = jnp.min(jnp.where(work == mx, lane, float(N_SLC_PAD)), axis=1, keepdims=True)
        hit = lane == first
        sel = jnp.where(hit & (mx > _NEG_INF), 1.0, sel)
        work = jnp.where(hit, _NEG_INF, work)
    sel_ref[0] = sel.astype(sel_ref.dtype)


def _nsa_cmp(proj, kc_pad, vc_pad, ov_pad, tc):
    s = proj.shape[0]
    t = ATT_TILE
    gw = NSA_HPG * DKP
    return pl.pallas_call(
        _nsa_cmp_kernel,
        out_shape=(
            jax.ShapeDtypeStruct((s, NSA_HEADS * NSA_DV), _BF),
            jax.ShapeDtypeStruct((NSA_GROUPS, s, N_SLC_PAD), _BF),
        ),
        grid=(NSA_GROUPS, s // t),
        in_specs=[
            pl.BlockSpec((t, gw), lambda g, i: (i, COL_QN // gw + g)),
            pl.BlockSpec((1,) + kc_pad.shape[1:], lambda g, i: (g, 0, 0)),
            pl.BlockSpec((1,) + vc_pad.shape[1:], lambda g, i: (g, 0, 0)),
            pl.BlockSpec(ov_pad.shape, lambda g, i: (0, 0)),
            pl.BlockSpec((NSA_HPG, 1, t, CMP_PAD), lambda g, i: (g, 0, 0, 0)),
        ],
        out_specs=(
            pl.BlockSpec((t, NSA_HPG * NSA_DV), lambda g, i: (i, g)),
            pl.BlockSpec((1, t, N_SLC_PAD), lambda g, i: (g, i, 0)),
        ),
        compiler_params=_params(("parallel", "arbitrary")),
        name="nsa_cmp",
    )(proj, kc_pad, vc_pad, ov_pad, tc)


def _nsa_main_kernel(q_ref, ks_ref, vs_ref, kw_ref, vw_ref, sel_ref, tn_ref, oc_ref, gate_ref,
                     o_ref, m_ref, l_ref, acc_ref):
    t = q_ref.shape[0]
    g = pl.program_id(0)
    i = pl.program_id(1)
    blocks_per_tile = t // SLC_LEN
    row = lax.broadcasted_iota(jnp.int32, (t, t), 0)
    col = lax.broadcasted_iota(jnp.int32, (t, t), 1)
    sel = sel_ref[0]

    m_ref[...] = jnp.full(m_ref.shape, -1e30, _F32)
    l_ref[...] = jnp.zeros(l_ref.shape, _F32)
    acc_ref[...] = jnp.zeros(acc_ref.shape, _F32)

    def sel_tile(j, bias_idx, causal):
        r0 = pl.multiple_of(j * t, t)
        k = ks_ref[pl.ds(r0, t), :]
        v = vs_ref[pl.ds(r0, t), :]
        eb = lax.broadcasted_iota(jnp.int32, (N_SLC_PAD, t), 0)
        ec = lax.broadcasted_iota(jnp.int32, (N_SLC_PAD, t), 1)
        expand = jnp.where(eb == j * blocks_per_tile + (ec >> SLC_SHIFT), 1.0, 0.0).astype(_BF)
        msk = _dot(sel, expand) > 0.5
        if causal:
            msk = msk & (col <= row)
        for h in range(NSA_HPG):
            s = _dot_nt(q_ref[:, h * DKP:(h + 1) * DKP], k)
            if bias_idx is not None:
                s = s + tn_ref[h, bias_idx]
            s = jnp.where(msk, s, _NEG_INF)
            m, l, acc = _online_update(s, v, m_ref[h], l_ref[h], acc_ref[h])
            m_ref[h] = m
            l_ref[h] = l
            acc_ref[h] = acc

    def far_body(j, carry):
        sel_tile(j, None, False)
        return carry

    lax.fori_loop(0, jnp.maximum(i - 1, 0), far_body, 0)

    @pl.when(i >= 1)
    def _():
        sel_tile(i - 1, 1, False)

    sel_tile(i, 0, True)

    w_tiles = []
    for a in range(3):
        ja = i - 2 + a
        r0 = pl.multiple_of(jnp.maximum(ja, 0) * t, t)
        dist = row - col + (2 - a) * t
        msk = (dist >= 0) & (dist < jnp.where(ja >= 0, WINDOW, 0))
        w_tiles.append((kw_ref[pl.ds(r0, t), :], vw_ref[pl.ds(r0, t), :], msk, (None, 1, 0)[a]))

    gates = jax.nn.sigmoid(gate_ref[...].astype(_F32))
    for h in range(NSA_HPG):
        q = q_ref[:, h * DKP:(h + 1) * DKP]
        ss = []
        for k, v, msk, bias_idx in w_tiles:
            s = _dot_nt(q, k)
            if bias_idx is not None:
                s = s + tn_ref[h, bias_idx]
            ss.append(jnp.where(msk, s, _NEG_INF))
        m = functools.reduce(jnp.maximum, [jnp.max(s, axis=1, keepdims=True) for s in ss])
        ps = [jnp.exp(s - m) for s in ss]
        den = functools.reduce(jnp.add, [jnp.sum(p, axis=1, keepdims=True) for p in ps])
        o_w = functools.reduce(jnp.add, [_dot(p.astype(_BF), w[1]) for p, w in zip(ps, w_tiles)]) / den
        o_s = acc_ref[h] / l_ref[h]
        o_c = oc_ref[:, h * NSA_DV:(h + 1) * NSA_DV].astype(_F32)
        lane = lax.broadcasted_iota(jnp.int32, gates.shape, 1)
        base = (g * NSA_HPG + h) * N_BRANCH
        gc, gs, gw = [jnp.sum(jnp.where(lane == base + b, gates, 0.0), axis=1, keepdims=True)
                      for b in range(N_BRANCH)]
        o_ref[:, h * NSA_DV:(h + 1) * NSA_DV] = (gc * o_c + gs * o_s + gw * o_w).astype(o_ref.dtype)


def _nsa_main(proj, sel, tn, o_c):
    s = proj.shape[0]
    t = ATT_TILE
    gw = NSA_HPG * DKP
    kn0 = COL_KN // DKP
    vn0 = COL_VN // NSA_DV
    return pl.pallas_call(
        _nsa_main_kernel,
        out_shape=jax.ShapeDtypeStruct((s, NSA_HEADS * NSA_DV), _BF),
        grid=(NSA_GROUPS, s // t),
        in_specs=[
            pl.BlockSpec((t, gw), lambda g, i: (i, COL_QN // gw + g)),
            pl.BlockSpec((s, DKP), lambda g, i: (0, kn0 + 1 * NSA_GROUPS + g)),
            pl.BlockSpec((s, NSA_DV), lambda g, i: (0, vn0 + 1 * NSA_GROUPS + g)),
            pl.BlockSpec((s, DKP), lambda g, i: (0, kn0 + 2 * NSA_GROUPS + g)),
            pl.BlockSpec((s, NSA_DV), lambda g, i: (0, vn0 + 2 * NSA_GROUPS + g)),
            pl.BlockSpec((1, t, N_SLC_PAD), lambda g, i: (g, i, 0)),
            pl.BlockSpec((NSA_HPG, 2, t, t), lambda g, i: (g, 0, 0, 0)),
            pl.BlockSpec((t, NSA_HPG * NSA_DV), lambda g, i: (i, g)),
            pl.BlockSpec((t, LANES), lambda g, i: (i, COL_GATE // LANES)),
        ],
        out_specs=pl.BlockSpec((t, NSA_HPG * NSA_DV), lambda g, i: (i, g)),
        scratch_shapes=[
            pltpu.VMEM((NSA_HPG, t, 1), _F32),
            pltpu.VMEM((NSA_HPG, t, 1), _F32),
            pltpu.VMEM((NSA_HPG, t, NSA_DV), _F32),
        ],
        compiler_params=_params(("parallel", "arbitrary")),
        name="nsa_main",
    )(proj, proj, proj, proj, proj, sel, tn, o_c, proj)


def _outproj_kernel(x_ref, mod_ref, om_ref, on_ref, w_ref, o_ref, *, mod_row):
    half = om_ref.shape[1]
    y = _dot(om_ref[...], w_ref[:half, :]) + _dot(on_ref[...], w_ref[half:, :])
    o_ref[...] = x_ref[...] + mod_ref[mod_row:mod_row + 1, :] * y


def _outproj(x, mod, o_mla, o_nsa, w_out, *, mod_row, tm=512):
    s = x.shape[0]
    return pl.pallas_call(
        functools.partial(_outproj_kernel, mod_row=mod_row),
        out_shape=jax.ShapeDtypeStruct((s, D_MODEL), _F32),
        grid=(s // tm,),
        in_specs=[
            pl.BlockSpec((tm, D_MODEL), lambda i: (i, 0)),
            pl.BlockSpec((9, D_MODEL), lambda i: (0, 0)),
            pl.BlockSpec((tm, o_mla.shape[1]), lambda i: (i, 0)),
            pl.BlockSpec((tm, o_nsa.shape[1]), lambda i: (i, 0)),
            pl.BlockSpec(w_out.shape, lambda i: (0, 0)),
        ],
        out_specs=pl.BlockSpec((tm, D_MODEL), lambda i: (i, 0)),
        compiler_params=_params(("parallel",)),
        name="outproj",
    )(x, mod, o_mla, o_nsa, w_out)


def _pad_cols(w, width):
    return jnp.pad(w, ((0, 0), (0, width - w.shape[1])))


def _pad_last(w, width):
    return jnp.pad(w, [(0, 0)] * (w.ndim - 1) + [(0, width - w.shape[-1])])


def _rotate_half_cols(w):
    half = MLA_ROPE // 2
    return jnp.concatenate([-w[..., half:], w[..., :half]], axis=-1)


def _w_in_padded(w_in):
    d = w_in.shape[0]
    sizes = (MLA_Q_RANK, MLA_KV_RANK, MLA_ROPE, NSA_HEADS * NSA_DK,
             N_BRANCH * NSA_GROUPS * NSA_DK, N_BRANCH * NSA_GROUPS * NSA_DV, NSA_HEADS * N_BRANCH)
    offs = np.concatenate([[0], np.cumsum(sizes)])
    cq, ckv, kr, qn, kn, vn, gn = [w_in[:, offs[a]:offs[a + 1]] for a in range(len(sizes))]
    qn = _pad_last(qn.reshape(d, NSA_HEADS, NSA_DK), DKP).reshape(d, NSA_HEADS * DKP)
    kn = _pad_last(kn.reshape(d, N_BRANCH * NSA_GROUPS, NSA_DK), DKP).reshape(d, -1)
    cols = [cq, _pad_cols(gn, LANES), _pad_cols(kr, LANES), ckv,
            _pad_cols(_rotate_half_cols(kr), LANES),
            jnp.zeros((d, COL_QN - COL_KR_ROT - LANES), w_in.dtype), qn, kn, vn]
    w = jnp.concatenate(cols, axis=1)
    assert w.shape[1] == D_IN_PAD
    colscale = np.ones((1, D_IN_PAD), np.float32)
    colscale[:, COL_QN:COL_KN] = NSA_DK ** -0.5
    return w.astype(_BF), jnp.asarray(colscale)


def _w_uq_padded(w_uq):
    r = w_uq.shape[0]
    w = w_uq.reshape(r, MLA_HEADS, MLA_NOPE + MLA_ROPE)
    nope = w[..., :MLA_NOPE].reshape(r, -1)
    rope = w[..., MLA_NOPE:]
    rope_p = _pad_last(rope, LANES).reshape(r, -1)
    rot_p = _pad_last(_rotate_half_cols(rope), LANES).reshape(r, -1)
    return jnp.concatenate([nope, rope_p, rot_p], axis=1).astype(_BF)


def _cmp_w1_padded(w1, d, dpad):
    return _pad_last(w1.reshape(CMP_LEN, d, CMP_HIDDEN).transpose(0, 2, 1), dpad) \
        .transpose(0, 2, 1).reshape(CMP_LEN * dpad, CMP_HIDDEN).astype(_BF)


def _rope_tables(s):
    inv = ROPE_THETA ** (-jnp.arange(0, MLA_ROPE, 2, dtype=_F32) / MLA_ROPE)
    ang = jnp.arange(s, dtype=_F32)[:, None] * inv[None, :]
    cos = _pad_cols(jnp.tile(jnp.cos(ang), (1, 2)), LANES)
    sin = _pad_cols(jnp.tile(jnp.sin(ang), (1, 2)), LANES)
    return cos, sin


def _overlap_padded(s):
    n_cmp_rows = s // CMP_STRIDE
    c_start = np.arange(n_cmp_rows)[:, None] * CMP_STRIDE
    s_start = np.arange(N_SLC_PAD)[None, :] * SLC_LEN
    ov = ((c_start < s_start + SLC_LEN) & (c_start + CMP_LEN > s_start)).astype(np.float32)
    return jnp.asarray(np.concatenate([np.zeros((CMP_PAD, N_SLC_PAD), np.float32), ov]), _BF)


def _mixer_heads(x, mod, norm_mix, w_in, mla_q_norm, w_uq, mla_kv_norm, w_ukv, pe_cmp_k, w_cmp_k1,
                 w_cmp_k2, pe_cmp_v, w_cmp_v1, w_cmp_v2, rel_bias):
    s = x.shape[0]
    assert s % ATT_TILE == 0 and s // SLC_LEN <= N_SLC_PAD and s >= 3 * ATT_TILE
    w_pad, colscale = _w_in_padded(w_in)
    proj = _inproj(x, mod, norm_mix.reshape(1, -1), w_pad, colscale, mod_row=3)

    cos, sin = _rope_tables(s)
    q, k, v = _mla_prep(proj, mla_q_norm.reshape(1, -1), mla_kv_norm.reshape(1, -1),
                        _w_uq_padded(w_uq), w_ukv.astype(_BF), cos, sin)
    o_mla = _mla_attn(q, k, v)

    n_chunks = s // CMP_STRIDE
    kn0 = COL_KN
    k_cmp = jnp.stack([proj[:, kn0 + g * DKP:kn0 + (g + 1) * DKP] for g in range(NSA_GROUPS)])
    v_cmp = jnp.stack([proj[:, COL_VN + g * NSA_DV:COL_VN + (g + 1) * NSA_DV] for g in range(NSA_GROUPS)])
    pe_k = jnp.broadcast_to(_pad_last(pe_cmp_k, DKP).reshape(1, -1), (8, CMP_LEN * DKP)).astype(_BF)
    pe_v = jnp.broadcast_to(pe_cmp_v.reshape(1, -1), (8, CMP_LEN * NSA_DV)).astype(_BF)
    kc = _compress(k_cmp.reshape(NSA_GROUPS, n_chunks, CMP_STRIDE * DKP), pe_k,
                   _cmp_w1_padded(w_cmp_k1, NSA_DK, DKP), _pad_cols(w_cmp_k2, DKP).astype(_BF))
    vc = _compress(v_cmp.reshape(NSA_GROUPS, n_chunks, CMP_STRIDE * NSA_DV), pe_v,
                   w_cmp_v1.astype(_BF), w_cmp_v2.astype(_BF))
    front = ((0, 0), (CMP_PAD, 0), (0, 0))
    tc = _bias_tables(rel_bias, _cmp_buckets(ATT_TILE))
    tn = _bias_tables(rel_bias, _near_buckets(ATT_TILE))
    o_c, sel = _nsa_cmp(proj, jnp.pad(kc, front), jnp.pad(vc, front), _overlap_padded(s), tc)
    o_nsa = _nsa_main(proj, sel, tn, o_c)
    return o_mla, o_nsa


def kernel(x, c, w_ada, b_ada, norm_ffn1, w1_gate, w1_up, w1_down, norm_mix, w_in, mla_q_norm, w_uq, mla_kv_norm, w_ukv, pe_cmp_k, w_cmp_k1, w_cmp_k2, pe_cmp_v, w_cmp_v1, w_cmp_v2, rel_bias, w_out, norm_ffn2, w2_gate, w2_up, w2_down, norm_final):
    assert x.shape[0] == 1 and w_ada.shape[0] == 1
    xs = x[0]
    fin = norm_final.reshape(1, -1)
    mod = _ada_mod(c, w_ada[0], b_ada[0]).reshape(9, D_MODEL)
    xs = _ffn(xs, mod, norm_ffn1[0].reshape(1, -1), w1_gate[0].astype(_BF), w1_up[0].astype(_BF),
              w1_down[0].astype(_BF), fin, mod_row=0, final_norm=False)
    o_mla, o_nsa = _mixer_heads(xs, mod, norm_mix[0], w_in[0], mla_q_norm[0], w_uq[0], mla_kv_norm[0],
                                w_ukv[0], pe_cmp_k[0], w_cmp_k1[0], w_cmp_k2[0], pe_cmp_v[0],
                                w_cmp_v1[0], w_cmp_v2[0], rel_bias)
    xs = _outproj(xs, mod, o_mla, o_nsa, w_out[0].astype(_BF), mod_row=5)
    xs = _ffn(xs, mod, norm_ffn2[0].reshape(1, -1), w2_gate[0].astype(_BF), w2_up[0].astype(_BF),
              w2_down[0].astype(_BF), fin, mod_row=6, final_norm=True)
    return xs[None]
```

```python
import functools
import math

import numpy as np
import jax
import jax.numpy as jnp
from jax import lax
from jax.experimental import pallas as pl
from jax.experimental.pallas import tpu as pltpu

D_MODEL = 2048
D_FF = 5632
EPS = 1e-6
MLA_HEADS = 8
MLA_Q_RANK = 768
MLA_KV_RANK = 512
MLA_NOPE = 128
MLA_ROPE = 64
MLA_V = 128
ROPE_THETA = 10000.0
NSA_HEADS = 8
NSA_GROUPS = 2
NSA_HPG = NSA_HEADS // NSA_GROUPS
NSA_DK = 192
NSA_DV = 128
CMP_LEN = 32
CMP_STRIDE = 16
CMP_HIDDEN = 256
SLC_LEN = 64
SLC_SHIFT = 6
SLC_TOPN = 16
WINDOW = 512
N_BRANCH = 3
REL_BUCKETS = 32
REL_MAX_DIST = 128

LANES = 128
VMEM_LIMIT = 56 * 1024 * 1024

DKP = 256
ATT_TILE = 256
CMP_PAD = 128
N_SLC_PAD = 128

_BF = jnp.bfloat16
_F32 = jnp.float32
_NEG_INF = float("-inf")
LOG2_E = math.log2(math.e)

COL_CQ = 0
COL_GATE = 768
COL_KR = 896
COL_CKV = 1024
COL_KR_ROT = 1536
COL_QN = 2048
COL_KN = COL_QN + NSA_HEADS * DKP
COL_VN = COL_KN + N_BRANCH * NSA_GROUPS * DKP
D_IN_PAD = COL_VN + N_BRANCH * NSA_GROUPS * NSA_DV


def _dot(a, b):
    return jnp.dot(a, b, preferred_element_type=_F32)


def _dot_nt(a, b):
    return lax.dot_general(a, b, (((1,), (1,)), ((), ())), preferred_element_type=_F32)


def _rms(x, gain):
    return x * lax.rsqrt(jnp.mean(x * x, axis=-1, keepdims=True) + EPS) * gain


def _params(semantics):
    return pltpu.CompilerParams(dimension_semantics=semantics, vmem_limit_bytes=VMEM_LIMIT)


ADA_ROWS = 256


def _ada_kernel(c_ref, w_ref, b_ref, o_ref):
    tn = o_ref.shape[1]

    def body(k, acc):
        r = pl.multiple_of(k * ADA_ROWS, ADA_ROWS)
        c = c_ref[pl.ds(r, ADA_ROWS), :]
        s = c * jax.nn.sigmoid(c)
        prod = w_ref[pl.ds(r, ADA_ROWS), :] * s
        return acc + jnp.sum(prod.reshape(ADA_ROWS // 8, 8, tn), axis=0)

    acc = lax.fori_loop(0, D_MODEL // ADA_ROWS, body, jnp.zeros((8, tn), _F32))
    o_ref[...] = jnp.sum(acc, axis=0, keepdims=True) + b_ref[...]


def _ada_mod(c, w_ada, b_ada):
    n = w_ada.shape[1]
    tn = 1024
    return pl.pallas_call(
        _ada_kernel,
        out_shape=jax.ShapeDtypeStruct((1, n), _F32),
        grid=(n // tn,),
        in_specs=[
            pl.BlockSpec((D_MODEL, 1), lambda j: (0, 0)),
            pl.BlockSpec((D_MODEL, tn), lambda j: (0, j)),
            pl.BlockSpec((1, tn), lambda j: (0, j)),
        ],
        out_specs=pl.BlockSpec((1, tn), lambda j: (0, j)),
        compiler_params=_params(("arbitrary",)),
        name="ada_mod",
    )(c.reshape(D_MODEL, 1), w_ada, b_ada.reshape(1, n))


def _ffn_kernel(x_ref, mod_ref, gain_ref, wg_ref, wu_ref, wd_ref, fin_ref, o_ref, h_ref,
                *, mod_row, final_norm):
    j = pl.program_id(1)

    @pl.when(j == 0)
    def _():
        y = _rms(x_ref[...], gain_ref[...])
        h = y * (1.0 + mod_ref[mod_row + 1:mod_row + 2, :]) + mod_ref[mod_row:mod_row + 1, :]
        h_ref[...] = h.astype(_BF)
        o_ref[...] = jnp.zeros_like(o_ref)

    h = h_ref[...]
    g = _dot(h, wg_ref[...])
    u = _dot(h, wu_ref[...])
    a = (g * jax.nn.sigmoid(g)) * u
    o_ref[...] += _dot(a.astype(_BF), wd_ref[...])

    @pl.when(j == pl.num_programs(1) - 1)
    def _():
        x2 = x_ref[...] + (0.5 * mod_ref[mod_row + 2:mod_row + 3, :]) * o_ref[...]
        if final_norm:
            x2 = _rms(x2, fin_ref[...])
        o_ref[...] = x2


def _ffn(x, mod, gain, wg, wu, wd, fin, *, mod_row, final_norm, tm=512, tf=512):
    s = x.shape[0]
    kern = functools.partial(_ffn_kernel, mod_row=mod_row, final_norm=final_norm)
    return pl.pallas_call(
        kern,
        out_shape=jax.ShapeDtypeStruct((s, D_MODEL), _F32),
        grid=(s // tm, D_FF // tf),
        in_specs=[
            pl.BlockSpec((tm, D_MODEL), lambda i, j: (i, 0)),
            pl.BlockSpec((9, D_MODEL), lambda i, j: (0, 0)),
            pl.BlockSpec((1, D_MODEL), lambda i, j: (0, 0)),
            pl.BlockSpec((D_MODEL, tf), lambda i, j: (0, j)),
            pl.BlockSpec((D_MODEL, tf), lambda i, j: (0, j)),
            pl.BlockSpec((tf, D_MODEL), lambda i, j: (j, 0)),
            pl.BlockSpec((1, D_MODEL), lambda i, j: (0, 0)),
        ],
        out_specs=pl.BlockSpec((tm, D_MODEL), lambda i, j: (i, 0)),
        scratch_shapes=[pltpu.VMEM((tm, D_MODEL), _BF)],
        compiler_params=_params(("parallel", "arbitrary")),
        name="ffn_final" if final_norm else "ffn",
    )(x, mod, gain, wg, wu, wd, fin)


def _inproj_kernel(x_ref, mod_ref, gain_ref, w_ref, cs_ref, o_ref, h_ref, *, mod_row):
    @pl.when(pl.program_id(1) == 0)
    def _():
        y = _rms(x_ref[...], gain_ref[...])
        h = y * (1.0 + mod_ref[mod_row + 1:mod_row + 2, :]) + mod_ref[mod_row:mod_row + 1, :]
        h_ref[...] = h.astype(_BF)

    o_ref[...] = (_dot(h_ref[...], w_ref[...]) * cs_ref[...]).astype(o_ref.dtype)


def _inproj(x, mod, gain, w_pad, colscale, *, mod_row, tm=512, tn=1280):
    s = x.shape[0]
    n = w_pad.shape[1]
    return pl.pallas_call(
        functools.partial(_inproj_kernel, mod_row=mod_row),
        out_shape=jax.ShapeDtypeStruct((s, n), _BF),
        grid=(s // tm, n // tn),
        in_specs=[
            pl.BlockSpec((tm, D_MODEL), lambda i, j: (i, 0)),
            pl.BlockSpec((9, D_MODEL), lambda i, j: (0, 0)),
            pl.BlockSpec((1, D_MODEL), lambda i, j: (0, 0)),
            pl.BlockSpec((D_MODEL, tn), lambda i, j: (0, j)),
            pl.BlockSpec((1, tn), lambda i, j: (0, j)),
        ],
        out_specs=pl.BlockSpec((tm, tn), lambda i, j: (i, j)),
        scratch_shapes=[pltpu.VMEM((tm, D_MODEL), _BF)],
        compiler_params=_params(("parallel", "arbitrary")),
        name="inproj",
    )(x, mod, gain, w_pad, colscale)


def _mla_prep_kernel(cq_ref, ckv_ref, kr_ref, krr_ref, gq_ref, gkv_ref, wq_ref, wkv_ref,
                     cos_ref, sin_ref, q_ref, k_ref, v_ref):
    scale = (MLA_NOPE + MLA_ROPE) ** -0.5 * LOG2_E
    cos = cos_ref[...]
    sin = sin_ref[...]
    hq = _rms(cq_ref[...].astype(_F32), gq_ref[...]).astype(_BF)
    qa = _dot(hq, wq_ref[...])
    nh = MLA_HEADS * LANES
    for h in range(MLA_HEADS):
        lo = h * LANES
        nope = qa[:, lo:lo + LANES]
        a = qa[:, nh + lo:nh + lo + LANES]
        b = qa[:, 2 * nh + lo:2 * nh + lo + LANES]
        q_ref[:, h * DKP:h * DKP + LANES] = (nope * scale).astype(_BF)
        q_ref[:, h * DKP + LANES:(h + 1) * DKP] = ((a * cos + b * sin) * scale).astype(_BF)
    hkv = _rms(ckv_ref[...].astype(_F32), gkv_ref[...]).astype(_BF)
    kv = _dot(hkv, wkv_ref[...])
    k_rope = (kr_ref[...].astype(_F32) * cos + krr_ref[...].astype(_F32) * sin).astype(_BF)
    for h in range(MLA_HEADS):
        lo = h * (MLA_NOPE + MLA_V)
        k_ref[:, h * DKP:h * DKP + LANES] = kv[:, lo:lo + MLA_NOPE].astype(_BF)
        k_ref[:, h * DKP + LANES:(h + 1) * DKP] = k_rope
        v_ref[:, h * MLA_V:(h + 1) * MLA_V] = kv[:, lo + MLA_NOPE:lo + MLA_NOPE + MLA_V].astype(_BF)


def _mla_prep(proj, gq, gkv, wq, wkv, cos, sin, *, tm=512):
    s = proj.shape[0]
    const = lambda i: (0, 0)
    return pl.pallas_call(
        _mla_prep_kernel,
        out_shape=(
            jax.ShapeDtypeStruct((s, MLA_HEADS * DKP), _BF),
            jax.ShapeDtypeStruct((s, MLA_HEADS * DKP), _BF),
            jax.ShapeDtypeStruct((s, MLA_HEADS * MLA_V), _BF),
        ),
        grid=(s // tm,),
        in_specs=[
            pl.BlockSpec((tm, MLA_Q_RANK), lambda i: (i, COL_CQ // MLA_Q_RANK)),
            pl.BlockSpec((tm, MLA_KV_RANK), lambda i: (i, COL_CKV // MLA_KV_RANK)),
            pl.BlockSpec((tm, LANES), lambda i: (i, COL_KR // LANES)),
            pl.BlockSpec((tm, LANES), lambda i: (i, COL_KR_ROT // LANES)),
            pl.BlockSpec((1, MLA_Q_RANK), const),
            pl.BlockSpec((1, MLA_KV_RANK), const),
            pl.BlockSpec(wq.shape, const),
            pl.BlockSpec(wkv.shape, const),
            pl.BlockSpec((tm, LANES), lambda i: (i, 0)),
            pl.BlockSpec((tm, LANES), lambda i: (i, 0)),
        ],
        out_specs=(
            pl.BlockSpec((tm, MLA_HEADS * DKP), lambda i: (i, 0)),
            pl.BlockSpec((tm, MLA_HEADS * DKP), lambda i: (i, 0)),
            pl.BlockSpec((tm, MLA_HEADS * MLA_V), lambda i: (i, 0)),
        ),
        compiler_params=_params(("parallel",)),
        name="mla_prep",
    )(proj, proj, proj, proj, gq, gkv, wq, wkv, cos, sin)


def _online_update(s, v, m, l, acc):
    m_new = jnp.maximum(m, jnp.max(s, axis=1, keepdims=True))
    alpha = jnp.exp(m - m_new)
    p = jnp.exp(s - m_new)
    l = alpha * l + jnp.sum(p, axis=1, keepdims=True)
    acc = alpha * acc + _dot(p.astype(_BF), v)
    return m_new, l, acc


def _online_update_t(st, vts, m, l, acc):
    t = vts[0].shape[1]
    m_new = jnp.maximum(m, jnp.max(st, axis=0, keepdims=True))
    alpha = jnp.exp2(m - m_new)
    pt = jnp.exp2(st - m_new)
    l = alpha * l + jnp.sum(pt, axis=0, keepdims=True)
    pb = pt.astype(_BF)
    pv = functools.reduce(jnp.add, [_dot(vt, pb[a * t:(a + 1) * t, :]) for a, vt in enumerate(vts)])
    return m_new, l, alpha * acc + pv


def _mla_attn_kernel(q_ref, k_ref, vt_ref, o_ref, *, n_sub):
    t = q_ref.shape[0]
    tk = n_sub * t
    i = pl.program_id(1)
    q = q_ref[...]
    n_full = i // n_sub
    tail_b0 = jnp.maximum(i + 1 - n_sub, 0)

    def logits(c):
        b0 = jnp.where(c < n_full, c * n_sub, tail_b0)
        return _dot_nt(k_ref[pl.ds(pl.multiple_of(b0 * t, t), tk), :], q)

    def values(b0):
        return [vt_ref[0, b0 + a] for a in range(n_sub)]

    def body(j, carry):
        st, m, l, acc = carry
        st_next = logits(j + 1)
        return (st_next,) + _online_update_t(st, values(j * n_sub), m, l, acc)

    init = (logits(0), jnp.full((1, t), -1e30, _F32), jnp.zeros((1, t), _F32), jnp.zeros((MLA_V, t), _F32))
    st, m, l, acc = lax.fori_loop(0, n_full, body, init)
    kpos = tail_b0 * t + lax.broadcasted_iota(jnp.int32, (tk, t), 0)
    qpos = i * t + lax.broadcasted_iota(jnp.int32, (tk, t), 1)
    st = jnp.where((kpos >= n_full * tk) & (kpos <= qpos), st, _NEG_INF)
    _, l, acc = _online_update_t(st, values(tail_b0), m, l, acc)
    o_ref[...] = (acc * (1.0 / l)).T.astype(o_ref.dtype)


def _mla_attn(q, k, v, *, n_sub=4):
    s = q.shape[0]
    t = ATT_TILE
    nb = s // t
    vt = v.reshape(nb, t, MLA_HEADS, MLA_V).transpose(2, 0, 3, 1)
    return pl.pallas_call(
        functools.partial(_mla_attn_kernel, n_sub=n_sub),
        out_shape=jax.ShapeDtypeStruct((s, MLA_HEADS * MLA_V), _BF),
        grid=(MLA_HEADS, nb),
        in_specs=[
            pl.BlockSpec((t, DKP), lambda h, i: (i, h)),
            pl.BlockSpec((s, DKP), lambda h, i: (0, h)),
            pl.BlockSpec((1, nb, MLA_V, t), lambda h, i: (h, 0, 0, 0)),
        ],
        out_specs=pl.BlockSpec((t, MLA_V), lambda h, i: (i, h)),
        compiler_params=_params(("parallel", "arbitrary")),
        name="mla_attn",
    )(q, k, vt)


def _t5_bucket_np(dist):
    n = np.maximum(dist, 0)
    max_exact = REL_BUCKETS // 2
    nf = np.maximum(n, 1).astype(np.float32)
    large = max_exact + (np.log(nf / np.float32(max_exact)) / np.float32(math.log(REL_MAX_DIST / max_exact))
                         * np.float32(REL_BUCKETS - max_exact)).astype(np.int32)
    large = np.minimum(large, REL_BUCKETS - 1)
    return np.where(n < max_exact, n, large).astype(np.int32)


def _bias_table_kernel(rb_ref, bk_ref, o_ref):
    h = pl.program_id(0)
    bk = bk_ref[0]
    base = rb_ref[REL_BUCKETS - 1, h]
    acc = jnp.zeros(bk.shape, _F32)
    for b in range(REL_BUCKETS - 1):
        acc = jnp.where(bk == b, rb_ref[b, h] - base, acc)
    o_ref[0, 0] = acc


def _bias_tables(rel_bias, buckets):
    nt, r, c = buckets.shape
    return pl.pallas_call(
        _bias_table_kernel,
        out_shape=jax.ShapeDtypeStruct((NSA_HEADS, nt, r, c), _F32),
        grid=(NSA_HEADS, nt),
        in_specs=[
            pl.BlockSpec(memory_space=pltpu.SMEM),
            pl.BlockSpec((1, r, c), lambda h, t: (t, 0, 0)),
        ],
        out_specs=pl.BlockSpec((1, 1, r, c), lambda h, t: (h, t, 0, 0)),
        compiler_params=_params(("arbitrary", "arbitrary")),
        name="bias_tables",
    )(rel_bias, jnp.asarray(buckets))


def _near_buckets(t):
    r = np.arange(t)[:, None]
    c = np.arange(t)[None, :]
    return np.stack([_t5_bucket_np(r - c), _t5_bucket_np(t + r - c)])


def _cmp_buckets(t):
    r = np.arange(t)[:, None]
    c = np.arange(CMP_PAD)[None, :]
    return _t5_bucket_np(r - (CMP_LEN - 1) - t + CMP_STRIDE * (CMP_PAD - c))[None]


def _compress_kernel(ch_ref, pe_ref, w1_ref, w2_ref, o_ref):
    half = w1_ref.shape[0] // 2
    ch = ch_ref[0]
    a = _dot(ch, w1_ref[:half, :])
    b = _dot(ch, w1_ref[half:, :])
    n = ch.shape[0]
    b_next = pltpu.roll(b, n - 1, axis=0)
    c0 = _dot(pe_ref[...], w1_ref[...])[0:1, :]
    hid = a + b_next + c0
    hid = hid * jax.nn.sigmoid(hid)
    o_ref[0] = _dot(hid.astype(_BF), w2_ref[...]).astype(o_ref.dtype)


def _compress(chunks, pe_flat, w1, w2):
    g, n, kd = chunks.shape
    d = w2.shape[1]
    return pl.pallas_call(
        _compress_kernel,
        out_shape=jax.ShapeDtypeStruct((g, n, d), _BF),
        grid=(g,),
        in_specs=[
            pl.BlockSpec((1, n, kd), lambda i: (i, 0, 0)),
            pl.BlockSpec(pe_flat.shape, lambda i: (0, 0)),
            pl.BlockSpec(w1.shape, lambda i: (0, 0)),
            pl.BlockSpec(w2.shape, lambda i: (0, 0)),
        ],
        out_specs=pl.BlockSpec((1, n, d), lambda i: (i, 0, 0)),
        compiler_params=_params(("arbitrary",)),
        name="nsa_compress",
    )(chunks, pe_flat, w1, w2)


def _nsa_cmp_kernel(q_ref, kc_ref, vc_ref, ov_ref, tc_ref, o_ref, sel_ref):
    t = q_ref.shape[0]
    nc = kc_ref.shape[1] - CMP_PAD
    per_tile = t // CMP_STRIDE
    i = pl.program_id(1)
    thr = (i + 1) * per_tile - CMP_PAD
    near0 = pl.multiple_of((i + 1) * per_tile, per_tile)

    kc_far = kc_ref[0, CMP_PAD:, :]
    vc_far = vc_ref[0, CMP_PAD:, :]
    ov_far = ov_ref[CMP_PAD:, :]
    kc_near = kc_ref[0, pl.ds(near0, CMP_PAD), :]
    vc_near = vc_ref[0, pl.ds(near0, CMP_PAD), :]
    ov_near = ov_ref[pl.ds(near0, CMP_PAD), :]

    far_mask = lax.broadcasted_iota(jnp.int32, (t, nc), 1) < thr
    r = lax.broadcasted_iota(jnp.int32, (t, CMP_PAD), 0)
    c = lax.broadcasted_iota(jnp.int32, (t, CMP_PAD), 1)
    near_mask = (CMP_STRIDE * c <= r - (CMP_LEN - 1) - t + CMP_STRIDE * CMP_PAD) & (c >= -thr)

    imp = jnp.zeros((t, N_SLC_PAD), _F32)
    for h in range(NSA_HPG):
        q = q_ref[:, h * DKP:(h + 1) * DKP]
        sf = jnp.where(far_mask, _dot_nt(q, kc_far), _NEG_INF)
        sn = jnp.where(near_mask, _dot_nt(q, kc_near) + tc_ref[h, 0], _NEG_INF)
        m = jnp.maximum(jnp.max(sf, axis=1, keepdims=True), jnp.max(sn, axis=1, keepdims=True))
        m = jnp.where(m == _NEG_INF, 0.0, m)
        pf = jnp.exp(sf - m)
        pn = jnp.exp(sn - m)
        den = jnp.sum(pf, axis=1, keepdims=True) + jnp.sum(pn, axis=1, keepdims=True)
        inv = 1.0 / jnp.maximum(den, 1e-30)
        pf = pf * inv
        pn = pn * inv
        pf_hi = pf.astype(_BF)
        pn_hi = pn.astype(_BF)
        o_ref[:, h * NSA_DV:(h + 1) * NSA_DV] = (_dot(pf_hi, vc_far) + _dot(pn_hi, vc_near)).astype(o_ref.dtype)
        pf_lo = (pf - pf_hi.astype(_F32)).astype(_BF)
        pn_lo = (pn - pn_hi.astype(_F32)).astype(_BF)
        imp = imp + (_dot(pf_hi, ov_far) + _dot(pn_hi, ov_near)) + (_dot(pf_lo, ov_far) + _dot(pn_lo, ov_near))

    tq = i * t + lax.broadcasted_iota(jnp.int32, (t, N_SLC_PAD), 0)
    jj = lax.broadcasted_iota(jnp.int32, (t, N_SLC_PAD), 1)
    cur = tq >> SLC_SHIFT
    valid = jj * SLC_LEN <= tq
    forced = (jj == 0) | ((jj <= cur) & (jj >= cur - 1))
    work = jnp.where(valid, jnp.where(forced, jnp.inf, imp), _NEG_INF)
    lane = jj.astype(_F32)
    sel = jnp.zeros((t, N_SLC_PAD), _F32)
    for _ in range(SLC_TOPN):
        mx = jnp.max(work, axis=1, keepdims=True)
        first = jnp.min(jnp.where(work == mx, lane, float(N_SLC_PAD)), axis=1, keepdims=True)
        hit = lane == first
        sel = jnp.where(hit & (mx > _NEG_INF), 1.0, sel)
        work = jnp.where(hit, _NEG_INF, work)
    sel_ref[0] = sel.astype(sel_ref.dtype)


def _nsa_cmp(proj, kc_pad, vc_pad, ov_pad, tc):
    s = proj.shape[0]
    t = ATT_TILE
    gw = NSA_HPG * DKP
    return pl.pallas_call(
        _nsa_cmp_kernel,
        out_shape=(
            jax.ShapeDtypeStruct((s, NSA_HEADS * NSA_DV), _BF),
            jax.ShapeDtypeStruct((NSA_GROUPS, s, N_SLC_PAD), _BF),
        ),
        grid=(NSA_GROUPS, s // t),
        in_specs=[
            pl.BlockSpec((t, gw), lambda g, i: (i, COL_QN // gw + g)),
            pl.BlockSpec((1,) + kc_pad.shape[1:], lambda g, i: (g, 0, 0)),
            pl.BlockSpec((1,) + vc_pad.shape[1:], lambda g, i: (g, 0, 0)),
            pl.BlockSpec(ov_pad.shape, lambda g, i: (0, 0)),
            pl.BlockSpec((NSA_HPG, 1, t, CMP_PAD), lambda g, i: (g, 0, 0, 0)),
        ],
        out_specs=(
            pl.BlockSpec((t, NSA_HPG * NSA_DV), lambda g, i: (i, g)),
            pl.BlockSpec((1, t, N_SLC_PAD), lambda g, i: (g, i, 0)),
        ),
        compiler_params=_params(("parallel", "arbitrary")),
        name="nsa_cmp",
    )(proj, kc_pad, vc_pad, ov_pad, tc)


def _nsa_main_kernel(q_ref, ks_ref, vs_ref, kw_ref, vw_ref, sel_ref, tn_ref, oc_ref, gate_ref,
                     o_ref, m_ref, l_ref, acc_ref):
    t = q_ref.shape[0]
    g = pl.program_id(0)
    i = pl.program_id(1)
    blocks_per_tile = t // SLC_LEN
    row = lax.broadcasted_iota(jnp.int32, (t, t), 0)
    col = lax.broadcasted_iota(jnp.int32, (t, t), 1)
    sel = sel_ref[0]

    m_ref[...] = jnp.full(m_ref.shape, -1e30, _F32)
    l_ref[...] = jnp.zeros(l_ref.shape, _F32)
    acc_ref[...] = jnp.zeros(acc_ref.shape, _F32)

    def sel_tile(j, bias_idx, causal):
        r0 = pl.multiple_of(j * t, t)
        k = ks_ref[pl.ds(r0, t), :]
        v = vs_ref[pl.ds(r0, t), :]
        eb = lax.broadcasted_iota(jnp.int32, (N_SLC_PAD, t), 0)
        ec = lax.broadcasted_iota(jnp.int32, (N_SLC_PAD, t), 1)
        expand = jnp.where(eb == j * blocks_per_tile + (ec >> SLC_SHIFT), 1.0, 0.0).astype(_BF)
        msk = _dot(sel, expand) > 0.5
        if causal:
            msk = msk & (col <= row)
        for h in range(NSA_HPG):
            s = _dot_nt(q_ref[:, h * DKP:(h + 1) * DKP], k)
            if bias_idx is not None:
                s = s + tn_ref[h, bias_idx]
            s = jnp.where(msk, s, _NEG_INF)
            m, l, acc = _online_update(s, v, m_ref[h], l_ref[h], acc_ref[h])
            m_ref[h] = m
            l_ref[h] = l
            acc_ref[h] = acc

    def far_body(j, carry):
        sel_tile(j, None, False)
        return carry

    lax.fori_loop(0, jnp.maximum(i - 1, 0), far_body, 0)

    @pl.when(i >= 1)
    def _():
        sel_tile(i - 1, 1, False)

    sel_tile(i, 0, True)

    w_tiles = []
    for a in range(3):
        ja = i - 2 + a
        r0 = pl.multiple_of(jnp.maximum(ja, 0) * t, t)
        dist = row - col + (2 - a) * t
        msk = (dist >= 0) & (dist < jnp.where(ja >= 0, WINDOW, 0))
        w_tiles.append((kw_ref[pl.ds(r0, t), :], vw_ref[pl.ds(r0, t), :], msk, (None, 1, 0)[a]))

    gates = jax.nn.sigmoid(gate_ref[...].astype(_F32))
    for h in range(NSA_HPG):
        q = q_ref[:, h * DKP:(h + 1) * DKP]
        ss = []
        for k, v, msk, bias_idx in w_tiles:
            s = _dot_nt(q, k)
            if bias_idx is not None:
                s = s + tn_ref[h, bias_idx]
            ss.append(jnp.where(msk, s, _NEG_INF))
        m = functools.reduce(jnp.maximum, [jnp.max(s, axis=1, keepdims=True) for s in ss])
        ps = [jnp.exp(s - m) for s in ss]
        den = functools.reduce(jnp.add, [jnp.sum(p, axis=1, keepdims=True) for p in ps])
        o_w = functools.reduce(jnp.add, [_dot(p.astype(_BF), w[1]) for p, w in zip(ps, w_tiles)]) / den
        o_s = acc_ref[h] / l_ref[h]
        o_c = oc_ref[:, h * NSA_DV:(h + 1) * NSA_DV].astype(_F32)
        lane = lax.broadcasted_iota(jnp.int32, gates.shape, 1)
        base = (g * NSA_HPG + h) * N_BRANCH
        gc, gs, gw = [jnp.sum(jnp.where(lane == base + b, gates, 0.0), axis=1, keepdims=True)
                      for b in range(N_BRANCH)]
        o_ref[:, h * NSA_DV:(h + 1) * NSA_DV] = (gc * o_c + gs * o_s + gw * o_w).astype(o_ref.dtype)


def _nsa_main(proj, sel, tn, o_c):
    s = proj.shape[0]
    t = ATT_TILE
    gw = NSA_HPG * DKP
    kn0 = COL_KN // DKP
    vn0 = COL_VN // NSA_DV
    return pl.pallas_call(
        _nsa_main_kernel,
        out_shape=jax.ShapeDtypeStruct((s, NSA_HEADS * NSA_DV), _BF),
        grid=(NSA_GROUPS, s // t),
        in_specs=[
            pl.BlockSpec((t, gw), lambda g, i: (i, COL_QN // gw + g)),
            pl.BlockSpec((s, DKP), lambda g, i: (0, kn0 + 1 * NSA_GROUPS + g)),
            pl.BlockSpec((s, NSA_DV), lambda g, i: (0, vn0 + 1 * NSA_GROUPS + g)),
            pl.BlockSpec((s, DKP), lambda g, i: (0, kn0 + 2 * NSA_GROUPS + g)),
            pl.BlockSpec((s, NSA_DV), lambda g, i: (0, vn0 + 2 * NSA_GROUPS + g)),
            pl.BlockSpec((1, t, N_SLC_PAD), lambda g, i: (g, i, 0)),
            pl.BlockSpec((NSA_HPG, 2, t, t), lambda g, i: (g, 0, 0, 0)),
            pl.BlockSpec((t, NSA_HPG * NSA_DV), lambda g, i: (i, g)),
            pl.BlockSpec((t, LANES), lambda g, i: (i, COL_GATE // LANES)),
        ],
        out_specs=pl.BlockSpec((t, NSA_HPG * NSA_DV), lambda g, i: (i, g)),
        scratch_shapes=[
            pltpu.VMEM((NSA_HPG, t, 1), _F32),
            pltpu.VMEM((NSA_HPG, t, 1), _F32),
            pltpu.VMEM((NSA_HPG, t, NSA_DV), _F32),
        ],
        compiler_params=_params(("parallel", "arbitrary")),
        name="nsa_main",
    )(proj, proj, proj, proj, proj, sel, tn, o_c, proj)


def _outproj_kernel(x_ref, mod_ref, om_ref, on_ref, w_ref, o_ref, *, mod_row):
    half = om_ref.shape[1]
    y = _dot(om_ref[...], w_ref[:half, :]) + _dot(on_ref[...], w_ref[half:, :])
    o_ref[...] = x_ref[...] + mod_ref[mod_row:mod_row + 1, :] * y


def _outproj(x, mod, o_mla, o_nsa, w_out, *, mod_row, tm=512):
    s = x.shape[0]
    return pl.pallas_call(
        functools.partial(_outproj_kernel, mod_row=mod_row),
        out_shape=jax.ShapeDtypeStruct((s, D_MODEL), _F32),
        grid=(s // tm,),
        in_specs=[
            pl.BlockSpec((tm, D_MODEL), lambda i: (i, 0)),
            pl.BlockSpec((9, D_MODEL), lambda i: (0, 0)),
            pl.BlockSpec((tm, o_mla.shape[1]), lambda i: (i, 0)),
            pl.BlockSpec((tm, o_nsa.shape[1]), lambda i: (i, 0)),
            pl.BlockSpec(w_out.shape, lambda i: (0, 0)),
        ],
        out_specs=pl.BlockSpec((tm, D_MODEL), lambda i: (i, 0)),
        compiler_params=_params(("parallel",)),
        name="outproj",
    )(x, mod, o_mla, o_nsa, w_out)


def _pad_cols(w, width):
    return jnp.pad(w, ((0, 0), (0, width - w.shape[1])))


def _pad_last(w, width):
    return jnp.pad(w, [(0, 0)] * (w.ndim - 1) + [(0, width - w.shape[-1])])


def _rotate_half_cols(w):
    half = MLA_ROPE // 2
    return jnp.concatenate([-w[..., half:], w[..., :half]], axis=-1)


def _w_in_padded(w_in):
    d = w_in.shape[0]
    sizes = (MLA_Q_RANK, MLA_KV_RANK, MLA_ROPE, NSA_HEADS * NSA_DK,
             N_BRANCH * NSA_GROUPS * NSA_DK, N_BRANCH * NSA_GROUPS * NSA_DV, NSA_HEADS * N_BRANCH)
    offs = np.concatenate([[0], np.cumsum(sizes)])
    cq, ckv, kr, qn, kn, vn, gn = [w_in[:, offs[a]:offs[a + 1]] for a in range(len(sizes))]
    qn = _pad_last(qn.reshape(d, NSA_HEADS, NSA_DK), DKP).reshape(d, NSA_HEADS * DKP)
    kn = _pad_last(kn.reshape(d, N_BRANCH * NSA_GROUPS, NSA_DK), DKP).reshape(d, -1)
    cols = [cq, _pad_cols(gn, LANES), _pad_cols(kr, LANES), ckv,
            _pad_cols(_rotate_half_cols(kr), LANES),
            jnp.zeros((d, COL_QN - COL_KR_ROT - LANES), w_in.dtype), qn, kn, vn]
    w = jnp.concatenate(cols, axis=1)
    assert w.shape[1] == D_IN_PAD
    colscale = np.ones((1, D_IN_PAD), np.float32)
    colscale[:, COL_QN:COL_KN] = NSA_DK ** -0.5
    return w.astype(_BF), jnp.asarray(colscale)


def _w_uq_padded(w_uq):
    r = w_uq.shape[0]
    w = w_uq.reshape(r, MLA_HEADS, MLA_NOPE + MLA_ROPE)
    nope = w[..., :MLA_NOPE].reshape(r, -1)
    rope = w[..., MLA_NOPE:]
    rope_p = _pad_last(rope, LANES).reshape(r, -1)
    rot_p = _pad_last(_rotate_half_cols(rope), LANES).reshape(r, -1)
    return jnp.concatenate([nope, rope_p, rot_p], axis=1).astype(_BF)


def _cmp_w1_padded(w1, d, dpad):
    return _pad_last(w1.reshape(CMP_LEN, d, CMP_HIDDEN).transpose(0, 2, 1), dpad) \
        .transpose(0, 2, 1).reshape(CMP_LEN * dpad, CMP_HIDDEN).astype(_BF)


def _rope_tables(s):
    inv = ROPE_THETA ** (-jnp.arange(0, MLA_ROPE, 2, dtype=_F32) / MLA_ROPE)
    ang = jnp.arange(s, dtype=_F32)[:, None] * inv[None, :]
    cos = _pad_cols(jnp.tile(jnp.cos(ang), (1, 2)), LANES)
    sin = _pad_cols(jnp.tile(jnp.sin(ang), (1, 2)), LANES)
    return cos, sin


def _overlap_padded(s):
    n_cmp_rows = s // CMP_STRIDE
    c_start = np.arange(n_cmp_rows)[:, None] * CMP_STRIDE
    s_start = np.arange(N_SLC_PAD)[None, :] * SLC_LEN
    ov = ((c_start < s_start + SLC_LEN) & (c_start + CMP_LEN > s_start)).astype(np.float32)
    return jnp.asarray(np.concatenate([np.zeros((CMP_PAD, N_SLC_PAD), np.float32), ov]), _BF)


def _mixer_heads(x, mod, norm_mix, w_in, mla_q_norm, w_uq, mla_kv_norm, w_ukv, pe_cmp_k, w_cmp_k1,
                 w_cmp_k2, pe_cmp_v, w_cmp_v1, w_cmp_v2, rel_bias):
    s = x.shape[0]
    assert s % ATT_TILE == 0 and s // SLC_LEN <= N_SLC_PAD and s >= 3 * ATT_TILE
    w_pad, colscale = _w_in_padded(w_in)
    proj = _inproj(x, mod, norm_mix.reshape(1, -1), w_pad, colscale, mod_row=3)

    cos, sin = _rope_tables(s)
    q, k, v = _mla_prep(proj, mla_q_norm.reshape(1, -1), mla_kv_norm.reshape(1, -1),
                        _w_uq_padded(w_uq), w_ukv.astype(_BF), cos, sin)
    o_mla = _mla_attn(q, k, v)

    n_chunks = s // CMP_STRIDE
    kn0 = COL_KN
    k_cmp = jnp.stack([proj[:, kn0 + g * DKP:kn0 + (g + 1) * DKP] for g in range(NSA_GROUPS)])
    v_cmp = jnp.stack([proj[:, COL_VN + g * NSA_DV:COL_VN + (g + 1) * NSA_DV] for g in range(NSA_GROUPS)])
    pe_k = jnp.broadcast_to(_pad_last(pe_cmp_k, DKP).reshape(1, -1), (8, CMP_LEN * DKP)).astype(_BF)
    pe_v = jnp.broadcast_to(pe_cmp_v.reshape(1, -1), (8, CMP_LEN * NSA_DV)).astype(_BF)
    kc = _compress(k_cmp.reshape(NSA_GROUPS, n_chunks, CMP_STRIDE * DKP), pe_k,
                   _cmp_w1_padded(w_cmp_k1, NSA_DK, DKP), _pad_cols(w_cmp_k2, DKP).astype(_BF))
    vc = _compress(v_cmp.reshape(NSA_GROUPS, n_chunks, CMP_STRIDE * NSA_DV), pe_v,
                   w_cmp_v1.astype(_BF), w_cmp_v2.astype(_BF))
    front = ((0, 0), (CMP_PAD, 0), (0, 0))
    tc = _bias_tables(rel_bias, _cmp_buckets(ATT_TILE))
    tn = _bias_tables(rel_bias, _near_buckets(ATT_TILE))
    o_c, sel = _nsa_cmp(proj, jnp.pad(kc, front), jnp.pad(vc, front), _overlap_padded(s), tc)
    o_nsa = _nsa_main(proj, sel, tn, o_c)
    return o_mla, o_nsa


def kernel(x, c, w_ada, b_ada, norm_ffn1, w1_gate, w1_up, w1_down, norm_mix, w_in, mla_q_norm, w_uq, mla_kv_norm, w_ukv, pe_cmp_k, w_cmp_k1, w_cmp_k2, pe_cmp_v, w_cmp_v1, w_cmp_v2, rel_bias, w_out, norm_ffn2, w2_gate, w2_up, w2_down, norm_final):
    assert x.shape[0] == 1 and w_ada.shape[0] == 1
    xs = x[0]
    fin = norm_final.reshape(1, -1)
    mod = _ada_mod(c, w_ada[0], b_ada[0]).reshape(9, D_MODEL)
    xs = _ffn(xs, mod, norm_ffn1[0].reshape(1, -1), w1_gate[0].astype(_BF), w1_up[0].astype(_BF),
              w1_down[0].astype(_BF), fin, mod_row=0, final_norm=False)
    o_mla, o_nsa = _mixer_heads(xs, mod, norm_mix[0], w_in[0], mla_q_norm[0], w_uq[0], mla_kv_norm[0],
                                w_ukv[0], pe_cmp_k[0], w_cmp_k1[0], w_cmp_k2[0], pe_cmp_v[0],
                                w_cmp_v1[0], w_cmp_v2[0], rel_bias)
    xs = _outproj(xs, mod, o_mla, o_nsa, w_out[0].astype(_BF), mod_row=5)
    xs = _ffn(xs, mod, norm_ffn2[0].reshape(1, -1), w2_gate[0].astype(_BF), w2_up[0].astype(_BF),
              w2_down[0].astype(_BF), fin, mod_row=6, final_norm=True)
    return xs[None]
```

```python
import functools
import math

import numpy as np
import jax
import jax.numpy as jnp
from jax import lax
from jax.experimental import pallas as pl
from jax.experimental.pallas import tpu as pltpu

D_MODEL = 2048
D_FF = 5632
EPS = 1e-6
MLA_HEADS = 8
MLA_Q_RANK = 768
MLA_KV_RANK = 512
MLA_NOPE = 128
MLA_ROPE = 64
MLA_V = 128
ROPE_THETA = 10000.0
NSA_HEADS = 8
NSA_GROUPS = 2
NSA_HPG = NSA_HEADS // NSA_GROUPS
NSA_DK = 192
NSA_DV = 128
CMP_LEN = 32
CMP_STRIDE = 16
CMP_HIDDEN = 256
SLC_LEN = 64
SLC_SHIFT = 6
SLC_TOPN = 16
WINDOW = 512
N_BRANCH = 3
REL_BUCKETS = 32
REL_MAX_DIST = 128

LANES = 128
VMEM_LIMIT = 56 * 1024 * 1024

DKP = 256
ATT_TILE = 256
CMP_PAD = 128
N_SLC_PAD = 128

_BF = jnp.bfloat16
_F32 = jnp.float32
_NEG_INF = float("-inf")
LOG2_E = math.log2(math.e)

COL_CQ = 0
COL_GATE = 768
COL_KR = 896
COL_CKV = 1024
COL_KR_ROT = 1536
COL_QN = 2048
COL_KN = COL_QN + NSA_HEADS * DKP
COL_VN = COL_KN + N_BRANCH * NSA_GROUPS * DKP
D_IN_PAD = COL_VN + N_BRANCH * NSA_GROUPS * NSA_DV


def _dot(a, b):
    return jnp.dot(a, b, preferred_element_type=_F32)


def _dot_nt(a, b):
    return lax.dot_general(a, b, (((1,), (1,)), ((), ())), preferred_element_type=_F32)


def _rms(x, gain):
    return x * lax.rsqrt(jnp.mean(x * x, axis=-1, keepdims=True) + EPS) * gain


def _params(semantics):
    return pltpu.CompilerParams(dimension_semantics=semantics, vmem_limit_bytes=VMEM_LIMIT)


ADA_ROWS = 256


def _ada_kernel(c_ref, w_ref, b_ref, o_ref):
    tn = o_ref.shape[1]

    def body(k, acc):
        r = pl.multiple_of(k * ADA_ROWS, ADA_ROWS)
        c = c_ref[pl.ds(r, ADA_ROWS), :]
        s = c * jax.nn.sigmoid(c)
        prod = w_ref[pl.ds(r, ADA_ROWS), :] * s
        return acc + jnp.sum(prod.reshape(ADA_ROWS // 8, 8, tn), axis=0)

    acc = lax.fori_loop(0, D_MODEL // ADA_ROWS, body, jnp.zeros((8, tn), _F32))
    o_ref[...] = jnp.sum(acc, axis=0, keepdims=True) + b_ref[...]


def _ada_mod(c, w_ada, b_ada):
    n = w_ada.shape[1]
    tn = 1024
    return pl.pallas_call(
        _ada_kernel,
        out_shape=jax.ShapeDtypeStruct((1, n), _F32),
        grid=(n // tn,),
        in_specs=[
            pl.BlockSpec((D_MODEL, 1), lambda j: (0, 0)),
            pl.BlockSpec((D_MODEL, tn), lambda j: (0, j)),
            pl.BlockSpec((1, tn), lambda j: (0, j)),
        ],
        out_specs=pl.BlockSpec((1, tn), lambda j: (0, j)),
        compiler_params=_params(("arbitrary",)),
        name="ada_mod",
    )(c.reshape(D_MODEL, 1), w_ada, b_ada.reshape(1, n))


def _ffn_kernel(x_ref, mod_ref, gain_ref, wg_ref, wu_ref, wd_ref, fin_ref, o_ref, h_ref,
                *, mod_row, final_norm):
    j = pl.program_id(1)

    @pl.when(j == 0)
    def _():
        y = _rms(x_ref[...], gain_ref[...])
        h = y * (1.0 + mod_ref[mod_row + 1:mod_row + 2, :]) + mod_ref[mod_row:mod_row + 1, :]
        h_ref[...] = h.astype(_BF)
        o_ref[...] = jnp.zeros_like(o_ref)

    h = h_ref[...]
    g = _dot(h, wg_ref[...])
    u = _dot(h, wu_ref[...])
    a = (g * jax.nn.sigmoid(g)) * u
    o_ref[...] += _dot(a.astype(_BF), wd_ref[...])

    @pl.when(j == pl.num_programs(1) - 1)
    def _():
        x2 = x_ref[...] + (0.5 * mod_ref[mod_row + 2:mod_row + 3, :]) * o_ref[...]
        if final_norm:
            x2 = _rms(x2, fin_ref[...])
        o_ref[...] = x2


def _ffn(x, mod, gain, wg, wu, wd, fin, *, mod_row, final_norm, tm=512, tf=512):
    s = x.shape[0]
    kern = functools.partial(_ffn_kernel, mod_row=mod_row, final_norm=final_norm)
    return pl.pallas_call(
        kern,
        out_shape=jax.ShapeDtypeStruct((s, D_MODEL), _F32),
        grid=(s // tm, D_FF // tf),
        in_specs=[
            pl.BlockSpec((tm, D_MODEL), lambda i, j: (i, 0)),
            pl.BlockSpec((9, D_MODEL), lambda i, j: (0, 0)),
            pl.BlockSpec((1, D_MODEL), lambda i, j: (0, 0)),
            pl.BlockSpec((D_MODEL, tf), lambda i, j: (0, j)),
            pl.BlockSpec((D_MODEL, tf), lambda i, j: (0, j)),
            pl.BlockSpec((tf, D_MODEL), lambda i, j: (j, 0)),
            pl.BlockSpec((1, D_MODEL), lambda i, j: (0, 0)),
        ],
        out_specs=pl.BlockSpec((tm, D_MODEL), lambda i, j: (i, 0)),
        scratch_shapes=[pltpu.VMEM((tm, D_MODEL), _BF)],
        compiler_params=_params(("parallel", "arbitrary")),
        name="ffn_final" if final_norm else "ffn",
    )(x, mod, gain, wg, wu, wd, fin)


def _inproj_kernel(x_ref, mod_ref, gain_ref, w_ref, cs_ref, o_ref, h_ref, *, mod_row):
    @pl.when(pl.program_id(1) == 0)
    def _():
        y = _rms(x_ref[...], gain_ref[...])
        h = y * (1.0 + mod_ref[mod_row + 1:mod_row + 2, :]) + mod_ref[mod_row:mod_row + 1, :]
        h_ref[...] = h.astype(_BF)

    o_ref[...] = (_dot(h_ref[...], w_ref[...]) * cs_ref[...]).astype(o_ref.dtype)


def _inproj(x, mod, gain, w_pad, colscale, *, mod_row, tm=512, tn=1280):
    s = x.shape[0]
    n = w_pad.shape[1]
    return pl.pallas_call(
        functools.partial(_inproj_kernel, mod_row=mod_row),
        out_shape=jax.ShapeDtypeStruct((s, n), _BF),
        grid=(s // tm, n // tn),
        in_specs=[
            pl.BlockSpec((tm, D_MODEL), lambda i, j: (i, 0)),
            pl.BlockSpec((9, D_MODEL), lambda i, j: (0, 0)),
            pl.BlockSpec((1, D_MODEL), lambda i, j: (0, 0)),
            pl.BlockSpec((D_MODEL, tn), lambda i, j: (0, j)),
            pl.BlockSpec((1, tn), lambda i, j: (0, j)),
        ],
        out_specs=pl.BlockSpec((tm, tn), lambda i, j: (i, j)),
        scratch_shapes=[pltpu.VMEM((tm, D_MODEL), _BF)],
        compiler_params=_params(("parallel", "arbitrary")),
        name="inproj",
    )(x, mod, gain, w_pad, colscale)


def _mla_prep_kernel(cq_ref, ckv_ref, kr_ref, krr_ref, gq_ref, gkv_ref, wq_ref, wkv_ref,
                     cos_ref, sin_ref, q_ref, k_ref, v_ref):
    scale = (MLA_NOPE + MLA_ROPE) ** -0.5 * LOG2_E
    cos = cos_ref[...]
    sin = sin_ref[...]
    hq = _rms(cq_ref[...].astype(_F32), gq_ref[...]).astype(_BF)
    qa = _dot(hq, wq_ref[...])
    nh = MLA_HEADS * LANES
    for h in range(MLA_HEADS):
        lo = h * LANES
        nope = qa[:, lo:lo + LANES]
        a = qa[:, nh + lo:nh + lo + LANES]
        b = qa[:, 2 * nh + lo:2 * nh + lo + LANES]
        q_ref[:, h * DKP:h * DKP + LANES] = (nope * scale).astype(_BF)
        q_ref[:, h * DKP + LANES:(h + 1) * DKP] = ((a * cos + b * sin) * scale).astype(_BF)
    hkv = _rms(ckv_ref[...].astype(_F32), gkv_ref[...]).astype(_BF)
    kv = _dot(hkv, wkv_ref[...])
    k_rope = (kr_ref[...].astype(_F32) * cos + krr_ref[...].astype(_F32) * sin).astype(_BF)
    for h in range(MLA_HEADS):
        lo = h * (MLA_NOPE + MLA_V)
        k_ref[:, h * DKP:h * DKP + LANES] = kv[:, lo:lo + MLA_NOPE].astype(_BF)
        k_ref[:, h * DKP + LANES:(h + 1) * DKP] = k_rope
        v_ref[:, h * MLA_V:(h + 1) * MLA_V] = kv[:, lo + MLA_NOPE:lo + MLA_NOPE + MLA_V].astype(_BF)


def _mla_prep(proj, gq, gkv, wq, wkv, cos, sin, *, tm=512):
    s = proj.shape[0]
    const = lambda i: (0, 0)
    return pl.pallas_call(
        _mla_prep_kernel,
        out_shape=(
            jax.ShapeDtypeStruct((s, MLA_HEADS * DKP), _BF),
            jax.ShapeDtypeStruct((s, MLA_HEADS * DKP), _BF),
            jax.ShapeDtypeStruct((s, MLA_HEADS * MLA_V), _BF),
        ),
        grid=(s // tm,),
        in_specs=[
            pl.BlockSpec((tm, MLA_Q_RANK), lambda i: (i, COL_CQ // MLA_Q_RANK)),
            pl.BlockSpec((tm, MLA_KV_RANK), lambda i: (i, COL_CKV // MLA_KV_RANK)),
            pl.BlockSpec((tm, LANES), lambda i: (i, COL_KR // LANES)),
            pl.BlockSpec((tm, LANES), lambda i: (i, COL_KR_ROT // LANES)),
            pl.BlockSpec((1, MLA_Q_RANK), const),
            pl.BlockSpec((1, MLA_KV_RANK), const),
            pl.BlockSpec(wq.shape, const),
            pl.BlockSpec(wkv.shape, const),
            pl.BlockSpec((tm, LANES), lambda i: (i, 0)),
            pl.BlockSpec((tm, LANES), lambda i: (i, 0)),
        ],
        out_specs=(
            pl.BlockSpec((tm, MLA_HEADS * DKP), lambda i: (i, 0)),
            pl.BlockSpec((tm, MLA_HEADS * DKP), lambda i: (i, 0)),
            pl.BlockSpec((tm, MLA_HEADS * MLA_V), lambda i: (i, 0)),
        ),
        compiler_params=_params(("parallel",)),
        name="mla_prep",
    )(proj, proj, proj, proj, gq, gkv, wq, wkv, cos, sin)


def _online_update_t(st, vts, m, l, acc):
    t = vts[0].shape[1]
    m_new = jnp.maximum(m, jnp.max(st, axis=0, keepdims=True))
    alpha = jnp.exp2(m - m_new)
    pt = jnp.exp2(st - m_new)
    l = alpha * l + jnp.sum(pt, axis=0, keepdims=True)
    pb = pt.astype(_BF)
    pv = functools.reduce(jnp.add, [_dot(vt, pb[a * t:(a + 1) * t, :]) for a, vt in enumerate(vts)])
    return m_new, l, alpha * acc + pv


def _mla_attn_kernel(q_ref, k_ref, vt_ref, o_ref, *, n_sub):
    t = q_ref.shape[0]
    tk = n_sub * t
    i = pl.program_id(1)
    q = q_ref[...]
    n_full = i // n_sub
    tail_b0 = jnp.maximum(i + 1 - n_sub, 0)

    def logits(c):
        b0 = jnp.where(c < n_full, c * n_sub, tail_b0)
        return _dot_nt(k_ref[pl.ds(pl.multiple_of(b0 * t, t), tk), :], q)

    def values(b0):
        return [vt_ref[0, b0 + a] for a in range(n_sub)]

    def body(j, carry):
        st, m, l, acc = carry
        st_next = logits(j + 1)
        return (st_next,) + _online_update_t(st, values(j * n_sub), m, l, acc)

    init = (logits(0), jnp.full((1, t), -1e30, _F32), jnp.zeros((1, t), _F32), jnp.zeros((MLA_V, t), _F32))
    st, m, l, acc = lax.fori_loop(0, n_full, body, init)
    kpos = tail_b0 * t + lax.broadcasted_iota(jnp.int32, (tk, t), 0)
    qpos = i * t + lax.broadcasted_iota(jnp.int32, (tk, t), 1)
    st = jnp.where((kpos >= n_full * tk) & (kpos <= qpos), st, _NEG_INF)
    _, l, acc = _online_update_t(st, values(tail_b0), m, l, acc)
    o_ref[...] = (acc * (1.0 / l)).T.astype(o_ref.dtype)


def _mla_attn(q, k, v, *, n_sub=4):
    s = q.shape[0]
    t = ATT_TILE
    nb = s // t
    vt = v.reshape(nb, t, MLA_HEADS, MLA_V).transpose(2, 0, 3, 1)
    return pl.pallas_call(
        functools.partial(_mla_attn_kernel, n_sub=n_sub),
        out_shape=jax.ShapeDtypeStruct((s, MLA_HEADS * MLA_V), _BF),
        grid=(MLA_HEADS, nb),
        in_specs=[
            pl.BlockSpec((t, DKP), lambda h, i: (i, h)),
            pl.BlockSpec((s, DKP), lambda h, i: (0, h)),
            pl.BlockSpec((1, nb, MLA_V, t), lambda h, i: (h, 0, 0, 0)),
        ],
        out_specs=pl.BlockSpec((t, MLA_V), lambda h, i: (i, h)),
        compiler_params=_params(("parallel", "arbitrary")),
        name="mla_attn",
    )(q, k, vt)


def _t5_bucket_np(dist):
    n = np.maximum(dist, 0)
    max_exact = REL_BUCKETS // 2
    nf = np.maximum(n, 1).astype(np.float32)
    large = max_exact + (np.log(nf / np.float32(max_exact)) / np.float32(math.log(REL_MAX_DIST / max_exact))
                         * np.float32(REL_BUCKETS - max_exact)).astype(np.int32)
    large = np.minimum(large, REL_BUCKETS - 1)
    return np.where(n < max_exact, n, large).astype(np.int32)


def _bias_table_kernel(rb_ref, bk_ref, o_ref):
    h = pl.program_id(0)
    bk = bk_ref[0]
    base = rb_ref[REL_BUCKETS - 1, h]
    acc = jnp.zeros(bk.shape, _F32)
    for b in range(REL_BUCKETS - 1):
        acc = jnp.where(bk == b, (rb_ref[b, h] - base) * LOG2_E, acc)
    o_ref[0, 0] = acc


def _bias_tables(rel_bias, buckets):
    nt, r, c = buckets.shape
    return pl.pallas_call(
        _bias_table_kernel,
        out_shape=jax.ShapeDtypeStruct((NSA_HEADS, nt, r, c), _F32),
        grid=(NSA_HEADS, nt),
        in_specs=[
            pl.BlockSpec(memory_space=pltpu.SMEM),
            pl.BlockSpec((1, r, c), lambda h, t: (t, 0, 0)),
        ],
        out_specs=pl.BlockSpec((1, 1, r, c), lambda h, t: (h, t, 0, 0)),
        compiler_params=_params(("arbitrary", "arbitrary")),
        name="bias_tables",
    )(rel_bias, jnp.asarray(buckets))


def _window_buckets(t):
    r = np.arange(3 * t)[:, None]
    c = np.arange(t)[None, :]
    return _t5_bucket_np(c - r + 2 * t)[None]


def _cmp_buckets(t):
    r = np.arange(t)[:, None]
    c = np.arange(CMP_PAD)[None, :]
    return _t5_bucket_np(r - (CMP_LEN - 1) - t + CMP_STRIDE * (CMP_PAD - c))[None]


def _compress_kernel(ch_ref, pe_ref, w1_ref, w2_ref, o_ref):
    half = w1_ref.shape[0] // 2
    ch = ch_ref[0]
    a = _dot(ch, w1_ref[:half, :])
    b = _dot(ch, w1_ref[half:, :])
    n = ch.shape[0]
    b_next = pltpu.roll(b, n - 1, axis=0)
    c0 = _dot(pe_ref[...], w1_ref[...])[0:1, :]
    hid = a + b_next + c0
    hid = hid * jax.nn.sigmoid(hid)
    o_ref[0] = _dot(hid.astype(_BF), w2_ref[...]).astype(o_ref.dtype)


def _compress(chunks, pe_flat, w1, w2):
    g, n, kd = chunks.shape
    d = w2.shape[1]
    return pl.pallas_call(
        _compress_kernel,
        out_shape=jax.ShapeDtypeStruct((g, n, d), _BF),
        grid=(g,),
        in_specs=[
            pl.BlockSpec((1, n, kd), lambda i: (i, 0, 0)),
            pl.BlockSpec(pe_flat.shape, lambda i: (0, 0)),
            pl.BlockSpec(w1.shape, lambda i: (0, 0)),
            pl.BlockSpec(w2.shape, lambda i: (0, 0)),
        ],
        out_specs=pl.BlockSpec((1, n, d), lambda i: (i, 0, 0)),
        compiler_params=_params(("arbitrary",)),
        name="nsa_compress",
    )(chunks, pe_flat, w1, w2)


def _nsa_cmp_kernel(q_ref, kc_ref, vc_ref, ov_ref, tc_ref, o_ref, sel_ref):
    t = q_ref.shape[0]
    nc = kc_ref.shape[1] - CMP_PAD
    per_tile = t // CMP_STRIDE
    i = pl.program_id(1)
    thr = (i + 1) * per_tile - CMP_PAD
    near0 = pl.multiple_of((i + 1) * per_tile, per_tile)

    kc_far = kc_ref[0, CMP_PAD:, :]
    vc_far = vc_ref[0, CMP_PAD:, :]
    ov_far = ov_ref[CMP_PAD:, :]
    kc_near = kc_ref[0, pl.ds(near0, CMP_PAD), :]
    vc_near = vc_ref[0, pl.ds(near0, CMP_PAD), :]
    ov_near = ov_ref[pl.ds(near0, CMP_PAD), :]

    far_mask = lax.broadcasted_iota(jnp.int32, (t, nc), 1) < thr
    r = lax.broadcasted_iota(jnp.int32, (t, CMP_PAD), 0)
    c = lax.broadcasted_iota(jnp.int32, (t, CMP_PAD), 1)
    near_mask = (CMP_STRIDE * c <= r - (CMP_LEN - 1) - t + CMP_STRIDE * CMP_PAD) & (c >= -thr)

    imp = jnp.zeros((t, N_SLC_PAD), _F32)
    for h in range(NSA_HPG):
        q = q_ref[:, h * DKP:(h + 1) * DKP]
        sf = jnp.where(far_mask, _dot_nt(q, kc_far), _NEG_INF)
        sn = jnp.where(near_mask, _dot_nt(q, kc_near) + tc_ref[h, 0], _NEG_INF)
        m = jnp.maximum(jnp.max(sf, axis=1, keepdims=True), jnp.max(sn, axis=1, keepdims=True))
        m = jnp.where(m == _NEG_INF, 0.0, m)
        pf = jnp.exp2(sf - m)
        pn = jnp.exp2(sn - m)
        den = jnp.sum(pf, axis=1, keepdims=True) + jnp.sum(pn, axis=1, keepdims=True)
        inv = 1.0 / jnp.maximum(den, 1e-30)
        pf = pf * inv
        pn = pn * inv
        pf_hi = pf.astype(_BF)
        pn_hi = pn.astype(_BF)
        o_ref[:, h * NSA_DV:(h + 1) * NSA_DV] = (_dot(pf_hi, vc_far) + _dot(pn_hi, vc_near)).astype(o_ref.dtype)
        pf_lo = (pf - pf_hi.astype(_F32)).astype(_BF)
        pn_lo = (pn - pn_hi.astype(_F32)).astype(_BF)
        imp = imp + (_dot(pf_hi, ov_far) + _dot(pn_hi, ov_near)) + (_dot(pf_lo, ov_far) + _dot(pn_lo, ov_near))

    tq = i * t + lax.broadcasted_iota(jnp.int32, (t, N_SLC_PAD), 0)
    jj = lax.broadcasted_iota(jnp.int32, (t, N_SLC_PAD), 1)
    cur = tq >> SLC_SHIFT
    valid = jj * SLC_LEN <= tq
    forced = (jj == 0) | ((jj <= cur) & (jj >= cur - 1))
    work = jnp.where(valid, jnp.where(forced, jnp.inf, imp), _NEG_INF)
    lane = jj.astype(_F32)
    sel = jnp.zeros((t, N_SLC_PAD), _F32)
    for _ in range(SLC_TOPN):
        mx = jnp.max(work, axis=1, keepdims=True)
        first = jnp.min(jnp.where(work == mx, lane, float(N_SLC_PAD)), axis=1, keepdims=True)
        hit = lane == first
        sel = jnp.where(hit & (mx > _NEG_INF), 1.0, sel)
        work = jnp.where(hit, _NEG_INF, work)
    sel_ref[0] = sel.astype(sel_ref.dtype)


def _nsa_cmp(proj, kc_pad, vc_pad, ov_pad, tc):
    s = proj.shape[0]
    t = ATT_TILE
    gw = NSA_HPG * DKP
    return pl.pallas_call(
        _nsa_cmp_kernel,
        out_shape=(
            jax.ShapeDtypeStruct((s, NSA_HEADS * NSA_DV), _BF),
            jax.ShapeDtypeStruct((NSA_GROUPS, s, N_SLC_PAD), _BF),
        ),
        grid=(NSA_GROUPS, s // t),
        in_specs=[
            pl.BlockSpec((t, gw), lambda g, i: (i, COL_QN // gw + g)),
            pl.BlockSpec((1,) + kc_pad.shape[1:], lambda g, i: (g, 0, 0)),
            pl.BlockSpec((1,) + vc_pad.shape[1:], lambda g, i: (g, 0, 0)),
            pl.BlockSpec(ov_pad.shape, lambda g, i: (0, 0)),
            pl.BlockSpec((NSA_HPG, 1, t, CMP_PAD), lambda g, i: (g, 0, 0, 0)),
        ],
        out_specs=(
            pl.BlockSpec((t, NSA_HPG * NSA_DV), lambda g, i: (i, g)),
            pl.BlockSpec((1, t, N_SLC_PAD), lambda g, i: (g, i, 0)),
        ),
        compiler_params=_params(("parallel", "arbitrary")),
        name="nsa_cmp",
    )(proj, kc_pad, vc_pad, ov_pad, tc)


def _nsa_main_kernel(q_ref, ks_ref, vst_ref, kw_ref, vwt_ref, selt_ref, tw_ref, oc_ref, gate_ref,
                     o_ref, m_ref, l_ref, acc_ref, *, n_sub):
    t = q_ref.shape[0]
    tk = n_sub * t
    g = pl.program_id(0)
    i = pl.program_id(1)
    blocks_per_tile = t // SLC_LEN
    qs = [q_ref[:, h * DKP:(h + 1) * DKP] for h in range(NSA_HPG)]

    m_ref[...] = jnp.full(m_ref.shape, -1e30, _F32)
    l_ref[...] = jnp.zeros(l_ref.shape, _F32)
    acc_ref[...] = jnp.zeros(acc_ref.shape, _F32)

    def selected(tile0, n_tiles):
        rows = [jnp.broadcast_to(selt_ref[0, pl.ds(tile0 * blocks_per_tile + b, 1), :], (SLC_LEN, t))
                for b in range(n_tiles * blocks_per_tile)]
        return jnp.concatenate(rows, axis=0) > 0.5

    def run_heads(k, vts, mask, bias):
        def logits(h):
            st = _dot_nt(k, qs[h])
            if bias is not None:
                st = st + bias(h)
            return jnp.where(mask, st, _NEG_INF)

        st = logits(0)
        for h in range(NSA_HPG):
            st_next = logits(h + 1) if h + 1 < NSA_HPG else None
            m, l, acc = _online_update_t(st, vts, m_ref[h], l_ref[h], acc_ref[h])
            m_ref[h] = m
            l_ref[h] = l
            acc_ref[h] = acc
            st = st_next

    def far_chunk(tile0, mask_extra):
        r0 = pl.multiple_of(tile0 * t, t)
        mask = selected(tile0, n_sub)
        if mask_extra is not None:
            mask = mask & mask_extra(r0)
        run_heads(ks_ref[pl.ds(r0, tk), :], [vst_ref[0, tile0 + a] for a in range(n_sub)], mask, None)

    n_far = jnp.maximum(i - 1, 0)
    n_full = n_far // n_sub

    def far_body(c, carry):
        far_chunk(c * n_sub, None)
        return carry

    lax.fori_loop(0, n_full, far_body, 0)

    @pl.when(n_far > n_full * n_sub)
    def _():
        def not_yet_covered(r0):
            kpos = r0 + lax.broadcasted_iota(jnp.int32, (tk, t), 0)
            return (kpos >= n_full * tk) & (kpos < n_far * t)

        far_chunk(jnp.maximum(n_far - n_sub, 0), not_yet_covered)

    prev = jnp.maximum(i - 1, 0)
    row2 = lax.broadcasted_iota(jnp.int32, (2 * t, t), 0)
    col2 = lax.broadcasted_iota(jnp.int32, (2 * t, t), 1)
    near_mask = (jnp.concatenate([selected(prev, 1), selected(i, 1)], axis=0)
                 & (col2 - row2 + t >= 0) & (row2 >= jnp.where(i >= 1, 0, t)))
    k_near = jnp.concatenate([ks_ref[pl.ds(pl.multiple_of(prev * t, t), t), :],
                              ks_ref[pl.ds(pl.multiple_of(i * t, t), t), :]], axis=0)
    run_heads(k_near, [vst_ref[0, prev], vst_ref[0, i]], near_mask, lambda h: tw_ref[h, 0, t:, :])

    prev2 = jnp.maximum(i - 2, 0)
    w_tiles = (prev2, prev, i)
    k_win = jnp.concatenate([kw_ref[pl.ds(pl.multiple_of(a * t, t), t), :] for a in w_tiles], axis=0)
    row3 = lax.broadcasted_iota(jnp.int32, (3 * t, t), 0)
    col3 = lax.broadcasted_iota(jnp.int32, (3 * t, t), 1)
    dist = col3 - row3 + 2 * t
    first_row = jnp.where(i >= 2, 0, jnp.where(i >= 1, t, 2 * t))
    win_mask = (dist >= 0) & (dist < WINDOW) & (row3 >= first_row)

    gates_t = jax.nn.sigmoid(gate_ref[...].astype(_F32)).T
    for h in range(NSA_HPG):
        st = jnp.where(win_mask, _dot_nt(k_win, qs[h]) + tw_ref[h, 0], _NEG_INF)
        m = jnp.max(st, axis=0, keepdims=True)
        pt = jnp.exp2(st - m)
        den = jnp.sum(pt, axis=0, keepdims=True)
        pb = pt.astype(_BF)
        o_w = functools.reduce(jnp.add, [_dot(vwt_ref[0, a], pb[n * t:(n + 1) * t, :])
                                         for n, a in enumerate(w_tiles)]) * (1.0 / den)
        o_s = acc_ref[h] * (1.0 / l_ref[h])
        o_c = oc_ref[:, h * NSA_DV:(h + 1) * NSA_DV].astype(_F32).T
        gc, gs, gw = [jnp.where(g == 0, gates_t[h * N_BRANCH + b:h * N_BRANCH + b + 1, :],
                                gates_t[(NSA_HPG + h) * N_BRANCH + b:(NSA_HPG + h) * N_BRANCH + b + 1, :])
                      for b in range(N_BRANCH)]
        o_ref[:, h * NSA_DV:(h + 1) * NSA_DV] = (gc * o_c + gs * o_s + gw * o_w).T.astype(o_ref.dtype)


def _tiles_transposed(v, t):
    s, d = v.shape
    return v.reshape(s // t, t, d).transpose(0, 2, 1)


def _nsa_main(proj, sel, tw, o_c, *, n_sub=4):
    s = proj.shape[0]
    t = ATT_TILE
    nb = s // t
    gw = NSA_HPG * DKP
    kn0 = COL_KN // DKP

    def values_t(branch):
        c0 = COL_VN + branch * NSA_GROUPS * NSA_DV
        return jnp.stack([_tiles_transposed(proj[:, c0 + g * NSA_DV:c0 + (g + 1) * NSA_DV], t)
                          for g in range(NSA_GROUPS)])

    sel_t = sel.astype(_F32).transpose(0, 2, 1)
    return pl.pallas_call(
        functools.partial(_nsa_main_kernel, n_sub=n_sub),
        out_shape=jax.ShapeDtypeStruct((s, NSA_HEADS * NSA_DV), _BF),
        grid=(NSA_GROUPS, nb),
        in_specs=[
            pl.BlockSpec((t, gw), lambda g, i: (i, COL_QN // gw + g)),
            pl.BlockSpec((s, DKP), lambda g, i: (0, kn0 + 1 * NSA_GROUPS + g)),
            pl.BlockSpec((1, nb, NSA_DV, t), lambda g, i: (g, 0, 0, 0)),
            pl.BlockSpec((s, DKP), lambda g, i: (0, kn0 + 2 * NSA_GROUPS + g)),
            pl.BlockSpec((1, nb, NSA_DV, t), lambda g, i: (g, 0, 0, 0)),
            pl.BlockSpec((1, N_SLC_PAD, t), lambda g, i: (g, 0, i)),
            pl.BlockSpec((NSA_HPG, 1, 3 * t, t), lambda g, i: (g, 0, 0, 0)),
            pl.BlockSpec((t, NSA_HPG * NSA_DV), lambda g, i: (i, g)),
            pl.BlockSpec((t, LANES), lambda g, i: (i, COL_GATE // LANES)),
        ],
        out_specs=pl.BlockSpec((t, NSA_HPG * NSA_DV), lambda g, i: (i, g)),
        scratch_shapes=[
            pltpu.VMEM((NSA_HPG, 1, t), _F32),
            pltpu.VMEM((NSA_HPG, 1, t), _F32),
            pltpu.VMEM((NSA_HPG, NSA_DV, t), _F32),
        ],
        compiler_params=_params(("parallel", "arbitrary")),
        name="nsa_main",
    )(proj, proj, values_t(1), proj, values_t(2), sel_t, tw, o_c, proj)


def _outproj_kernel(x_ref, mod_ref, om_ref, on_ref, w_ref, o_ref, *, mod_row):
    half = om_ref.shape[1]
    y = _dot(om_ref[...], w_ref[:half, :]) + _dot(on_ref[...], w_ref[half:, :])
    o_ref[...] = x_ref[...] + mod_ref[mod_row:mod_row + 1, :] * y


def _outproj(x, mod, o_mla, o_nsa, w_out, *, mod_row, tm=512):
    s = x.shape[0]
    return pl.pallas_call(
        functools.partial(_outproj_kernel, mod_row=mod_row),
        out_shape=jax.ShapeDtypeStruct((s, D_MODEL), _F32),
        grid=(s // tm,),
        in_specs=[
            pl.BlockSpec((tm, D_MODEL), lambda i: (i, 0)),
            pl.BlockSpec((9, D_MODEL), lambda i: (0, 0)),
            pl.BlockSpec((tm, o_mla.shape[1]), lambda i: (i, 0)),
            pl.BlockSpec((tm, o_nsa.shape[1]), lambda i: (i, 0)),
            pl.BlockSpec(w_out.shape, lambda i: (0, 0)),
        ],
        out_specs=pl.BlockSpec((tm, D_MODEL), lambda i: (i, 0)),
        compiler_params=_params(("parallel",)),
        name="outproj",
    )(x, mod, o_mla, o_nsa, w_out)


def _pad_cols(w, width):
    return jnp.pad(w, ((0, 0), (0, width - w.shape[1])))


def _pad_last(w, width):
    return jnp.pad(w, [(0, 0)] * (w.ndim - 1) + [(0, width - w.shape[-1])])


def _rotate_half_cols(w):
    half = MLA_ROPE // 2
    return jnp.concatenate([-w[..., half:], w[..., :half]], axis=-1)


def _w_in_padded(w_in):
    d = w_in.shape[0]
    sizes = (MLA_Q_RANK, MLA_KV_RANK, MLA_ROPE, NSA_HEADS * NSA_DK,
             N_BRANCH * NSA_GROUPS * NSA_DK, N_BRANCH * NSA_GROUPS * NSA_DV, NSA_HEADS * N_BRANCH)
    offs = np.concatenate([[0], np.cumsum(sizes)])
    cq, ckv, kr, qn, kn, vn, gn = [w_in[:, offs[a]:offs[a + 1]] for a in range(len(sizes))]
    qn = _pad_last(qn.reshape(d, NSA_HEADS, NSA_DK), DKP).reshape(d, NSA_HEADS * DKP)
    kn = _pad_last(kn.reshape(d, N_BRANCH * NSA_GROUPS, NSA_DK), DKP).reshape(d, -1)
    cols = [cq, _pad_cols(gn, LANES), _pad_cols(kr, LANES), ckv,
            _pad_cols(_rotate_half_cols(kr), LANES),
            jnp.zeros((d, COL_QN - COL_KR_ROT - LANES), w_in.dtype), qn, kn, vn]
    w = jnp.concatenate(cols, axis=1)
    assert w.shape[1] == D_IN_PAD
    colscale = np.ones((1, D_IN_PAD), np.float32)
    colscale[:, COL_QN:COL_KN] = NSA_DK ** -0.5 * LOG2_E
    return w.astype(_BF), jnp.asarray(colscale)


def _w_uq_padded(w_uq):
    r = w_uq.shape[0]
    w = w_uq.reshape(r, MLA_HEADS, MLA_NOPE + MLA_ROPE)
    nope = w[..., :MLA_NOPE].reshape(r, -1)
    rope = w[..., MLA_NOPE:]
    rope_p = _pad_last(rope, LANES).reshape(r, -1)
    rot_p = _pad_last(_rotate_half_cols(rope), LANES).reshape(r, -1)
    return jnp.concatenate([nope, rope_p, rot_p], axis=1).astype(_BF)


def _cmp_w1_padded(w1, d, dpad):
    return _pad_last(w1.reshape(CMP_LEN, d, CMP_HIDDEN).transpose(0, 2, 1), dpad) \
        .transpose(0, 2, 1).reshape(CMP_LEN * dpad, CMP_HIDDEN).astype(_BF)


def _rope_tables(s):
    inv = ROPE_THETA ** (-jnp.arange(0, MLA_ROPE, 2, dtype=_F32) / MLA_ROPE)
    ang = jnp.arange(s, dtype=_F32)[:, None] * inv[None, :]
    cos = _pad_cols(jnp.tile(jnp.cos(ang), (1, 2)), LANES)
    sin = _pad_cols(jnp.tile(jnp.sin(ang), (1, 2)), LANES)
    return cos, sin


def _overlap_padded(s):
    n_cmp_rows = s // CMP_STRIDE
    c_start = np.arange(n_cmp_rows)[:, None] * CMP_STRIDE
    s_start = np.arange(N_SLC_PAD)[None, :] * SLC_LEN
    ov = ((c_start < s_start + SLC_LEN) & (c_start + CMP_LEN > s_start)).astype(np.float32)
    return jnp.asarray(np.concatenate([np.zeros((CMP_PAD, N_SLC_PAD), np.float32), ov]), _BF)


def _mixer_heads(x, mod, norm_mix, w_in, mla_q_norm, w_uq, mla_kv_norm, w_ukv, pe_cmp_k, w_cmp_k1,
                 w_cmp_k2, pe_cmp_v, w_cmp_v1, w_cmp_v2, rel_bias):
    s = x.shape[0]
    assert s % ATT_TILE == 0 and s // SLC_LEN <= N_SLC_PAD and s >= 3 * ATT_TILE
    w_pad, colscale = _w_in_padded(w_in)
    proj = _inproj(x, mod, norm_mix.reshape(1, -1), w_pad, colscale, mod_row=3)

    cos, sin = _rope_tables(s)
    q, k, v = _mla_prep(proj, mla_q_norm.reshape(1, -1), mla_kv_norm.reshape(1, -1),
                        _w_uq_padded(w_uq), w_ukv.astype(_BF), cos, sin)
    o_mla = _mla_attn(q, k, v)

    n_chunks = s // CMP_STRIDE
    kn0 = COL_KN
    k_cmp = jnp.stack([proj[:, kn0 + g * DKP:kn0 + (g + 1) * DKP] for g in range(NSA_GROUPS)])
    v_cmp = jnp.stack([proj[:, COL_VN + g * NSA_DV:COL_VN + (g + 1) * NSA_DV] for g in range(NSA_GROUPS)])
    pe_k = jnp.broadcast_to(_pad_last(pe_cmp_k, DKP).reshape(1, -1), (8, CMP_LEN * DKP)).astype(_BF)
    pe_v = jnp.broadcast_to(pe_cmp_v.reshape(1, -1), (8, CMP_LEN * NSA_DV)).astype(_BF)
    kc = _compress(k_cmp.reshape(NSA_GROUPS, n_chunks, CMP_STRIDE * DKP), pe_k,
                   _cmp_w1_padded(w_cmp_k1, NSA_DK, DKP), _pad_cols(w_cmp_k2, DKP).astype(_BF))
    vc = _compress(v_cmp.reshape(NSA_GROUPS, n_chunks, CMP_STRIDE * NSA_DV), pe_v,
                   w_cmp_v1.astype(_BF), w_cmp_v2.astype(_BF))
    front = ((0, 0), (CMP_PAD, 0), (0, 0))
    tc = _bias_tables(rel_bias, _cmp_buckets(ATT_TILE))
    tw = _bias_tables(rel_bias, _window_buckets(ATT_TILE))
    o_c, sel = _nsa_cmp(proj, jnp.pad(kc, front), jnp.pad(vc, front), _overlap_padded(s), tc)
    o_nsa = _nsa_main(proj, sel, tw, o_c)
    return o_mla, o_nsa


def kernel(x, c, w_ada, b_ada, norm_ffn1, w1_gate, w1_up, w1_down, norm_mix, w_in, mla_q_norm, w_uq, mla_kv_norm, w_ukv, pe_cmp_k, w_cmp_k1, w_cmp_k2, pe_cmp_v, w_cmp_v1, w_cmp_v2, rel_bias, w_out, norm_ffn2, w2_gate, w2_up, w2_down, norm_final):
    assert x.shape[0] == 1 and w_ada.shape[0] == 1
    xs = x[0]
    fin = norm_final.reshape(1, -1)
    mod = _ada_mod(c, w_ada[0], b_ada[0]).reshape(9, D_MODEL)
    xs = _ffn(xs, mod, norm_ffn1[0].reshape(1, -1), w1_gate[0].astype(_BF), w1_up[0].astype(_BF),
              w1_down[0].astype(_BF), fin, mod_row=0, final_norm=False)
    o_mla, o_nsa = _mixer_heads(xs, mod, norm_mix[0], w_in[0], mla_q_norm[0], w_uq[0], mla_kv_norm[0],
                                w_ukv[0], pe_cmp_k[0], w_cmp_k1[0], w_cmp_k2[0], pe_cmp_v[0],
                                w_cmp_v1[0], w_cmp_v2[0], rel_bias)
    xs = _outproj(xs, mod, o_mla, o_nsa, w_out[0].astype(_BF), mod_row=5)
    xs = _ffn(xs, mod, norm_ffn2[0].reshape(1, -1), w2_gate[0].astype(_BF), w2_up[0].astype(_BF),
              w2_down[0].astype(_BF), fin, mod_row=6, final_norm=True)
    return xs[None]
```

```python
import functools
import math

import numpy as np
import jax
import jax.numpy as jnp
from jax import lax
from jax.experimental import pallas as pl
from jax.experimental.pallas import tpu as pltpu

D_MODEL = 2048
D_FF = 5632
EPS = 1e-6
MLA_HEADS = 8
MLA_Q_RANK = 768
MLA_KV_RANK = 512
MLA_NOPE = 128
MLA_ROPE = 64
MLA_V = 128
ROPE_THETA = 10000.0
NSA_HEADS = 8
NSA_GROUPS = 2
NSA_HPG = NSA_HEADS // NSA_GROUPS
NSA_DK = 192
NSA_DV = 128
CMP_LEN = 32
CMP_STRIDE = 16
CMP_HIDDEN = 256
SLC_LEN = 64
SLC_SHIFT = 6
SLC_TOPN = 16
WINDOW = 512
N_BRANCH = 3
REL_BUCKETS = 32
REL_MAX_DIST = 128

LANES = 128
VMEM_LIMIT = 56 * 1024 * 1024

DKP = 256
ATT_TILE = 256
CMP_PAD = 128
N_SLC_PAD = 128

_BF = jnp.bfloat16
_F32 = jnp.float32
_NEG_INF = float("-inf")
LOG2_E = math.log2(math.e)

COL_CQ = 0
COL_GATE = 768
COL_KR = 896
COL_CKV = 1024
COL_KR_ROT = 1536
COL_QN = 2048
COL_KN = COL_QN + NSA_HEADS * DKP
COL_VN = COL_KN + N_BRANCH * NSA_GROUPS * DKP
D_IN_PAD = COL_VN + N_BRANCH * NSA_GROUPS * NSA_DV


def _dot(a, b):
    return jnp.dot(a, b, preferred_element_type=_F32)


def _dot_nt(a, b):
    return lax.dot_general(a, b, (((1,), (1,)), ((), ())), preferred_element_type=_F32)


def _rms(x, gain):
    return x * lax.rsqrt(jnp.mean(x * x, axis=-1, keepdims=True) + EPS) * gain


def _params(semantics):
    return pltpu.CompilerParams(dimension_semantics=semantics, vmem_limit_bytes=VMEM_LIMIT)


ADA_ROWS = 256


def _ada_kernel(c_ref, w_ref, b_ref, o_ref):
    tn = o_ref.shape[1]

    def body(k, acc):
        r = pl.multiple_of(k * ADA_ROWS, ADA_ROWS)
        c = c_ref[pl.ds(r, ADA_ROWS), :]
        s = c * jax.nn.sigmoid(c)
        prod = w_ref[pl.ds(r, ADA_ROWS), :] * s
        return acc + jnp.sum(prod.reshape(ADA_ROWS // 8, 8, tn), axis=0)

    acc = lax.fori_loop(0, D_MODEL // ADA_ROWS, body, jnp.zeros((8, tn), _F32))
    o_ref[...] = jnp.sum(acc, axis=0, keepdims=True) + b_ref[...]


def _ada_mod(c, w_ada, b_ada):
    n = w_ada.shape[1]
    tn = 1024
    return pl.pallas_call(
        _ada_kernel,
        out_shape=jax.ShapeDtypeStruct((1, n), _F32),
        grid=(n // tn,),
        in_specs=[
            pl.BlockSpec((D_MODEL, 1), lambda j: (0, 0)),
            pl.BlockSpec((D_MODEL, tn), lambda j: (0, j)),
            pl.BlockSpec((1, tn), lambda j: (0, j)),
        ],
        out_specs=pl.BlockSpec((1, tn), lambda j: (0, j)),
        compiler_params=_params(("arbitrary",)),
        name="ada_mod",
    )(c.reshape(D_MODEL, 1), w_ada, b_ada.reshape(1, n))


def _ffn_kernel(x_ref, mod_ref, gain_ref, wg_ref, wu_ref, wd_ref, fin_ref, o_ref, h_ref,
                *, mod_row, final_norm):
    j = pl.program_id(1)

    @pl.when(j == 0)
    def _():
        y = _rms(x_ref[...], gain_ref[...])
        h = y * (1.0 + mod_ref[mod_row + 1:mod_row + 2, :]) + mod_ref[mod_row:mod_row + 1, :]
        h_ref[...] = h.astype(_BF)
        o_ref[...] = jnp.zeros_like(o_ref)

    h = h_ref[...]
    g = _dot(h, wg_ref[...])
    u = _dot(h, wu_ref[...])
    a = (g * jax.nn.sigmoid(g)) * u
    o_ref[...] += _dot(a.astype(_BF), wd_ref[...])

    @pl.when(j == pl.num_programs(1) - 1)
    def _():
        x2 = x_ref[...] + (0.5 * mod_ref[mod_row + 2:mod_row + 3, :]) * o_ref[...]
        if final_norm:
            x2 = _rms(x2, fin_ref[...])
        o_ref[...] = x2


def _ffn(x, mod, gain, wg, wu, wd, fin, *, mod_row, final_norm, tm=1024, tf=256):
    s = x.shape[0]
    tm = min(tm, s)
    kern = functools.partial(_ffn_kernel, mod_row=mod_row, final_norm=final_norm)
    return pl.pallas_call(
        kern,
        out_shape=jax.ShapeDtypeStruct((s, D_MODEL), _F32),
        grid=(s // tm, D_FF // tf),
        in_specs=[
            pl.BlockSpec((tm, D_MODEL), lambda i, j: (i, 0)),
            pl.BlockSpec((9, D_MODEL), lambda i, j: (0, 0)),
            pl.BlockSpec((1, D_MODEL), lambda i, j: (0, 0)),
            pl.BlockSpec((D_MODEL, tf), lambda i, j: (0, j)),
            pl.BlockSpec((D_MODEL, tf), lambda i, j: (0, j)),
            pl.BlockSpec((tf, D_MODEL), lambda i, j: (j, 0)),
            pl.BlockSpec((1, D_MODEL), lambda i, j: (0, 0)),
        ],
        out_specs=pl.BlockSpec((tm, D_MODEL), lambda i, j: (i, 0)),
        scratch_shapes=[pltpu.VMEM((tm, D_MODEL), _BF)],
        compiler_params=_params(("parallel", "arbitrary")),
        name="ffn_final" if final_norm else "ffn",
    )(x, mod, gain, wg, wu, wd, fin)


def _inproj_kernel(x_ref, mod_ref, gain_ref, w_ref, cs_ref, o_ref, h_ref, *, mod_row):
    @pl.when(pl.program_id(1) == 0)
    def _():
        y = _rms(x_ref[...], gain_ref[...])
        h = y * (1.0 + mod_ref[mod_row + 1:mod_row + 2, :]) + mod_ref[mod_row:mod_row + 1, :]
        h_ref[...] = h.astype(_BF)

    o_ref[...] = (_dot(h_ref[...], w_ref[...]) * cs_ref[...]).astype(o_ref.dtype)


def _inproj(x, mod, gain, w_pad, colscale, *, mod_row, tm=1024, tn=1280):
    s = x.shape[0]
    tm = min(tm, s)
    n = w_pad.shape[1]
    return pl.pallas_call(
        functools.partial(_inproj_kernel, mod_row=mod_row),
        out_shape=jax.ShapeDtypeStruct((s, n), _BF),
        grid=(s // tm, n // tn),
        in_specs=[
            pl.BlockSpec((tm, D_MODEL), lambda i, j: (i, 0)),
            pl.BlockSpec((9, D_MODEL), lambda i, j: (0, 0)),
            pl.BlockSpec((1, D_MODEL), lambda i, j: (0, 0)),
            pl.BlockSpec((D_MODEL, tn), lambda i, j: (0, j)),
            pl.BlockSpec((1, tn), lambda i, j: (0, j)),
        ],
        out_specs=pl.BlockSpec((tm, tn), lambda i, j: (i, j)),
        scratch_shapes=[pltpu.VMEM((tm, D_MODEL), _BF)],
        compiler_params=_params(("parallel", "arbitrary")),
        name="inproj",
    )(x, mod, gain, w_pad, colscale)


def _mla_prep_kernel(cq_ref, ckv_ref, kr_ref, krr_ref, gq_ref, gkv_ref, wq_ref, wkv_ref,
                     cos_ref, sin_ref, q_ref, k_ref, v_ref):
    scale = (MLA_NOPE + MLA_ROPE) ** -0.5 * LOG2_E
    cos = cos_ref[...]
    sin = sin_ref[...]
    hq = _rms(cq_ref[...].astype(_F32), gq_ref[...]).astype(_BF)
    qa = _dot(hq, wq_ref[...])
    nh = MLA_HEADS * LANES
    for h in range(MLA_HEADS):
        lo = h * LANES
        nope = qa[:, lo:lo + LANES]
        a = qa[:, nh + lo:nh + lo + LANES]
        b = qa[:, 2 * nh + lo:2 * nh + lo + LANES]
        q_ref[:, h * DKP:h * DKP + LANES] = (nope * scale).astype(_BF)
        q_ref[:, h * DKP + LANES:(h + 1) * DKP] = ((a * cos + b * sin) * scale).astype(_BF)
    hkv = _rms(ckv_ref[...].astype(_F32), gkv_ref[...]).astype(_BF)
    kv = _dot(hkv, wkv_ref[...])
    k_rope = (kr_ref[...].astype(_F32) * cos + krr_ref[...].astype(_F32) * sin).astype(_BF)
    for h in range(MLA_HEADS):
        lo = h * (MLA_NOPE + MLA_V)
        k_ref[:, h * DKP:h * DKP + LANES] = kv[:, lo:lo + MLA_NOPE].astype(_BF)
        k_ref[:, h * DKP + LANES:(h + 1) * DKP] = k_rope
        v_ref[:, h * MLA_V:(h + 1) * MLA_V] = kv[:, lo + MLA_NOPE:lo + MLA_NOPE + MLA_V].astype(_BF)


def _mla_prep(proj, gq, gkv, wq, wkv, cos, sin, *, tm=512):
    s = proj.shape[0]
    const = lambda i: (0, 0)
    return pl.pallas_call(
        _mla_prep_kernel,
        out_shape=(
            jax.ShapeDtypeStruct((s, MLA_HEADS * DKP), _BF),
            jax.ShapeDtypeStruct((s, MLA_HEADS * DKP), _BF),
            jax.ShapeDtypeStruct((s, MLA_HEADS * MLA_V), _BF),
        ),
        grid=(s // tm,),
        in_specs=[
            pl.BlockSpec((tm, MLA_Q_RANK), lambda i: (i, COL_CQ // MLA_Q_RANK)),
            pl.BlockSpec((tm, MLA_KV_RANK), lambda i: (i, COL_CKV // MLA_KV_RANK)),
            pl.BlockSpec((tm, LANES), lambda i: (i, COL_KR // LANES)),
            pl.BlockSpec((tm, LANES), lambda i: (i, COL_KR_ROT // LANES)),
            pl.BlockSpec((1, MLA_Q_RANK), const),
            pl.BlockSpec((1, MLA_KV_RANK), const),
            pl.BlockSpec(wq.shape, const),
            pl.BlockSpec(wkv.shape, const),
            pl.BlockSpec((tm, LANES), lambda i: (i, 0)),
            pl.BlockSpec((tm, LANES), lambda i: (i, 0)),
        ],
        out_specs=(
            pl.BlockSpec((tm, MLA_HEADS * DKP), lambda i: (i, 0)),
            pl.BlockSpec((tm, MLA_HEADS * DKP), lambda i: (i, 0)),
            pl.BlockSpec((tm, MLA_HEADS * MLA_V), lambda i: (i, 0)),
        ),
        compiler_params=_params(("parallel",)),
        name="mla_prep",
    )(proj, proj, proj, proj, gq, gkv, wq, wkv, cos, sin)


def _online_update_t(st, vts, m, l, acc):
    t = vts[0].shape[1]
    m_new = jnp.maximum(m, jnp.max(st, axis=0, keepdims=True))
    alpha = jnp.exp2(m - m_new)
    pt = jnp.exp2(st - m_new)
    l = alpha * l + jnp.sum(pt, axis=0, keepdims=True)
    pb = pt.astype(_BF)
    pv = functools.reduce(jnp.add, [_dot(vt, pb[a * t:(a + 1) * t, :]) for a, vt in enumerate(vts)])
    return m_new, l, alpha * acc + pv


def _mla_attn_kernel(q_ref, k_ref, vt_ref, o_ref, m_ref, l_ref, acc_ref, *, n_sub):
    t = q_ref.shape[0]
    tk = n_sub * t
    hp = vt_ref.shape[0]
    i = pl.program_id(1)
    qs = [q_ref[:, h * DKP:(h + 1) * DKP] for h in range(hp)]
    n_full = i // n_sub
    tail_b0 = jnp.maximum(i + 1 - n_sub, 0)

    m_ref[...] = jnp.full(m_ref.shape, -1e30, _F32)
    l_ref[...] = jnp.zeros(l_ref.shape, _F32)
    acc_ref[...] = jnp.zeros(acc_ref.shape, _F32)

    def first_tile(c):
        return jnp.where(c < n_full, c * n_sub, tail_b0)

    def logits(c, h):
        r0 = pl.multiple_of(first_tile(c) * t, t)
        return _dot_nt(k_ref[pl.ds(r0, tk), h * DKP:(h + 1) * DKP], qs[h])

    def update(c, h, st):
        b0 = first_tile(c)
        m, l, acc = _online_update_t(st, [vt_ref[h, b0 + a] for a in range(n_sub)],
                                     m_ref[h], l_ref[h], acc_ref[h])
        m_ref[h] = m
        l_ref[h] = l
        acc_ref[h] = acc

    def body(c, st):
        for h in range(hp):
            st_next = logits(c, h + 1) if h + 1 < hp else logits(c + 1, 0)
            update(c, h, st)
            st = st_next
        return st

    st = lax.fori_loop(0, n_full, body, logits(0, 0))
    kpos = tail_b0 * t + lax.broadcasted_iota(jnp.int32, (tk, t), 0)
    qpos = i * t + lax.broadcasted_iota(jnp.int32, (tk, t), 1)
    tail_mask = (kpos >= n_full * tk) & (kpos <= qpos)
    for h in range(hp):
        st_next = logits(n_full, h + 1) if h + 1 < hp else None
        update(n_full, h, jnp.where(tail_mask, st, _NEG_INF))
        st = st_next
    for h in range(hp):
        o_ref[:, h * MLA_V:(h + 1) * MLA_V] = (acc_ref[h] * (1.0 / l_ref[h])).T.astype(o_ref.dtype)


def _mla_attn(q, k, v, *, n_sub=4, hp=2):
    s = q.shape[0]
    t = ATT_TILE
    nb = s // t
    vt = v.reshape(nb, t, MLA_HEADS, MLA_V).transpose(2, 0, 3, 1)
    return pl.pallas_call(
        functools.partial(_mla_attn_kernel, n_sub=n_sub),
        out_shape=jax.ShapeDtypeStruct((s, MLA_HEADS * MLA_V), _BF),
        grid=(MLA_HEADS // hp, nb),
        in_specs=[
            pl.BlockSpec((t, hp * DKP), lambda h, i: (i, h)),
            pl.BlockSpec((s, hp * DKP), lambda h, i: (0, h)),
            pl.BlockSpec((hp, nb, MLA_V, t), lambda h, i: (h, 0, 0, 0)),
        ],
        out_specs=pl.BlockSpec((t, hp * MLA_V), lambda h, i: (i, h)),
        scratch_shapes=[
            pltpu.VMEM((hp, 1, t), _F32),
            pltpu.VMEM((hp, 1, t), _F32),
            pltpu.VMEM((hp, MLA_V, t), _F32),
        ],
        compiler_params=_params(("parallel", "arbitrary")),
        name="mla_attn",
    )(q, k, vt)


def _t5_bucket_np(dist):
    n = np.maximum(dist, 0)
    max_exact = REL_BUCKETS // 2
    nf = np.maximum(n, 1).astype(np.float32)
    large = max_exact + (np.log(nf / np.float32(max_exact)) / np.float32(math.log(REL_MAX_DIST / max_exact))
                         * np.float32(REL_BUCKETS - max_exact)).astype(np.int32)
    large = np.minimum(large, REL_BUCKETS - 1)
    return np.where(n < max_exact, n, large).astype(np.int32)


def _bias_table_kernel(rb_ref, bk_ref, o_ref):
    h = pl.program_id(0)
    bk = bk_ref[0]
    base = rb_ref[REL_BUCKETS - 1, h]
    acc = jnp.zeros(bk.shape, _F32)
    for b in range(REL_BUCKETS - 1):
        acc = jnp.where(bk == b, (rb_ref[b, h] - base) * LOG2_E, acc)
    o_ref[0, 0] = acc


def _bias_tables(rel_bias, buckets):
    nt, r, c = buckets.shape
    return pl.pallas_call(
        _bias_table_kernel,
        out_shape=jax.ShapeDtypeStruct((NSA_HEADS, nt, r, c), _F32),
        grid=(NSA_HEADS, nt),
        in_specs=[
            pl.BlockSpec(memory_space=pltpu.SMEM),
            pl.BlockSpec((1, r, c), lambda h, t: (t, 0, 0)),
        ],
        out_specs=pl.BlockSpec((1, 1, r, c), lambda h, t: (h, t, 0, 0)),
        compiler_params=_params(("arbitrary", "arbitrary")),
        name="bias_tables",
    )(rel_bias, jnp.asarray(buckets))


def _window_buckets(t):
    r = np.arange(3 * t)[:, None]
    c = np.arange(t)[None, :]
    return _t5_bucket_np(c - r + 2 * t)[None]


def _cmp_buckets(t):
    r = np.arange(t)[:, None]
    c = np.arange(CMP_PAD)[None, :]
    return _t5_bucket_np(r - (CMP_LEN - 1) - t + CMP_STRIDE * (CMP_PAD - c))[None]


def _compress_kernel(ch_ref, pe_ref, w1_ref, w2_ref, o_ref):
    half = w1_ref.shape[0] // 2
    ch = ch_ref[0]
    a = _dot(ch, w1_ref[:half, :])
    b = _dot(ch, w1_ref[half:, :])
    n = ch.shape[0]
    b_next = pltpu.roll(b, n - 1, axis=0)
    c0 = _dot(pe_ref[...], w1_ref[...])[0:1, :]
    hid = a + b_next + c0
    hid = hid * jax.nn.sigmoid(hid)
    o_ref[0] = _dot(hid.astype(_BF), w2_ref[...]).astype(o_ref.dtype)


def _compress(chunks, pe_flat, w1, w2):
    g, n, kd = chunks.shape
    d = w2.shape[1]
    return pl.pallas_call(
        _compress_kernel,
        out_shape=jax.ShapeDtypeStruct((g, n, d), _BF),
        grid=(g,),
        in_specs=[
            pl.BlockSpec((1, n, kd), lambda i: (i, 0, 0)),
            pl.BlockSpec(pe_flat.shape, lambda i: (0, 0)),
            pl.BlockSpec(w1.shape, lambda i: (0, 0)),
            pl.BlockSpec(w2.shape, lambda i: (0, 0)),
        ],
        out_specs=pl.BlockSpec((1, n, d), lambda i: (i, 0, 0)),
        compiler_params=_params(("arbitrary",)),
        name="nsa_compress",
    )(chunks, pe_flat, w1, w2)


def _nsa_cmp_kernel(q_ref, kc_ref, vc_ref, ov_ref, tc_ref, o_ref, sel_ref):
    t = q_ref.shape[0]
    nc = kc_ref.shape[1] - CMP_PAD
    per_tile = t // CMP_STRIDE
    i = pl.program_id(1)
    thr = (i + 1) * per_tile - CMP_PAD
    near0 = pl.multiple_of((i + 1) * per_tile, per_tile)

    kc_far = kc_ref[0, CMP_PAD:, :]
    vc_far = vc_ref[0, CMP_PAD:, :]
    ov_far = ov_ref[CMP_PAD:, :]
    kc_near = kc_ref[0, pl.ds(near0, CMP_PAD), :]
    vc_near = vc_ref[0, pl.ds(near0, CMP_PAD), :]
    ov_near = ov_ref[pl.ds(near0, CMP_PAD), :]

    far_mask = lax.broadcasted_iota(jnp.int32, (t, nc), 1) < thr
    r = lax.broadcasted_iota(jnp.int32, (t, CMP_PAD), 0)
    c = lax.broadcasted_iota(jnp.int32, (t, CMP_PAD), 1)
    near_mask = (CMP_STRIDE * c <= r - (CMP_LEN - 1) - t + CMP_STRIDE * CMP_PAD) & (c >= -thr)

    imp = jnp.zeros((t, N_SLC_PAD), _F32)
    for h in range(NSA_HPG):
        q = q_ref[:, h * DKP:(h + 1) * DKP]
        sf = jnp.where(far_mask, _dot_nt(q, kc_far), _NEG_INF)
        sn = jnp.where(near_mask, _dot_nt(q, kc_near) + tc_ref[h, 0], _NEG_INF)
        m = jnp.maximum(jnp.max(sf, axis=1, keepdims=True), jnp.max(sn, axis=1, keepdims=True))
        m = jnp.where(m == _NEG_INF, 0.0, m)
        pf = jnp.exp2(sf - m)
        pn = jnp.exp2(sn - m)
        den = jnp.sum(pf, axis=1, keepdims=True) + jnp.sum(pn, axis=1, keepdims=True)
        inv = 1.0 / jnp.maximum(den, 1e-30)
        pf = pf * inv
        pn = pn * inv
        pf_hi = pf.astype(_BF)
        pn_hi = pn.astype(_BF)
        o_ref[:, h * NSA_DV:(h + 1) * NSA_DV] = (_dot(pf_hi, vc_far) + _dot(pn_hi, vc_near)).astype(o_ref.dtype)
        pf_lo = (pf - pf_hi.astype(_F32)).astype(_BF)
        pn_lo = (pn - pn_hi.astype(_F32)).astype(_BF)
        imp = imp + (_dot(pf_hi, ov_far) + _dot(pn_hi, ov_near)) + (_dot(pf_lo, ov_far) + _dot(pn_lo, ov_near))

    tq = i * t + lax.broadcasted_iota(jnp.int32, (t, N_SLC_PAD), 0)
    jj = lax.broadcasted_iota(jnp.int32, (t, N_SLC_PAD), 1)
    cur = tq >> SLC_SHIFT
    valid = jj * SLC_LEN <= tq
    forced = (jj == 0) | ((jj <= cur) & (jj >= cur - 1))
    work = jnp.where(valid, jnp.where(forced, jnp.inf, imp), _NEG_INF)
    lane = jj.astype(_F32)
    sel = jnp.zeros((t, N_SLC_PAD), _F32)
    for _ in range(SLC_TOPN):
        mx = jnp.max(work, axis=1, keepdims=True)
        first = jnp.min(jnp.where(work == mx, lane, float(N_SLC_PAD)), axis=1, keepdims=True)
        hit = lane == first
        sel = jnp.where(hit & (mx > _NEG_INF), 1.0, sel)
        work = jnp.where(hit, _NEG_INF, work)
    sel_ref[0] = sel.astype(sel_ref.dtype)


def _nsa_cmp(proj, kc_pad, vc_pad, ov_pad, tc):
    s = proj.shape[0]
    t = ATT_TILE
    gw = NSA_HPG * DKP
    return pl.pallas_call(
        _nsa_cmp_kernel,
        out_shape=(
            jax.ShapeDtypeStruct((s, NSA_HEADS * NSA_DV), _BF),
            jax.ShapeDtypeStruct((NSA_GROUPS, s, N_SLC_PAD), _BF),
        ),
        grid=(NSA_GROUPS, s // t),
        in_specs=[
            pl.BlockSpec((t, gw), lambda g, i: (i, COL_QN // gw + g)),
            pl.BlockSpec((1,) + kc_pad.shape[1:], lambda g, i: (g, 0, 0)),
            pl.BlockSpec((1,) + vc_pad.shape[1:], lambda g, i: (g, 0, 0)),
            pl.BlockSpec(ov_pad.shape, lambda g, i: (0, 0)),
            pl.BlockSpec((NSA_HPG, 1, t, CMP_PAD), lambda g, i: (g, 0, 0, 0)),
        ],
        out_specs=(
            pl.BlockSpec((t, NSA_HPG * NSA_DV), lambda g, i: (i, g)),
            pl.BlockSpec((1, t, N_SLC_PAD), lambda g, i: (g, i, 0)),
        ),
        compiler_params=_params(("parallel", "arbitrary")),
        name="nsa_cmp",
    )(proj, kc_pad, vc_pad, ov_pad, tc)


def _nsa_main_kernel(q_ref, ks_ref, vst_ref, kw_ref, vwt_ref, selt_ref, tw_ref, oc_ref, gate_ref,
                     o_ref, m_ref, l_ref, acc_ref, *, n_sub):
    t = q_ref.shape[0]
    tk = n_sub * t
    g = pl.program_id(0)
    i = pl.program_id(1)
    blocks_per_tile = t // SLC_LEN
    qs = [q_ref[:, h * DKP:(h + 1) * DKP] for h in range(NSA_HPG)]

    m_ref[...] = jnp.full(m_ref.shape, -1e30, _F32)
    l_ref[...] = jnp.zeros(l_ref.shape, _F32)
    acc_ref[...] = jnp.zeros(acc_ref.shape, _F32)

    def selected(tile0, n_tiles):
        rows = [jnp.broadcast_to(selt_ref[0, pl.ds(tile0 * blocks_per_tile + b, 1), :], (SLC_LEN, t))
                for b in range(n_tiles * blocks_per_tile)]
        return jnp.concatenate(rows, axis=0) > 0.5

    def run_heads(k, vts, mask, bias):
        def logits(h):
            st = _dot_nt(k, qs[h])
            if bias is not None:
                st = st + bias(h)
            return jnp.where(mask, st, _NEG_INF)

        st = logits(0)
        for h in range(NSA_HPG):
            st_next = logits(h + 1) if h + 1 < NSA_HPG else None
            m, l, acc = _online_update_t(st, vts, m_ref[h], l_ref[h], acc_ref[h])
            m_ref[h] = m
            l_ref[h] = l
            acc_ref[h] = acc
            st = st_next

    def far_chunk(tile0, mask_extra):
        r0 = pl.multiple_of(tile0 * t, t)
        mask = selected(tile0, n_sub)
        if mask_extra is not None:
            mask = mask & mask_extra(r0)
        run_heads(ks_ref[pl.ds(r0, tk), :], [vst_ref[0, tile0 + a] for a in range(n_sub)], mask, None)

    n_far = jnp.maximum(i - 1, 0)
    n_full = n_far // n_sub

    def far_body(c, carry):
        far_chunk(c * n_sub, None)
        return carry

    lax.fori_loop(0, n_full, far_body, 0)

    @pl.when(n_far > n_full * n_sub)
    def _():
        def not_yet_covered(r0):
            kpos = r0 + lax.broadcasted_iota(jnp.int32, (tk, t), 0)
            return (kpos >= n_full * tk) & (kpos < n_far * t)

        far_chunk(jnp.maximum(n_far - n_sub, 0), not_yet_covered)

    prev = jnp.maximum(i - 1, 0)
    row2 = lax.broadcasted_iota(jnp.int32, (2 * t, t), 0)
    col2 = lax.broadcasted_iota(jnp.int32, (2 * t, t), 1)
    near_mask = (jnp.concatenate([selected(prev, 1), selected(i, 1)], axis=0)
                 & (col2 - row2 + t >= 0) & (row2 >= jnp.where(i >= 1, 0, t)))
    k_near = jnp.concatenate([ks_ref[pl.ds(pl.multiple_of(prev * t, t), t), :],
                              ks_ref[pl.ds(pl.multiple_of(i * t, t), t), :]], axis=0)
    run_heads(k_near, [vst_ref[0, prev], vst_ref[0, i]], near_mask, lambda h: tw_ref[h, 0, t:, :])

    prev2 = jnp.maximum(i - 2, 0)
    w_tiles = (prev2, prev, i)
    k_win = jnp.concatenate([kw_ref[pl.ds(pl.multiple_of(a * t, t), t), :] for a in w_tiles], axis=0)
    row3 = lax.broadcasted_iota(jnp.int32, (3 * t, t), 0)
    col3 = lax.broadcasted_iota(jnp.int32, (3 * t, t), 1)
    dist = col3 - row3 + 2 * t
    first_row = jnp.where(i >= 2, 0, jnp.where(i >= 1, t, 2 * t))
    win_mask = (dist >= 0) & (dist < WINDOW) & (row3 >= first_row)

    gates_t = jax.nn.sigmoid(gate_ref[...].astype(_F32)).T
    for h in range(NSA_HPG):
        st = jnp.where(win_mask, _dot_nt(k_win, qs[h]) + tw_ref[h, 0], _NEG_INF)
        m = jnp.max(st, axis=0, keepdims=True)
        pt = jnp.exp2(st - m)
        den = jnp.sum(pt, axis=0, keepdims=True)
        pb = pt.astype(_BF)
        o_w = functools.reduce(jnp.add, [_dot(vwt_ref[0, a], pb[n * t:(n + 1) * t, :])
                                         for n, a in enumerate(w_tiles)]) * (1.0 / den)
        o_s = acc_ref[h] * (1.0 / l_ref[h])
        o_c = oc_ref[:, h * NSA_DV:(h + 1) * NSA_DV].astype(_F32).T
        gc, gs, gw = [jnp.where(g == 0, gates_t[h * N_BRANCH + b:h * N_BRANCH + b + 1, :],
                                gates_t[(NSA_HPG + h) * N_BRANCH + b:(NSA_HPG + h) * N_BRANCH + b + 1, :])
                      for b in range(N_BRANCH)]
        o_ref[:, h * NSA_DV:(h + 1) * NSA_DV] = (gc * o_c + gs * o_s + gw * o_w).T.astype(o_ref.dtype)


def _tiles_transposed(v, t):
    s, d = v.shape
    return v.reshape(s // t, t, d).transpose(0, 2, 1)


def _nsa_main(proj, sel, tw, o_c, *, n_sub=4):
    s = proj.shape[0]
    t = ATT_TILE
    nb = s // t
    gw = NSA_HPG * DKP
    kn0 = COL_KN // DKP

    def values_t(branch):
        c0 = COL_VN + branch * NSA_GROUPS * NSA_DV
        return jnp.stack([_tiles_transposed(proj[:, c0 + g * NSA_DV:c0 + (g + 1) * NSA_DV], t)
                          for g in range(NSA_GROUPS)])

    sel_t = sel.astype(_F32).transpose(0, 2, 1)
    return pl.pallas_call(
        functools.partial(_nsa_main_kernel, n_sub=n_sub),
        out_shape=jax.ShapeDtypeStruct((s, NSA_HEADS * NSA_DV), _BF),
        grid=(NSA_GROUPS, nb),
        in_specs=[
            pl.BlockSpec((t, gw), lambda g, i: (i, COL_QN // gw + g)),
            pl.BlockSpec((s, DKP), lambda g, i: (0, kn0 + 1 * NSA_GROUPS + g)),
            pl.BlockSpec((1, nb, NSA_DV, t), lambda g, i: (g, 0, 0, 0)),
            pl.BlockSpec((s, DKP), lambda g, i: (0, kn0 + 2 * NSA_GROUPS + g)),
            pl.BlockSpec((1, nb, NSA_DV, t), lambda g, i: (g, 0, 0, 0)),
            pl.BlockSpec((1, N_SLC_PAD, t), lambda g, i: (g, 0, i)),
            pl.BlockSpec((NSA_HPG, 1, 3 * t, t), lambda g, i: (g, 0, 0, 0)),
            pl.BlockSpec((t, NSA_HPG * NSA_DV), lambda g, i: (i, g)),
            pl.BlockSpec((t, LANES), lambda g, i: (i, COL_GATE // LANES)),
        ],
        out_specs=pl.BlockSpec((t, NSA_HPG * NSA_DV), lambda g, i: (i, g)),
        scratch_shapes=[
            pltpu.VMEM((NSA_HPG, 1, t), _F32),
            pltpu.VMEM((NSA_HPG, 1, t), _F32),
            pltpu.VMEM((NSA_HPG, NSA_DV, t), _F32),
        ],
        compiler_params=_params(("parallel", "arbitrary")),
        name="nsa_main",
    )(proj, proj, values_t(1), proj, values_t(2), sel_t, tw, o_c, proj)


def _outproj_kernel(x_ref, mod_ref, om_ref, on_ref, w_ref, o_ref, *, mod_row):
    half = om_ref.shape[1]
    y = _dot(om_ref[...], w_ref[:half, :]) + _dot(on_ref[...], w_ref[half:, :])
    o_ref[...] = x_ref[...] + mod_ref[mod_row:mod_row + 1, :] * y


def _outproj(x, mod, o_mla, o_nsa, w_out, *, mod_row, tm=512):
    s = x.shape[0]
    return pl.pallas_call(
        functools.partial(_outproj_kernel, mod_row=mod_row),
        out_shape=jax.ShapeDtypeStruct((s, D_MODEL), _F32),
        grid=(s // tm,),
        in_specs=[
            pl.BlockSpec((tm, D_MODEL), lambda i: (i, 0)),
            pl.BlockSpec((9, D_MODEL), lambda i: (0, 0)),
            pl.BlockSpec((tm, o_mla.shape[1]), lambda i: (i, 0)),
            pl.BlockSpec((tm, o_nsa.shape[1]), lambda i: (i, 0)),
            pl.BlockSpec(w_out.shape, lambda i: (0, 0)),
        ],
        out_specs=pl.BlockSpec((tm, D_MODEL), lambda i: (i, 0)),
        compiler_params=_params(("parallel",)),
        name="outproj",
    )(x, mod, o_mla, o_nsa, w_out)


def _pad_cols(w, width):
    return jnp.pad(w, ((0, 0), (0, width - w.shape[1])))


def _pad_last(w, width):
    return jnp.pad(w, [(0, 0)] * (w.ndim - 1) + [(0, width - w.shape[-1])])


def _rotate_half_cols(w):
    half = MLA_ROPE // 2
    return jnp.concatenate([-w[..., half:], w[..., :half]], axis=-1)


def _w_in_padded(w_in):
    d = w_in.shape[0]
    sizes = (MLA_Q_RANK, MLA_KV_RANK, MLA_ROPE, NSA_HEADS * NSA_DK,
             N_BRANCH * NSA_GROUPS * NSA_DK, N_BRANCH * NSA_GROUPS * NSA_DV, NSA_HEADS * N_BRANCH)
    offs = np.concatenate([[0], np.cumsum(sizes)])
    cq, ckv, kr, qn, kn, vn, gn = [w_in[:, offs[a]:offs[a + 1]] for a in range(len(sizes))]
    qn = _pad_last(qn.reshape(d, NSA_HEADS, NSA_DK), DKP).reshape(d, NSA_HEADS * DKP)
    kn = _pad_last(kn.reshape(d, N_BRANCH * NSA_GROUPS, NSA_DK), DKP).reshape(d, -1)
    cols = [cq, _pad_cols(gn, LANES), _pad_cols(kr, LANES), ckv,
            _pad_cols(_rotate_half_cols(kr), LANES),
            jnp.zeros((d, COL_QN - COL_KR_ROT - LANES), w_in.dtype), qn, kn, vn]
    w = jnp.concatenate(cols, axis=1)
    assert w.shape[1] == D_IN_PAD
    colscale = np.ones((1, D_IN_PAD), np.float32)
    colscale[:, COL_QN:COL_KN] = NSA_DK ** -0.5 * LOG2_E
    return w.astype(_BF), jnp.asarray(colscale)


def _w_uq_padded(w_uq):
    r = w_uq.shape[0]
    w = w_uq.reshape(r, MLA_HEADS, MLA_NOPE + MLA_ROPE)
    nope = w[..., :MLA_NOPE].reshape(r, -1)
    rope = w[..., MLA_NOPE:]
    rope_p = _pad_last(rope, LANES).reshape(r, -1)
    rot_p = _pad_last(_rotate_half_cols(rope), LANES).reshape(r, -1)
    return jnp.concatenate([nope, rope_p, rot_p], axis=1).astype(_BF)


def _cmp_w1_padded(w1, d, dpad):
    return _pad_last(w1.reshape(CMP_LEN, d, CMP_HIDDEN).transpose(0, 2, 1), dpad) \
        .transpose(0, 2, 1).reshape(CMP_LEN * dpad, CMP_HIDDEN).astype(_BF)


def _rope_tables(s):
    inv = ROPE_THETA ** (-jnp.arange(0, MLA_ROPE, 2, dtype=_F32) / MLA_ROPE)
    ang = jnp.arange(s, dtype=_F32)[:, None] * inv[None, :]
    cos = _pad_cols(jnp.tile(jnp.cos(ang), (1, 2)), LANES)
    sin = _pad_cols(jnp.tile(jnp.sin(ang), (1, 2)), LANES)
    return cos, sin


def _overlap_padded(s):
    n_cmp_rows = s // CMP_STRIDE
    c_start = np.arange(n_cmp_rows)[:, None] * CMP_STRIDE
    s_start = np.arange(N_SLC_PAD)[None, :] * SLC_LEN
    ov = ((c_start < s_start + SLC_LEN) & (c_start + CMP_LEN > s_start)).astype(np.float32)
    return jnp.asarray(np.concatenate([np.zeros((CMP_PAD, N_SLC_PAD), np.float32), ov]), _BF)


def _mixer_heads(x, mod, norm_mix, w_in, mla_q_norm, w_uq, mla_kv_norm, w_ukv, pe_cmp_k, w_cmp_k1,
                 w_cmp_k2, pe_cmp_v, w_cmp_v1, w_cmp_v2, rel_bias):
    s = x.shape[0]
    assert s % ATT_TILE == 0 and s // SLC_LEN <= N_SLC_PAD and s >= 3 * ATT_TILE
    w_pad, colscale = _w_in_padded(w_in)
    proj = _inproj(x, mod, norm_mix.reshape(1, -1), w_pad, colscale, mod_row=3)

    cos, sin = _rope_tables(s)
    q, k, v = _mla_prep(proj, mla_q_norm.reshape(1, -1), mla_kv_norm.reshape(1, -1),
                        _w_uq_padded(w_uq), w_ukv.astype(_BF), cos, sin)
    o_mla = _mla_attn(q, k, v)

    n_chunks = s // CMP_STRIDE
    kn0 = COL_KN
    k_cmp = jnp.stack([proj[:, kn0 + g * DKP:kn0 + (g + 1) * DKP] for g in range(NSA_GROUPS)])
    v_cmp = jnp.stack([proj[:, COL_VN + g * NSA_DV:COL_VN + (g + 1) * NSA_DV] for g in range(NSA_GROUPS)])
    pe_k = jnp.broadcast_to(_pad_last(pe_cmp_k, DKP).reshape(1, -1), (8, CMP_LEN * DKP)).astype(_BF)
    pe_v = jnp.broadcast_to(pe_cmp_v.reshape(1, -1), (8, CMP_LEN * NSA_DV)).astype(_BF)
    kc = _compress(k_cmp.reshape(NSA_GROUPS, n_chunks, CMP_STRIDE * DKP), pe_k,
                   _cmp_w1_padded(w_cmp_k1, NSA_DK, DKP), _pad_cols(w_cmp_k2, DKP).astype(_BF))
    vc = _compress(v_cmp.reshape(NSA_GROUPS, n_chunks, CMP_STRIDE * NSA_DV), pe_v,
                   w_cmp_v1.astype(_BF), w_cmp_v2.astype(_BF))
    front = ((0, 0), (CMP_PAD, 0), (0, 0))
    tc = _bias_tables(rel_bias, _cmp_buckets(ATT_TILE))
    tw = _bias_tables(rel_bias, _window_buckets(ATT_TILE))
    o_c, sel = _nsa_cmp(proj, jnp.pad(kc, front), jnp.pad(vc, front), _overlap_padded(s), tc)
    o_nsa = _nsa_main(proj, sel, tw, o_c)
    return o_mla, o_nsa


def kernel(x, c, w_ada, b_ada, norm_ffn1, w1_gate, w1_up, w1_down, norm_mix, w_in, mla_q_norm, w_uq, mla_kv_norm, w_ukv, pe_cmp_k, w_cmp_k1, w_cmp_k2, pe_cmp_v, w_cmp_v1, w_cmp_v2, rel_bias, w_out, norm_ffn2, w2_gate, w2_up, w2_down, norm_final):
    assert x.shape[0] == 1 and w_ada.shape[0] == 1
    xs = x[0]
    fin = norm_final.reshape(1, -1)
    mod = _ada_mod(c, w_ada[0], b_ada[0]).reshape(9, D_MODEL)
    xs = _ffn(xs, mod, norm_ffn1[0].reshape(1, -1), w1_gate[0].astype(_BF), w1_up[0].astype(_BF),
              w1_down[0].astype(_BF), fin, mod_row=0, final_norm=False)
    o_mla, o_nsa = _mixer_heads(xs, mod, norm_mix[0], w_in[0], mla_q_norm[0], w_uq[0], mla_kv_norm[0],
                                w_ukv[0], pe_cmp_k[0], w_cmp_k1[0], w_cmp_k2[0], pe_cmp_v[0],
                                w_cmp_v1[0], w_cmp_v2[0], rel_bias)
    xs = _outproj(xs, mod, o_mla, o_nsa, w_out[0].astype(_BF), mod_row=5)
    xs = _ffn(xs, mod, norm_ffn2[0].reshape(1, -1), w2_gate[0].astype(_BF), w2_up[0].astype(_BF),
              w2_down[0].astype(_BF), fin, mod_row=6, final_norm=True)
    return xs[None]
```

```python
import functools
import math

import numpy as np
import jax
import jax.numpy as jnp
from jax import lax
from jax.experimental import pallas as pl
from jax.experimental.pallas import tpu as pltpu

D_MODEL = 2048
D_FF = 5632
EPS = 1e-6
MLA_HEADS = 8
MLA_Q_RANK = 768
MLA_KV_RANK = 512
MLA_NOPE = 128
MLA_ROPE = 64
MLA_V = 128
ROPE_THETA = 10000.0
NSA_HEADS = 8
NSA_GROUPS = 2
NSA_HPG = NSA_HEADS // NSA_GROUPS
NSA_DK = 192
NSA_DV = 128
CMP_LEN = 32
CMP_STRIDE = 16
CMP_HIDDEN = 256
SLC_LEN = 64
SLC_SHIFT = 6
SLC_TOPN = 16
WINDOW = 512
N_BRANCH = 3
REL_BUCKETS = 32
REL_MAX_DIST = 128

LANES = 128
VMEM_LIMIT = 56 * 1024 * 1024

DKP = 256
ATT_TILE = 256
CMP_PAD = 128
N_SLC_PAD = 128

_BF = jnp.bfloat16
_F32 = jnp.float32
_NEG_INF = float("-inf")
LOG2_E = math.log2(math.e)

COL_CQ = 0
COL_GATE = 768
COL_KR = 896
COL_CKV = 1024
COL_KR_ROT = 1536
COL_QN = 2048
COL_KN = COL_QN + NSA_HEADS * DKP
COL_VN = COL_KN + N_BRANCH * NSA_GROUPS * DKP
D_IN_PAD = COL_VN + N_BRANCH * NSA_GROUPS * NSA_DV


def _dot(a, b):
    return jnp.dot(a, b, preferred_element_type=_F32)


def _dot_nt(a, b):
    return lax.dot_general(a, b, (((1,), (1,)), ((), ())), preferred_element_type=_F32)


def _rms(x, gain):
    return x * lax.rsqrt(jnp.mean(x * x, axis=-1, keepdims=True) + EPS) * gain


def _params(semantics):
    return pltpu.CompilerParams(dimension_semantics=semantics, vmem_limit_bytes=VMEM_LIMIT)


ADA_ROWS = 256


def _ada_kernel(c_ref, w_ref, b_ref, o_ref):
    tn = o_ref.shape[1]

    def body(k, acc):
        r = pl.multiple_of(k * ADA_ROWS, ADA_ROWS)
        c = c_ref[pl.ds(r, ADA_ROWS), :]
        s = c * jax.nn.sigmoid(c)
        prod = w_ref[pl.ds(r, ADA_ROWS), :] * s
        return acc + jnp.sum(prod.reshape(ADA_ROWS // 8, 8, tn), axis=0)

    acc = lax.fori_loop(0, D_MODEL // ADA_ROWS, body, jnp.zeros((8, tn), _F32))
    o_ref[...] = jnp.sum(acc, axis=0, keepdims=True) + b_ref[...]


def _ada_mod(c, w_ada, b_ada):
    n = w_ada.shape[1]
    tn = 1024
    return pl.pallas_call(
        _ada_kernel,
        out_shape=jax.ShapeDtypeStruct((1, n), _F32),
        grid=(n // tn,),
        in_specs=[
            pl.BlockSpec((D_MODEL, 1), lambda j: (0, 0)),
            pl.BlockSpec((D_MODEL, tn), lambda j: (0, j)),
            pl.BlockSpec((1, tn), lambda j: (0, j)),
        ],
        out_specs=pl.BlockSpec((1, tn), lambda j: (0, j)),
        compiler_params=_params(("arbitrary",)),
        name="ada_mod",
    )(c.reshape(D_MODEL, 1), w_ada, b_ada.reshape(1, n))


def _ffn_kernel(x_ref, mod_ref, gain_ref, wg_ref, wu_ref, wd_ref, fin_ref, o_ref, h_ref,
                *, mod_row, final_norm):
    j = pl.program_id(1)

    @pl.when(j == 0)
    def _():
        y = _rms(x_ref[...], gain_ref[...])
        h = y * (1.0 + mod_ref[mod_row + 1:mod_row + 2, :]) + mod_ref[mod_row:mod_row + 1, :]
        h_ref[...] = h.astype(_BF)
        o_ref[...] = jnp.zeros_like(o_ref)

    h = h_ref[...]
    g = _dot(h, wg_ref[...])
    u = _dot(h, wu_ref[...])
    a = (g * jax.nn.sigmoid(g)) * u
    o_ref[...] += _dot(a.astype(_BF), wd_ref[...])

    @pl.when(j == pl.num_programs(1) - 1)
    def _():
        x2 = x_ref[...] + (0.5 * mod_ref[mod_row + 2:mod_row + 3, :]) * o_ref[...]
        if final_norm:
            x2 = _rms(x2, fin_ref[...])
        o_ref[...] = x2


def _ffn(x, mod, gain, wg, wu, wd, fin, *, mod_row, final_norm, tm=1024, tf=256):
    s = x.shape[0]
    tm = min(tm, s)
    kern = functools.partial(_ffn_kernel, mod_row=mod_row, final_norm=final_norm)
    return pl.pallas_call(
        kern,
        out_shape=jax.ShapeDtypeStruct((s, D_MODEL), _F32),
        grid=(s // tm, D_FF // tf),
        in_specs=[
            pl.BlockSpec((tm, D_MODEL), lambda i, j: (i, 0)),
            pl.BlockSpec((9, D_MODEL), lambda i, j: (0, 0)),
            pl.BlockSpec((1, D_MODEL), lambda i, j: (0, 0)),
            pl.BlockSpec((D_MODEL, tf), lambda i, j: (0, j)),
            pl.BlockSpec((D_MODEL, tf), lambda i, j: (0, j)),
            pl.BlockSpec((tf, D_MODEL), lambda i, j: (j, 0)),
            pl.BlockSpec((1, D_MODEL), lambda i, j: (0, 0)),
        ],
        out_specs=pl.BlockSpec((tm, D_MODEL), lambda i, j: (i, 0)),
        scratch_shapes=[pltpu.VMEM((tm, D_MODEL), _BF)],
        compiler_params=_params(("parallel", "arbitrary")),
        name="ffn_final" if final_norm else "ffn",
    )(x, mod, gain, wg, wu, wd, fin)


def _inproj_kernel(x_ref, mod_ref, gain_ref, w_ref, cs_ref, o_ref, h_ref, *, mod_row):
    @pl.when(pl.program_id(1) == 0)
    def _():
        y = _rms(x_ref[...], gain_ref[...])
        h = y * (1.0 + mod_ref[mod_row + 1:mod_row + 2, :]) + mod_ref[mod_row:mod_row + 1, :]
        h_ref[...] = h.astype(_BF)

    o_ref[...] = (_dot(h_ref[...], w_ref[...]) * cs_ref[...]).astype(o_ref.dtype)


def _inproj(x, mod, gain, w_pad, colscale, *, mod_row, tm=1024, tn=1280):
    s = x.shape[0]
    tm = min(tm, s)
    n = w_pad.shape[1]
    return pl.pallas_call(
        functools.partial(_inproj_kernel, mod_row=mod_row),
        out_shape=jax.ShapeDtypeStruct((s, n), _BF),
        grid=(s // tm, n // tn),
        in_specs=[
            pl.BlockSpec((tm, D_MODEL), lambda i, j: (i, 0)),
            pl.BlockSpec((9, D_MODEL), lambda i, j: (0, 0)),
            pl.BlockSpec((1, D_MODEL), lambda i, j: (0, 0)),
            pl.BlockSpec((D_MODEL, tn), lambda i, j: (0, j)),
            pl.BlockSpec((1, tn), lambda i, j: (0, j)),
        ],
        out_specs=pl.BlockSpec((tm, tn), lambda i, j: (i, j)),
        scratch_shapes=[pltpu.VMEM((tm, D_MODEL), _BF)],
        compiler_params=_params(("parallel", "arbitrary")),
        name="inproj",
    )(x, mod, gain, w_pad, colscale)


def _mla_prep_kernel(cq_ref, ckv_ref, kr_ref, krr_ref, gq_ref, gkv_ref, wq_ref, wkv_ref,
                     cos_ref, sin_ref, q_ref, k_ref, v_ref):
    scale = (MLA_NOPE + MLA_ROPE) ** -0.5 * LOG2_E
    cos = cos_ref[...]
    sin = sin_ref[...]
    hq = _rms(cq_ref[...].astype(_F32), gq_ref[...]).astype(_BF)
    qa = _dot(hq, wq_ref[...])
    nh = MLA_HEADS * LANES
    for h in range(MLA_HEADS):
        lo = h * LANES
        nope = qa[:, lo:lo + LANES]
        a = qa[:, nh + lo:nh + lo + LANES]
        b = qa[:, 2 * nh + lo:2 * nh + lo + LANES]
        q_ref[:, h * DKP:h * DKP + LANES] = (nope * scale).astype(_BF)
        q_ref[:, h * DKP + LANES:(h + 1) * DKP] = ((a * cos + b * sin) * scale).astype(_BF)
    hkv = _rms(ckv_ref[...].astype(_F32), gkv_ref[...]).astype(_BF)
    kv = _dot(hkv, wkv_ref[...])
    k_rope = (kr_ref[...].astype(_F32) * cos + krr_ref[...].astype(_F32) * sin).astype(_BF)
    for h in range(MLA_HEADS):
        lo = h * (MLA_NOPE + MLA_V)
        k_ref[:, h * DKP:h * DKP + LANES] = kv[:, lo:lo + MLA_NOPE].astype(_BF)
        k_ref[:, h * DKP + LANES:(h + 1) * DKP] = k_rope
        v_ref[:, h * MLA_V:(h + 1) * MLA_V] = kv[:, lo + MLA_NOPE:lo + MLA_NOPE + MLA_V].astype(_BF)


def _mla_prep(proj, gq, gkv, wq, wkv, cos, sin, *, tm=512):
    s = proj.shape[0]
    const = lambda i: (0, 0)
    return pl.pallas_call(
        _mla_prep_kernel,
        out_shape=(
            jax.ShapeDtypeStruct((s, MLA_HEADS * DKP), _BF),
            jax.ShapeDtypeStruct((s, MLA_HEADS * DKP), _BF),
            jax.ShapeDtypeStruct((s, MLA_HEADS * MLA_V), _BF),
        ),
        grid=(s // tm,),
        in_specs=[
            pl.BlockSpec((tm, MLA_Q_RANK), lambda i: (i, COL_CQ // MLA_Q_RANK)),
            pl.BlockSpec((tm, MLA_KV_RANK), lambda i: (i, COL_CKV // MLA_KV_RANK)),
            pl.BlockSpec((tm, LANES), lambda i: (i, COL_KR // LANES)),
            pl.BlockSpec((tm, LANES), lambda i: (i, COL_KR_ROT // LANES)),
            pl.BlockSpec((1, MLA_Q_RANK), const),
            pl.BlockSpec((1, MLA_KV_RANK), const),
            pl.BlockSpec(wq.shape, const),
            pl.BlockSpec(wkv.shape, const),
            pl.BlockSpec((tm, LANES), lambda i: (i, 0)),
            pl.BlockSpec((tm, LANES), lambda i: (i, 0)),
        ],
        out_specs=(
            pl.BlockSpec((tm, MLA_HEADS * DKP), lambda i: (i, 0)),
            pl.BlockSpec((tm, MLA_HEADS * DKP), lambda i: (i, 0)),
            pl.BlockSpec((tm, MLA_HEADS * MLA_V), lambda i: (i, 0)),
        ),
        compiler_params=_params(("parallel",)),
        name="mla_prep",
    )(proj, proj, proj, proj, gq, gkv, wq, wkv, cos, sin)


def _online_update_t(st, vts, m, l, acc):
    t = vts[0].shape[1]
    m_new = jnp.maximum(m, jnp.max(st, axis=0, keepdims=True))
    alpha = jnp.exp2(m - m_new)
    pt = jnp.exp2(st - m_new)
    l = alpha * l + jnp.sum(pt, axis=0, keepdims=True)
    pb = pt.astype(_BF)
    pv = functools.reduce(jnp.add, [_dot(vt, pb[a * t:(a + 1) * t, :]) for a, vt in enumerate(vts)])
    return m_new, l, alpha * acc + pv


def _mla_attn_kernel(q_ref, k_ref, vt_ref, o_ref, m_ref, l_ref, acc_ref, *, n_sub):
    t = q_ref.shape[0]
    tk = n_sub * t
    hp = vt_ref.shape[0]
    i = pl.program_id(1)
    qs = [q_ref[:, h * DKP:(h + 1) * DKP] for h in range(hp)]
    n_full = i // n_sub
    tail_b0 = jnp.maximum(i + 1 - n_sub, 0)

    m_ref[...] = jnp.full(m_ref.shape, -1e30, _F32)
    l_ref[...] = jnp.zeros(l_ref.shape, _F32)
    acc_ref[...] = jnp.zeros(acc_ref.shape, _F32)

    def first_tile(c):
        return jnp.where(c < n_full, c * n_sub, tail_b0)

    def logits(c, h):
        r0 = pl.multiple_of(first_tile(c) * t, t)
        return _dot_nt(k_ref[pl.ds(r0, tk), h * DKP:(h + 1) * DKP], qs[h])

    def update(c, h, st):
        b0 = first_tile(c)
        m, l, acc = _online_update_t(st, [vt_ref[h, b0 + a] for a in range(n_sub)],
                                     m_ref[h], l_ref[h], acc_ref[h])
        m_ref[h] = m
        l_ref[h] = l
        acc_ref[h] = acc

    def body(c, st):
        for h in range(hp):
            st_next = logits(c, h + 1) if h + 1 < hp else logits(c + 1, 0)
            update(c, h, st)
            st = st_next
        return st

    st = lax.fori_loop(0, n_full, body, logits(0, 0))
    kpos = tail_b0 * t + lax.broadcasted_iota(jnp.int32, (tk, t), 0)
    qpos = i * t + lax.broadcasted_iota(jnp.int32, (tk, t), 1)
    tail_mask = (kpos >= n_full * tk) & (kpos <= qpos)
    for h in range(hp):
        st_next = logits(n_full, h + 1) if h + 1 < hp else None
        update(n_full, h, jnp.where(tail_mask, st, _NEG_INF))
        st = st_next
    for h in range(hp):
        o_ref[:, h * MLA_V:(h + 1) * MLA_V] = (acc_ref[h] * (1.0 / l_ref[h])).T.astype(o_ref.dtype)


def _mla_attn(q, k, v, *, n_sub=4, hp=2):
    s = q.shape[0]
    t = ATT_TILE
    nb = s // t
    vt = v.reshape(nb, t, MLA_HEADS, MLA_V).transpose(2, 0, 3, 1)
    return pl.pallas_call(
        functools.partial(_mla_attn_kernel, n_sub=n_sub),
        out_shape=jax.ShapeDtypeStruct((s, MLA_HEADS * MLA_V), _BF),
        grid=(MLA_HEADS // hp, nb),
        in_specs=[
            pl.BlockSpec((t, hp * DKP), lambda h, i: (i, h)),
            pl.BlockSpec((s, hp * DKP), lambda h, i: (0, h)),
            pl.BlockSpec((hp, nb, MLA_V, t), lambda h, i: (h, 0, 0, 0)),
        ],
        out_specs=pl.BlockSpec((t, hp * MLA_V), lambda h, i: (i, h)),
        scratch_shapes=[
            pltpu.VMEM((hp, 1, t), _F32),
            pltpu.VMEM((hp, 1, t), _F32),
            pltpu.VMEM((hp, MLA_V, t), _F32),
        ],
        compiler_params=_params(("parallel", "arbitrary")),
        name="mla_attn",
    )(q, k, vt)


def _t5_bucket_np(dist):
    n = np.maximum(dist, 0)
    max_exact = REL_BUCKETS // 2
    nf = np.maximum(n, 1).astype(np.float32)
    large = max_exact + (np.log(nf / np.float32(max_exact)) / np.float32(math.log(REL_MAX_DIST / max_exact))
                         * np.float32(REL_BUCKETS - max_exact)).astype(np.int32)
    large = np.minimum(large, REL_BUCKETS - 1)
    return np.where(n < max_exact, n, large).astype(np.int32)


def _bias_table_kernel(rb_ref, bk_ref, o_ref):
    h = pl.program_id(0)
    bk = bk_ref[0]
    base = rb_ref[REL_BUCKETS - 1, h]
    acc = jnp.zeros(bk.shape, _F32)
    for b in range(REL_BUCKETS - 1):
        acc = jnp.where(bk == b, (rb_ref[b, h] - base) * LOG2_E, acc)
    o_ref[0, 0] = acc


def _bias_tables(rel_bias, buckets):
    nt, r, c = buckets.shape
    return pl.pallas_call(
        _bias_table_kernel,
        out_shape=jax.ShapeDtypeStruct((NSA_HEADS, nt, r, c), _F32),
        grid=(NSA_HEADS, nt),
        in_specs=[
            pl.BlockSpec(memory_space=pltpu.SMEM),
            pl.BlockSpec((1, r, c), lambda h, t: (t, 0, 0)),
        ],
        out_specs=pl.BlockSpec((1, 1, r, c), lambda h, t: (h, t, 0, 0)),
        compiler_params=_params(("arbitrary", "arbitrary")),
        name="bias_tables",
    )(rel_bias, jnp.asarray(buckets))


def _window_buckets(t):
    r = np.arange(3 * t)[:, None]
    c = np.arange(t)[None, :]
    return _t5_bucket_np(c - r + 2 * t)[None]


def _cmp_buckets(t):
    c = np.arange(CMP_PAD)[:, None]
    r = np.arange(t)[None, :]
    return _t5_bucket_np(r - (CMP_LEN - 1) - t + CMP_STRIDE * (CMP_PAD - c))[None]


def _compress_kernel(ch_ref, pe_ref, w1_ref, w2_ref, o_ref):
    half = w1_ref.shape[0] // 2
    ch = ch_ref[0]
    a = _dot(ch, w1_ref[:half, :])
    b = _dot(ch, w1_ref[half:, :])
    n = ch.shape[0]
    b_next = pltpu.roll(b, n - 1, axis=0)
    c0 = _dot(pe_ref[...], w1_ref[...])[0:1, :]
    hid = a + b_next + c0
    hid = hid * jax.nn.sigmoid(hid)
    o_ref[0] = _dot(hid.astype(_BF), w2_ref[...]).astype(o_ref.dtype)


def _compress(chunks, pe_flat, w1, w2):
    g, n, kd = chunks.shape
    d = w2.shape[1]
    return pl.pallas_call(
        _compress_kernel,
        out_shape=jax.ShapeDtypeStruct((g, n, d), _BF),
        grid=(g,),
        in_specs=[
            pl.BlockSpec((1, n, kd), lambda i: (i, 0, 0)),
            pl.BlockSpec(pe_flat.shape, lambda i: (0, 0)),
            pl.BlockSpec(w1.shape, lambda i: (0, 0)),
            pl.BlockSpec(w2.shape, lambda i: (0, 0)),
        ],
        out_specs=pl.BlockSpec((1, n, d), lambda i: (i, 0, 0)),
        compiler_params=_params(("arbitrary",)),
        name="nsa_compress",
    )(chunks, pe_flat, w1, w2)


def _nsa_cmp_kernel(q_ref, kc_ref, vc_ref, afar_ref, tc_ref, oc_ref, sel_ref):
    t = q_ref.shape[0]
    nc = kc_ref.shape[1] - CMP_PAD
    per_tile = t // CMP_STRIDE
    i = pl.program_id(1)
    thr = (i + 1) * per_tile - CMP_PAD
    near0 = pl.multiple_of((i + 1) * per_tile, per_tile)

    k_ext = jnp.concatenate([kc_ref[0, CMP_PAD:, :], kc_ref[0, pl.ds(near0, CMP_PAD), :]], axis=0)
    vcn_t = vc_ref[0, pl.ds(near0, CMP_PAD), :].astype(_F32).T.astype(_BF)
    jn = lax.broadcasted_iota(jnp.int32, (N_SLC_PAD, CMP_PAD), 0) * SLC_LEN
    cn = (thr + lax.broadcasted_iota(jnp.int32, (N_SLC_PAD, CMP_PAD), 1)) * CMP_STRIDE
    ovn_t = jnp.where((cn < jn + SLC_LEN) & (cn + CMP_LEN > jn), 1.0, 0.0).astype(_BF)
    a_near = jnp.concatenate([vcn_t, ovn_t], axis=0)
    a_far = afar_ref[0]

    far_mask = lax.broadcasted_iota(jnp.int32, (nc, t), 0) < thr
    c = lax.broadcasted_iota(jnp.int32, (CMP_PAD, t), 0)
    r = lax.broadcasted_iota(jnp.int32, (CMP_PAD, t), 1)
    near_mask = (CMP_STRIDE * c <= r - (CMP_LEN - 1) - t + CMP_STRIDE * CMP_PAD) & (c >= -thr)

    imp = jnp.zeros((N_SLC_PAD, t), _F32)
    for h in range(NSA_HPG):
        st = _dot_nt(k_ext, q_ref[:, h * DKP:(h + 1) * DKP])
        sf = jnp.where(far_mask, st[:nc], _NEG_INF)
        sn = jnp.where(near_mask, st[nc:] + tc_ref[h, 0], _NEG_INF)
        m = jnp.maximum(jnp.max(sf, axis=0, keepdims=True), jnp.max(sn, axis=0, keepdims=True))
        m = jnp.where(m == _NEG_INF, 0.0, m)
        pf = jnp.exp2(sf - m)
        pn = jnp.exp2(sn - m)
        den = jnp.sum(pf, axis=0, keepdims=True) + jnp.sum(pn, axis=0, keepdims=True)
        inv = 1.0 / jnp.maximum(den, 1e-30)
        pf = pf * inv
        pn = pn * inv
        pf_hi = pf.astype(_BF)
        pn_hi = pn.astype(_BF)
        res = _dot(a_far, pf_hi) + _dot(a_near, pn_hi)
        oc_ref[h * NSA_DV:(h + 1) * NSA_DV, :] = res[:NSA_DV].astype(oc_ref.dtype)
        pf_lo = (pf - pf_hi.astype(_F32)).astype(_BF)
        pn_lo = (pn - pn_hi.astype(_F32)).astype(_BF)
        imp = imp + res[NSA_DV:] + (_dot(a_far[NSA_DV:], pf_lo) + _dot(a_near[NSA_DV:], pn_lo))

    jj = lax.broadcasted_iota(jnp.int32, (N_SLC_PAD, t), 0)
    tq = i * t + lax.broadcasted_iota(jnp.int32, (N_SLC_PAD, t), 1)
    cur = tq >> SLC_SHIFT
    valid = jj * SLC_LEN <= tq
    forced = (jj == 0) | ((jj <= cur) & (jj >= cur - 1))
    work = jnp.where(valid, jnp.where(forced, jnp.inf, imp), _NEG_INF)
    blk = jj.astype(_F32)
    sel = jnp.zeros((N_SLC_PAD, t), _F32)
    for _ in range(SLC_TOPN):
        mx = jnp.max(work, axis=0, keepdims=True)
        first = jnp.min(jnp.where(work == mx, blk, float(N_SLC_PAD)), axis=0, keepdims=True)
        hit = blk == first
        sel = jnp.where(hit & (mx > _NEG_INF), 1.0, sel)
        work = jnp.where(hit, _NEG_INF, work)
    sel_ref[0] = sel


def _nsa_cmp(proj, kc_pad, vc_pad, a_far, tc):
    s = proj.shape[0]
    t = ATT_TILE
    gw = NSA_HPG * DKP
    return pl.pallas_call(
        _nsa_cmp_kernel,
        out_shape=(
            jax.ShapeDtypeStruct((NSA_HEADS * NSA_DV, s), _BF),
            jax.ShapeDtypeStruct((NSA_GROUPS, N_SLC_PAD, s), _F32),
        ),
        grid=(NSA_GROUPS, s // t),
        in_specs=[
            pl.BlockSpec((t, gw), lambda g, i: (i, COL_QN // gw + g)),
            pl.BlockSpec((1,) + kc_pad.shape[1:], lambda g, i: (g, 0, 0)),
            pl.BlockSpec((1,) + vc_pad.shape[1:], lambda g, i: (g, 0, 0)),
            pl.BlockSpec((1,) + a_far.shape[1:], lambda g, i: (g, 0, 0)),
            pl.BlockSpec((NSA_HPG, 1, CMP_PAD, t), lambda g, i: (g, 0, 0, 0)),
        ],
        out_specs=(
            pl.BlockSpec((NSA_HPG * NSA_DV, t), lambda g, i: (g, i)),
            pl.BlockSpec((1, N_SLC_PAD, t), lambda g, i: (g, 0, i)),
        ),
        compiler_params=_params(("parallel", "arbitrary")),
        name="nsa_cmp",
    )(proj, kc_pad, vc_pad, a_far, tc)


def _nsa_main_kernel(q_ref, ks_ref, vst_ref, kw_ref, vwt_ref, selt_ref, tw_ref, oc_ref, gate_ref,
                     o_ref, m_ref, l_ref, acc_ref, *, n_sub):
    t = q_ref.shape[0]
    tk = n_sub * t
    g = pl.program_id(0)
    i = pl.program_id(1)
    blocks_per_tile = t // SLC_LEN
    qs = [q_ref[:, h * DKP:(h + 1) * DKP] for h in range(NSA_HPG)]

    m_ref[...] = jnp.full(m_ref.shape, -1e30, _F32)
    l_ref[...] = jnp.zeros(l_ref.shape, _F32)
    acc_ref[...] = jnp.zeros(acc_ref.shape, _F32)

    def selected(tile0, n_tiles):
        rows = [jnp.broadcast_to(selt_ref[0, pl.ds(tile0 * blocks_per_tile + b, 1), :], (SLC_LEN, t))
                for b in range(n_tiles * blocks_per_tile)]
        return jnp.concatenate(rows, axis=0) > 0.5

    def run_heads(k, vts, mask, bias):
        def logits(h):
            st = _dot_nt(k, qs[h])
            if bias is not None:
                st = st + bias(h)
            return jnp.where(mask, st, _NEG_INF)

        st = logits(0)
        for h in range(NSA_HPG):
            st_next = logits(h + 1) if h + 1 < NSA_HPG else None
            m, l, acc = _online_update_t(st, vts, m_ref[h], l_ref[h], acc_ref[h])
            m_ref[h] = m
            l_ref[h] = l
            acc_ref[h] = acc
            st = st_next

    def far_chunk(tile0, mask_extra):
        r0 = pl.multiple_of(tile0 * t, t)
        mask = selected(tile0, n_sub)
        if mask_extra is not None:
            mask = mask & mask_extra(r0)
        run_heads(ks_ref[pl.ds(r0, tk), :], [vst_ref[0, tile0 + a] for a in range(n_sub)], mask, None)

    n_far = jnp.maximum(i - 1, 0)
    n_full = n_far // n_sub

    def far_body(c, carry):
        far_chunk(c * n_sub, None)
        return carry

    lax.fori_loop(0, n_full, far_body, 0)

    @pl.when(n_far > n_full * n_sub)
    def _():
        def not_yet_covered(r0):
            kpos = r0 + lax.broadcasted_iota(jnp.int32, (tk, t), 0)
            return (kpos >= n_full * tk) & (kpos < n_far * t)

        far_chunk(jnp.maximum(n_far - n_sub, 0), not_yet_covered)

    prev = jnp.maximum(i - 1, 0)
    row2 = lax.broadcasted_iota(jnp.int32, (2 * t, t), 0)
    col2 = lax.broadcasted_iota(jnp.int32, (2 * t, t), 1)
    near_mask = (jnp.concatenate([selected(prev, 1), selected(i, 1)], axis=0)
                 & (col2 - row2 + t >= 0) & (row2 >= jnp.where(i >= 1, 0, t)))
    k_near = jnp.concatenate([ks_ref[pl.ds(pl.multiple_of(prev * t, t), t), :],
                              ks_ref[pl.ds(pl.multiple_of(i * t, t), t), :]], axis=0)
    run_heads(k_near, [vst_ref[0, prev], vst_ref[0, i]], near_mask, lambda h: tw_ref[h, 0, t:, :])

    prev2 = jnp.maximum(i - 2, 0)
    w_tiles = (prev2, prev, i)
    k_win = jnp.concatenate([kw_ref[pl.ds(pl.multiple_of(a * t, t), t), :] for a in w_tiles], axis=0)
    row3 = lax.broadcasted_iota(jnp.int32, (3 * t, t), 0)
    col3 = lax.broadcasted_iota(jnp.int32, (3 * t, t), 1)
    dist = col3 - row3 + 2 * t
    first_row = jnp.where(i >= 2, 0, jnp.where(i >= 1, t, 2 * t))
    win_mask = (dist >= 0) & (dist < WINDOW) & (row3 >= first_row)

    gates_t = jax.nn.sigmoid(gate_ref[...].astype(_F32)).T
    for h in range(NSA_HPG):
        st = jnp.where(win_mask, _dot_nt(k_win, qs[h]) + tw_ref[h, 0], _NEG_INF)
        m = jnp.max(st, axis=0, keepdims=True)
        pt = jnp.exp2(st - m)
        den = jnp.sum(pt, axis=0, keepdims=True)
        pb = pt.astype(_BF)
        o_w = functools.reduce(jnp.add, [_dot(vwt_ref[0, a], pb[n * t:(n + 1) * t, :])
                                         for n, a in enumerate(w_tiles)]) * (1.0 / den)
        o_s = acc_ref[h] * (1.0 / l_ref[h])
        o_c = oc_ref[h * NSA_DV:(h + 1) * NSA_DV, :].astype(_F32)
        gc, gs, gw = [jnp.where(g == 0, gates_t[h * N_BRANCH + b:h * N_BRANCH + b + 1, :],
                                gates_t[(NSA_HPG + h) * N_BRANCH + b:(NSA_HPG + h) * N_BRANCH + b + 1, :])
                      for b in range(N_BRANCH)]
        o_ref[:, h * NSA_DV:(h + 1) * NSA_DV] = (gc * o_c + gs * o_s + gw * o_w).T.astype(o_ref.dtype)


def _tiles_transposed(v, t):
    s, d = v.shape
    return v.reshape(s // t, t, d).transpose(0, 2, 1)


def _nsa_main(proj, sel_t, tw, oc_t, *, n_sub=4):
    s = proj.shape[0]
    t = ATT_TILE
    nb = s // t
    gw = NSA_HPG * DKP
    kn0 = COL_KN // DKP

    def values_t(branch):
        c0 = COL_VN + branch * NSA_GROUPS * NSA_DV
        return jnp.stack([_tiles_transposed(proj[:, c0 + g * NSA_DV:c0 + (g + 1) * NSA_DV], t)
                          for g in range(NSA_GROUPS)])

    return pl.pallas_call(
        functools.partial(_nsa_main_kernel, n_sub=n_sub),
        out_shape=jax.ShapeDtypeStruct((s, NSA_HEADS * NSA_DV), _BF),
        grid=(NSA_GROUPS, nb),
        in_specs=[
            pl.BlockSpec((t, gw), lambda g, i: (i, COL_QN // gw + g)),
            pl.BlockSpec((s, DKP), lambda g, i: (0, kn0 + 1 * NSA_GROUPS + g)),
            pl.BlockSpec((1, nb, NSA_DV, t), lambda g, i: (g, 0, 0, 0)),
            pl.BlockSpec((s, DKP), lambda g, i: (0, kn0 + 2 * NSA_GROUPS + g)),
            pl.BlockSpec((1, nb, NSA_DV, t), lambda g, i: (g, 0, 0, 0)),
            pl.BlockSpec((1, N_SLC_PAD, t), lambda g, i: (g, 0, i)),
            pl.BlockSpec((NSA_HPG, 1, 3 * t, t), lambda g, i: (g, 0, 0, 0)),
            pl.BlockSpec((NSA_HPG * NSA_DV, t), lambda g, i: (g, i)),
            pl.BlockSpec((t, LANES), lambda g, i: (i, COL_GATE // LANES)),
        ],
        out_specs=pl.BlockSpec((t, NSA_HPG * NSA_DV), lambda g, i: (i, g)),
        scratch_shapes=[
            pltpu.VMEM((NSA_HPG, 1, t), _F32),
            pltpu.VMEM((NSA_HPG, 1, t), _F32),
            pltpu.VMEM((NSA_HPG, NSA_DV, t), _F32),
        ],
        compiler_params=_params(("parallel", "arbitrary")),
        name="nsa_main",
    )(proj, proj, values_t(1), proj, values_t(2), sel_t, tw, oc_t, proj)


def _outproj_kernel(x_ref, mod_ref, om_ref, on_ref, w_ref, o_ref, *, mod_row):
    half = om_ref.shape[1]
    y = _dot(om_ref[...], w_ref[:half, :]) + _dot(on_ref[...], w_ref[half:, :])
    o_ref[...] = x_ref[...] + mod_ref[mod_row:mod_row + 1, :] * y


def _outproj(x, mod, o_mla, o_nsa, w_out, *, mod_row, tm=512):
    s = x.shape[0]
    return pl.pallas_call(
        functools.partial(_outproj_kernel, mod_row=mod_row),
        out_shape=jax.ShapeDtypeStruct((s, D_MODEL), _F32),
        grid=(s // tm,),
        in_specs=[
            pl.BlockSpec((tm, D_MODEL), lambda i: (i, 0)),
            pl.BlockSpec((9, D_MODEL), lambda i: (0, 0)),
            pl.BlockSpec((tm, o_mla.shape[1]), lambda i: (i, 0)),
            pl.BlockSpec((tm, o_nsa.shape[1]), lambda i: (i, 0)),
            pl.BlockSpec(w_out.shape, lambda i: (0, 0)),
        ],
        out_specs=pl.BlockSpec((tm, D_MODEL), lambda i: (i, 0)),
        compiler_params=_params(("parallel",)),
        name="outproj",
    )(x, mod, o_mla, o_nsa, w_out)


def _pad_cols(w, width):
    return jnp.pad(w, ((0, 0), (0, width - w.shape[1])))


def _pad_last(w, width):
    return jnp.pad(w, [(0, 0)] * (w.ndim - 1) + [(0, width - w.shape[-1])])


def _rotate_half_cols(w):
    half = MLA_ROPE // 2
    return jnp.concatenate([-w[..., half:], w[..., :half]], axis=-1)


def _w_in_padded(w_in):
    d = w_in.shape[0]
    sizes = (MLA_Q_RANK, MLA_KV_RANK, MLA_ROPE, NSA_HEADS * NSA_DK,
             N_BRANCH * NSA_GROUPS * NSA_DK, N_BRANCH * NSA_GROUPS * NSA_DV, NSA_HEADS * N_BRANCH)
    offs = np.concatenate([[0], np.cumsum(sizes)])
    cq, ckv, kr, qn, kn, vn, gn = [w_in[:, offs[a]:offs[a + 1]] for a in range(len(sizes))]
    qn = _pad_last(qn.reshape(d, NSA_HEADS, NSA_DK), DKP).reshape(d, NSA_HEADS * DKP)
    kn = _pad_last(kn.reshape(d, N_BRANCH * NSA_GROUPS, NSA_DK), DKP).reshape(d, -1)
    cols = [cq, _pad_cols(gn, LANES), _pad_cols(kr, LANES), ckv,
            _pad_cols(_rotate_half_cols(kr), LANES),
            jnp.zeros((d, COL_QN - COL_KR_ROT - LANES), w_in.dtype), qn, kn, vn]
    w = jnp.concatenate(cols, axis=1)
    assert w.shape[1] == D_IN_PAD
    colscale = np.ones((1, D_IN_PAD), np.float32)
    colscale[:, COL_QN:COL_KN] = NSA_DK ** -0.5 * LOG2_E
    return w.astype(_BF), jnp.asarray(colscale)


def _w_uq_padded(w_uq):
    r = w_uq.shape[0]
    w = w_uq.reshape(r, MLA_HEADS, MLA_NOPE + MLA_ROPE)
    nope = w[..., :MLA_NOPE].reshape(r, -1)
    rope = w[..., MLA_NOPE:]
    rope_p = _pad_last(rope, LANES).reshape(r, -1)
    rot_p = _pad_last(_rotate_half_cols(rope), LANES).reshape(r, -1)
    return jnp.concatenate([nope, rope_p, rot_p], axis=1).astype(_BF)


def _cmp_w1_padded(w1, d, dpad):
    return _pad_last(w1.reshape(CMP_LEN, d, CMP_HIDDEN).transpose(0, 2, 1), dpad) \
        .transpose(0, 2, 1).reshape(CMP_LEN * dpad, CMP_HIDDEN).astype(_BF)


def _rope_tables(s):
    inv = ROPE_THETA ** (-jnp.arange(0, MLA_ROPE, 2, dtype=_F32) / MLA_ROPE)
    ang = jnp.arange(s, dtype=_F32)[:, None] * inv[None, :]
    cos = _pad_cols(jnp.tile(jnp.cos(ang), (1, 2)), LANES)
    sin = _pad_cols(jnp.tile(jnp.sin(ang), (1, 2)), LANES)
    return cos, sin


def _overlap_t(s):
    s_start = np.arange(N_SLC_PAD)[:, None] * SLC_LEN
    c_start = np.arange(s // CMP_STRIDE)[None, :] * CMP_STRIDE
    ov = (c_start < s_start + SLC_LEN) & (c_start + CMP_LEN > s_start)
    return jnp.asarray(ov.astype(np.float32), _BF)


def _mixer_heads(x, mod, norm_mix, w_in, mla_q_norm, w_uq, mla_kv_norm, w_ukv, pe_cmp_k, w_cmp_k1,
                 w_cmp_k2, pe_cmp_v, w_cmp_v1, w_cmp_v2, rel_bias):
    s = x.shape[0]
    assert s % ATT_TILE == 0 and s // SLC_LEN <= N_SLC_PAD and s >= 3 * ATT_TILE
    w_pad, colscale = _w_in_padded(w_in)
    proj = _inproj(x, mod, norm_mix.reshape(1, -1), w_pad, colscale, mod_row=3)

    cos, sin = _rope_tables(s)
    q, k, v = _mla_prep(proj, mla_q_norm.reshape(1, -1), mla_kv_norm.reshape(1, -1),
                        _w_uq_padded(w_uq), w_ukv.astype(_BF), cos, sin)
    o_mla = _mla_attn(q, k, v)

    n_chunks = s // CMP_STRIDE
    kn0 = COL_KN
    k_cmp = jnp.stack([proj[:, kn0 + g * DKP:kn0 + (g + 1) * DKP] for g in range(NSA_GROUPS)])
    v_cmp = jnp.stack([proj[:, COL_VN + g * NSA_DV:COL_VN + (g + 1) * NSA_DV] for g in range(NSA_GROUPS)])
    pe_k = jnp.broadcast_to(_pad_last(pe_cmp_k, DKP).reshape(1, -1), (8, CMP_LEN * DKP)).astype(_BF)
    pe_v = jnp.broadcast_to(pe_cmp_v.reshape(1, -1), (8, CMP_LEN * NSA_DV)).astype(_BF)
    kc = _compress(k_cmp.reshape(NSA_GROUPS, n_chunks, CMP_STRIDE * DKP), pe_k,
                   _cmp_w1_padded(w_cmp_k1, NSA_DK, DKP), _pad_cols(w_cmp_k2, DKP).astype(_BF))
    vc = _compress(v_cmp.reshape(NSA_GROUPS, n_chunks, CMP_STRIDE * NSA_DV), pe_v,
                   w_cmp_v1.astype(_BF), w_cmp_v2.astype(_BF))
    front = ((0, 0), (CMP_PAD, 0), (0, 0))
    tc = _bias_tables(rel_bias, _cmp_buckets(ATT_TILE))
    tw = _bias_tables(rel_bias, _window_buckets(ATT_TILE))
    ov_t = _overlap_t(s)
    a_far = jnp.stack([jnp.concatenate([vc[g].T, ov_t], axis=0) for g in range(NSA_GROUPS)])
    oc_t, sel_t = _nsa_cmp(proj, jnp.pad(kc, front), jnp.pad(vc, front), a_far, tc)
    o_nsa = _nsa_main(proj, sel_t, tw, oc_t)
    return o_mla, o_nsa


def kernel(x, c, w_ada, b_ada, norm_ffn1, w1_gate, w1_up, w1_down, norm_mix, w_in, mla_q_norm, w_uq, mla_kv_norm, w_ukv, pe_cmp_k, w_cmp_k1, w_cmp_k2, pe_cmp_v, w_cmp_v1, w_cmp_v2, rel_bias, w_out, norm_ffn2, w2_gate, w2_up, w2_down, norm_final):
    assert x.shape[0] == 1 and w_ada.shape[0] == 1
    xs = x[0]
    fin = norm_final.reshape(1, -1)
    mod = _ada_mod(c, w_ada[0], b_ada[0]).reshape(9, D_MODEL)
    xs = _ffn(xs, mod, norm_ffn1[0].reshape(1, -1), w1_gate[0].astype(_BF), w1_up[0].astype(_BF),
              w1_down[0].astype(_BF), fin, mod_row=0, final_norm=False)
    o_mla, o_nsa = _mixer_heads(xs, mod, norm_mix[0], w_in[0], mla_q_norm[0], w_uq[0], mla_kv_norm[0],
                                w_ukv[0], pe_cmp_k[0], w_cmp_k1[0], w_cmp_k2[0], pe_cmp_v[0],
                                w_cmp_v1[0], w_cmp_v2[0], rel_bias)
    xs = _outproj(xs, mod, o_mla, o_nsa, w_out[0].astype(_BF), mod_row=5)
    xs = _ffn(xs, mod, norm_ffn2[0].reshape(1, -1), w2_gate[0].astype(_BF), w2_up[0].astype(_BF),
              w2_down[0].astype(_BF), fin, mod_row=6, final_norm=True)
    return xs[None]
```

```python
import functools
import math

import numpy as np
import jax
import jax.numpy as jnp
from jax import lax
from jax.experimental import pallas as pl
from jax.experimental.pallas import tpu as pltpu

D_MODEL = 2048
D_FF = 5632
EPS = 1e-6
MLA_HEADS = 8
MLA_Q_RANK = 768
MLA_KV_RANK = 512
MLA_NOPE = 128
MLA_ROPE = 64
MLA_V = 128
ROPE_THETA = 10000.0
NSA_HEADS = 8
NSA_GROUPS = 2
NSA_HPG = NSA_HEADS // NSA_GROUPS
NSA_DK = 192
NSA_DV = 128
CMP_LEN = 32
CMP_STRIDE = 16
CMP_HIDDEN = 256
SLC_LEN = 64
SLC_SHIFT = 6
SLC_TOPN = 16
WINDOW = 512
N_BRANCH = 3
REL_BUCKETS = 32
REL_MAX_DIST = 128

LANES = 128
VMEM_LIMIT = 56 * 1024 * 1024

DKP = 256
ATT_TILE = 256
CMP_PAD = 128
N_SLC_PAD = 128

_BF = jnp.bfloat16
_F32 = jnp.float32
_NEG_INF = float("-inf")
LOG2_E = math.log2(math.e)

COL_CQ = 0
COL_GATE = 768
COL_KR = 896
COL_CKV = 1024
COL_KR_ROT = 1536
COL_QN = 2048
COL_KN = COL_QN + NSA_HEADS * DKP
COL_VN = COL_KN + N_BRANCH * NSA_GROUPS * DKP
D_IN_PAD = COL_VN + N_BRANCH * NSA_GROUPS * NSA_DV


def _dot(a, b):
    return jnp.dot(a, b, preferred_element_type=_F32)


def _dot_nt(a, b):
    return lax.dot_general(a, b, (((1,), (1,)), ((), ())), preferred_element_type=_F32)


def _rms(x, gain):
    return x * lax.rsqrt(jnp.mean(x * x, axis=-1, keepdims=True) + EPS) * gain


def _params(semantics):
    return pltpu.CompilerParams(dimension_semantics=semantics, vmem_limit_bytes=VMEM_LIMIT)


ADA_ROWS = 256


def _ada_kernel(c_ref, w_ref, b_ref, o_ref):
    tn = o_ref.shape[1]

    def body(k, acc):
        r = pl.multiple_of(k * ADA_ROWS, ADA_ROWS)
        c = c_ref[pl.ds(r, ADA_ROWS), :]
        s = c * jax.nn.sigmoid(c)
        prod = w_ref[pl.ds(r, ADA_ROWS), :] * s
        return acc + jnp.sum(prod.reshape(ADA_ROWS // 8, 8, tn), axis=0)

    acc = lax.fori_loop(0, D_MODEL // ADA_ROWS, body, jnp.zeros((8, tn), _F32))
    o_ref[...] = jnp.sum(acc, axis=0, keepdims=True) + b_ref[...]


def _ada_mod(c, w_ada, b_ada):
    n = w_ada.shape[1]
    tn = 1024
    return pl.pallas_call(
        _ada_kernel,
        out_shape=jax.ShapeDtypeStruct((1, n), _F32),
        grid=(n // tn,),
        in_specs=[
            pl.BlockSpec((D_MODEL, 1), lambda j: (0, 0)),
            pl.BlockSpec((D_MODEL, tn), lambda j: (0, j)),
            pl.BlockSpec((1, tn), lambda j: (0, j)),
        ],
        out_specs=pl.BlockSpec((1, tn), lambda j: (0, j)),
        compiler_params=_params(("arbitrary",)),
        name="ada_mod",
    )(c.reshape(D_MODEL, 1), w_ada, b_ada.reshape(1, n))


def _ffn_kernel(x_ref, mod_ref, gain_ref, wg_ref, wu_ref, wd_ref, fin_ref, o_ref, h_ref,
                *, mod_row, final_norm):
    j = pl.program_id(1)

    @pl.when(j == 0)
    def _():
        y = _rms(x_ref[...], gain_ref[...])
        h = y * (1.0 + mod_ref[mod_row + 1:mod_row + 2, :]) + mod_ref[mod_row:mod_row + 1, :]
        h_ref[...] = h.astype(_BF)
        o_ref[...] = jnp.zeros_like(o_ref)

    h = h_ref[...]
    g = _dot(h, wg_ref[...])
    u = _dot(h, wu_ref[...])
    a = (g * jax.nn.sigmoid(g)) * u
    o_ref[...] += _dot(a.astype(_BF), wd_ref[...])

    @pl.when(j == pl.num_programs(1) - 1)
    def _():
        x2 = x_ref[...] + (0.5 * mod_ref[mod_row + 2:mod_row + 3, :]) * o_ref[...]
        if final_norm:
            x2 = _rms(x2, fin_ref[...])
        o_ref[...] = x2


def _ffn(x, mod, gain, wg, wu, wd, fin, *, mod_row, final_norm, tm=1024, tf=256):
    s = x.shape[0]
    tm = min(tm, s)
    kern = functools.partial(_ffn_kernel, mod_row=mod_row, final_norm=final_norm)
    return pl.pallas_call(
        kern,
        out_shape=jax.ShapeDtypeStruct((s, D_MODEL), _F32),
        grid=(s // tm, D_FF // tf),
        in_specs=[
            pl.BlockSpec((tm, D_MODEL), lambda i, j: (i, 0)),
            pl.BlockSpec((9, D_MODEL), lambda i, j: (0, 0)),
            pl.BlockSpec((1, D_MODEL), lambda i, j: (0, 0)),
            pl.BlockSpec((D_MODEL, tf), lambda i, j: (0, j)),
            pl.BlockSpec((D_MODEL, tf), lambda i, j: (0, j)),
            pl.BlockSpec((tf, D_MODEL), lambda i, j: (j, 0)),
            pl.BlockSpec((1, D_MODEL), lambda i, j: (0, 0)),
        ],
        out_specs=pl.BlockSpec((tm, D_MODEL), lambda i, j: (i, 0)),
        scratch_shapes=[pltpu.VMEM((tm, D_MODEL), _BF)],
        compiler_params=_params(("parallel", "arbitrary")),
        name="ffn_final" if final_norm else "ffn",
    )(x, mod, gain, wg, wu, wd, fin)


def _inproj_kernel(x_ref, mod_ref, gain_ref, w_ref, cs_ref, o_ref, h_ref, *, mod_row):
    @pl.when(pl.program_id(1) == 0)
    def _():
        y = _rms(x_ref[...], gain_ref[...])
        h = y * (1.0 + mod_ref[mod_row + 1:mod_row + 2, :]) + mod_ref[mod_row:mod_row + 1, :]
        h_ref[...] = h.astype(_BF)

    o_ref[...] = (_dot(h_ref[...], w_ref[...]) * cs_ref[...]).astype(o_ref.dtype)


def _inproj(x, mod, gain, w_pad, colscale, *, mod_row, tm=1024, tn=1280):
    s = x.shape[0]
    tm = min(tm, s)
    n = w_pad.shape[1]
    return pl.pallas_call(
        functools.partial(_inproj_kernel, mod_row=mod_row),
        out_shape=jax.ShapeDtypeStruct((s, n), _BF),
        grid=(s // tm, n // tn),
        in_specs=[
            pl.BlockSpec((tm, D_MODEL), lambda i, j: (i, 0)),
            pl.BlockSpec((9, D_MODEL), lambda i, j: (0, 0)),
            pl.BlockSpec((1, D_MODEL), lambda i, j: (0, 0)),
            pl.BlockSpec((D_MODEL, tn), lambda i, j: (0, j)),
            pl.BlockSpec((1, tn), lambda i, j: (0, j)),
        ],
        out_specs=pl.BlockSpec((tm, tn), lambda i, j: (i, j)),
        scratch_shapes=[pltpu.VMEM((tm, D_MODEL), _BF)],
        compiler_params=_params(("parallel", "arbitrary")),
        name="inproj",
    )(x, mod, gain, w_pad, colscale)


def _mla_prep_kernel(cq_ref, ckv_ref, kr_ref, krr_ref, gq_ref, gkv_ref, wq_ref, wkv_ref,
                     cos_ref, sin_ref, q_ref, k_ref, v_ref):
    scale = (MLA_NOPE + MLA_ROPE) ** -0.5 * LOG2_E
    cos = cos_ref[...]
    sin = sin_ref[...]
    hq = _rms(cq_ref[...].astype(_F32), gq_ref[...]).astype(_BF)
    qa = _dot(hq, wq_ref[...])
    nh = MLA_HEADS * LANES
    for h in range(MLA_HEADS):
        lo = h * LANES
        nope = qa[:, lo:lo + LANES]
        a = qa[:, nh + lo:nh + lo + LANES]
        b = qa[:, 2 * nh + lo:2 * nh + lo + LANES]
        q_ref[:, h * DKP:h * DKP + LANES] = (nope * scale).astype(_BF)
        q_ref[:, h * DKP + LANES:(h + 1) * DKP] = ((a * cos + b * sin) * scale).astype(_BF)
    hkv = _rms(ckv_ref[...].astype(_F32), gkv_ref[...]).astype(_BF)
    kv = _dot(hkv, wkv_ref[...])
    k_rope = (kr_ref[...].astype(_F32) * cos + krr_ref[...].astype(_F32) * sin).astype(_BF)
    for h in range(MLA_HEADS):
        lo = h * (MLA_NOPE + MLA_V)
        k_ref[:, h * DKP:h * DKP + LANES] = kv[:, lo:lo + MLA_NOPE].astype(_BF)
        k_ref[:, h * DKP + LANES:(h + 1) * DKP] = k_rope
        v_ref[:, h * MLA_V:(h + 1) * MLA_V] = kv[:, lo + MLA_NOPE:lo + MLA_NOPE + MLA_V].astype(_BF)


def _mla_prep(proj, gq, gkv, wq, wkv, cos, sin, *, tm=512):
    s = proj.shape[0]
    const = lambda i: (0, 0)
    return pl.pallas_call(
        _mla_prep_kernel,
        out_shape=(
            jax.ShapeDtypeStruct((s, MLA_HEADS * DKP), _BF),
            jax.ShapeDtypeStruct((s, MLA_HEADS * DKP), _BF),
            jax.ShapeDtypeStruct((s, MLA_HEADS * MLA_V), _BF),
        ),
        grid=(s // tm,),
        in_specs=[
            pl.BlockSpec((tm, MLA_Q_RANK), lambda i: (i, COL_CQ // MLA_Q_RANK)),
            pl.BlockSpec((tm, MLA_KV_RANK), lambda i: (i, COL_CKV // MLA_KV_RANK)),
            pl.BlockSpec((tm, LANES), lambda i: (i, COL_KR // LANES)),
            pl.BlockSpec((tm, LANES), lambda i: (i, COL_KR_ROT // LANES)),
            pl.BlockSpec((1, MLA_Q_RANK), const),
            pl.BlockSpec((1, MLA_KV_RANK), const),
            pl.BlockSpec(wq.shape, const),
            pl.BlockSpec(wkv.shape, const),
            pl.BlockSpec((tm, LANES), lambda i: (i, 0)),
            pl.BlockSpec((tm, LANES), lambda i: (i, 0)),
        ],
        out_specs=(
            pl.BlockSpec((tm, MLA_HEADS * DKP), lambda i: (i, 0)),
            pl.BlockSpec((tm, MLA_HEADS * DKP), lambda i: (i, 0)),
            pl.BlockSpec((tm, MLA_HEADS * MLA_V), lambda i: (i, 0)),
        ),
        compiler_params=_params(("parallel",)),
        name="mla_prep",
    )(proj, proj, proj, proj, gq, gkv, wq, wkv, cos, sin)


def _dot_tn(a, b):
    return lax.dot_general(a, b, (((0,), (0,)), ((), ())), preferred_element_type=_F32)


def _online_update_t(st, vs, m, l, acc):
    t = vs[0].shape[0]
    m_new = jnp.maximum(m, jnp.max(st, axis=0, keepdims=True))
    alpha = jnp.exp2(m - m_new)
    pt = jnp.exp2(st - m_new)
    l = alpha * l + jnp.sum(pt, axis=0, keepdims=True)
    pb = pt.astype(_BF)
    pv = functools.reduce(jnp.add, [_dot_tn(v, pb[a * t:(a + 1) * t, :]) for a, v in enumerate(vs)])
    return m_new, l, alpha * acc + pv


def _mla_attn_kernel(q_ref, k_ref, v_ref, o_ref, m_ref, l_ref, acc_ref, *, n_sub):
    t = q_ref.shape[0]
    tk = n_sub * t
    hp = q_ref.shape[1] // DKP
    i = pl.program_id(1)
    qs = [q_ref[:, h * DKP:(h + 1) * DKP] for h in range(hp)]
    n_full = i // n_sub
    tail_b0 = jnp.maximum(i + 1 - n_sub, 0)

    m_ref[...] = jnp.full(m_ref.shape, -1e30, _F32)
    l_ref[...] = jnp.zeros(l_ref.shape, _F32)
    acc_ref[...] = jnp.zeros(acc_ref.shape, _F32)

    def first_tile(c):
        return jnp.where(c < n_full, c * n_sub, tail_b0)

    def logits(c, h):
        r0 = pl.multiple_of(first_tile(c) * t, t)
        return _dot_nt(k_ref[pl.ds(r0, tk), h * DKP:(h + 1) * DKP], qs[h])

    def update(c, h, st):
        b0 = first_tile(c)
        vs = [v_ref[pl.ds(pl.multiple_of((b0 + a) * t, t), t), h * MLA_V:(h + 1) * MLA_V] for a in range(n_sub)]
        m, l, acc = _online_update_t(st, vs, m_ref[h], l_ref[h], acc_ref[h])
        m_ref[h] = m
        l_ref[h] = l
        acc_ref[h] = acc

    def body(c, st):
        for h in range(hp):
            st_next = logits(c, h + 1) if h + 1 < hp else logits(c + 1, 0)
            update(c, h, st)
            st = st_next
        return st

    st = lax.fori_loop(0, n_full, body, logits(0, 0))
    kpos = tail_b0 * t + lax.broadcasted_iota(jnp.int32, (tk, t), 0)
    qpos = i * t + lax.broadcasted_iota(jnp.int32, (tk, t), 1)
    tail_mask = (kpos >= n_full * tk) & (kpos <= qpos)
    for h in range(hp):
        st_next = logits(n_full, h + 1) if h + 1 < hp else None
        update(n_full, h, jnp.where(tail_mask, st, _NEG_INF))
        st = st_next
    for h in range(hp):
        o_ref[:, h * MLA_V:(h + 1) * MLA_V] = (acc_ref[h] * (1.0 / l_ref[h])).T.astype(o_ref.dtype)


def _mla_attn(q, k, v, *, n_sub=4, hp=4):
    s = q.shape[0]
    t = ATT_TILE
    nb = s // t
    return pl.pallas_call(
        functools.partial(_mla_attn_kernel, n_sub=n_sub),
        out_shape=jax.ShapeDtypeStruct((s, MLA_HEADS * MLA_V), _BF),
        grid=(MLA_HEADS // hp, nb),
        in_specs=[
            pl.BlockSpec((t, hp * DKP), lambda h, i: (i, h)),
            pl.BlockSpec((s, hp * DKP), lambda h, i: (0, h), pipeline_mode=pl.Buffered(1)),
            pl.BlockSpec((s, hp * MLA_V), lambda h, i: (0, h), pipeline_mode=pl.Buffered(1)),
        ],
        out_specs=pl.BlockSpec((t, hp * MLA_V), lambda h, i: (i, h)),
        scratch_shapes=[
            pltpu.VMEM((hp, 1, t), _F32),
            pltpu.VMEM((hp, 1, t), _F32),
            pltpu.VMEM((hp, MLA_V, t), _F32),
        ],
        compiler_params=_params(("parallel", "arbitrary")),
        name="mla_attn",
    )(q, k, v)


def _t5_bucket_np(dist):
    n = np.maximum(dist, 0)
    max_exact = REL_BUCKETS // 2
    nf = np.maximum(n, 1).astype(np.float32)
    large = max_exact + (np.log(nf / np.float32(max_exact)) / np.float32(math.log(REL_MAX_DIST / max_exact))
                         * np.float32(REL_BUCKETS - max_exact)).astype(np.int32)
    large = np.minimum(large, REL_BUCKETS - 1)
    return np.where(n < max_exact, n, large).astype(np.int32)


def _bias_table_kernel(rb_ref, bk_ref, o_ref):
    h = pl.program_id(0)
    bk = bk_ref[0]
    base = rb_ref[REL_BUCKETS - 1, h]
    acc = jnp.zeros(bk.shape, _F32)
    for b in range(REL_BUCKETS - 1):
        acc = jnp.where(bk == b, (rb_ref[b, h] - base) * LOG2_E, acc)
    o_ref[0, 0] = acc


def _bias_tables(rel_bias, buckets):
    nt, r, c = buckets.shape
    return pl.pallas_call(
        _bias_table_kernel,
        out_shape=jax.ShapeDtypeStruct((NSA_HEADS, nt, r, c), _F32),
        grid=(NSA_HEADS, nt),
        in_specs=[
            pl.BlockSpec(memory_space=pltpu.SMEM),
            pl.BlockSpec((1, r, c), lambda h, t: (t, 0, 0)),
        ],
        out_specs=pl.BlockSpec((1, 1, r, c), lambda h, t: (h, t, 0, 0)),
        compiler_params=_params(("arbitrary", "arbitrary")),
        name="bias_tables",
    )(rel_bias, jnp.asarray(buckets))


def _window_buckets(t):
    r = np.arange(3 * t)[:, None]
    c = np.arange(t)[None, :]
    return _t5_bucket_np(c - r + 2 * t)[None]


def _cmp_buckets(t):
    c = np.arange(CMP_PAD)[:, None]
    r = np.arange(t)[None, :]
    return _t5_bucket_np(r - (CMP_LEN - 1) - t + CMP_STRIDE * (CMP_PAD - c))[None]


def _compress_kernel(ch_ref, pe_ref, w1_ref, w2_ref, o_ref):
    half = w1_ref.shape[0] // 2
    ch = ch_ref[0]
    a = _dot(ch, w1_ref[:half, :])
    b = _dot(ch, w1_ref[half:, :])
    n = ch.shape[0]
    b_next = pltpu.roll(b, n - 1, axis=0)
    c0 = _dot(pe_ref[...], w1_ref[...])[0:1, :]
    hid = a + b_next + c0
    hid = hid * jax.nn.sigmoid(hid)
    o_ref[0] = _dot(hid.astype(_BF), w2_ref[...]).astype(o_ref.dtype)


def _compress(chunks, pe_flat, w1, w2):
    g, n, kd = chunks.shape
    d = w2.shape[1]
    return pl.pallas_call(
        _compress_kernel,
        out_shape=jax.ShapeDtypeStruct((g, n, d), _BF),
        grid=(g,),
        in_specs=[
            pl.BlockSpec((1, n, kd), lambda i: (i, 0, 0)),
            pl.BlockSpec(pe_flat.shape, lambda i: (0, 0)),
            pl.BlockSpec(w1.shape, lambda i: (0, 0)),
            pl.BlockSpec(w2.shape, lambda i: (0, 0)),
        ],
        out_specs=pl.BlockSpec((1, n, d), lambda i: (i, 0, 0)),
        compiler_params=_params(("arbitrary",)),
        name="nsa_compress",
    )(chunks, pe_flat, w1, w2)


def _nsa_cmp_kernel(q_ref, kc_ref, vc_ref, afar_ref, tc_ref, oc_ref, sel_ref):
    t = q_ref.shape[0]
    nc = kc_ref.shape[1] - CMP_PAD
    per_tile = t // CMP_STRIDE
    i = pl.program_id(1)
    thr = (i + 1) * per_tile - CMP_PAD
    near0 = pl.multiple_of((i + 1) * per_tile, per_tile)

    k_ext = jnp.concatenate([kc_ref[0, CMP_PAD:, :], kc_ref[0, pl.ds(near0, CMP_PAD), :]], axis=0)
    vcn_t = vc_ref[0, pl.ds(near0, CMP_PAD), :].astype(_F32).T.astype(_BF)
    jn = lax.broadcasted_iota(jnp.int32, (N_SLC_PAD, CMP_PAD), 0) * SLC_LEN
    cn = (thr + lax.broadcasted_iota(jnp.int32, (N_SLC_PAD, CMP_PAD), 1)) * CMP_STRIDE
    ovn_t = jnp.where((cn < jn + SLC_LEN) & (cn + CMP_LEN > jn), 1.0, 0.0).astype(_BF)
    a_near = jnp.concatenate([vcn_t, ovn_t], axis=0)
    a_far = afar_ref[0]

    far_mask = lax.broadcasted_iota(jnp.int32, (nc, t), 0) < thr
    c = lax.broadcasted_iota(jnp.int32, (CMP_PAD, t), 0)
    r = lax.broadcasted_iota(jnp.int32, (CMP_PAD, t), 1)
    near_mask = (CMP_STRIDE * c <= r - (CMP_LEN - 1) - t + CMP_STRIDE * CMP_PAD) & (c >= -thr)

    imp = jnp.zeros((N_SLC_PAD, t), _F32)
    for h in range(NSA_HPG):
        st = _dot_nt(k_ext, q_ref[:, h * DKP:(h + 1) * DKP])
        sf = jnp.where(far_mask, st[:nc], _NEG_INF)
        sn = jnp.where(near_mask, st[nc:] + tc_ref[h, 0], _NEG_INF)
        m = jnp.maximum(jnp.max(sf, axis=0, keepdims=True), jnp.max(sn, axis=0, keepdims=True))
        m = jnp.where(m == _NEG_INF, 0.0, m)
        pf = jnp.exp2(sf - m)
        pn = jnp.exp2(sn - m)
        den = jnp.sum(pf, axis=0, keepdims=True) + jnp.sum(pn, axis=0, keepdims=True)
        inv = 1.0 / jnp.maximum(den, 1e-30)
        pf = pf * inv
        pn = pn * inv
        pf_hi = pf.astype(_BF)
        pn_hi = pn.astype(_BF)
        res = _dot(a_far, pf_hi) + _dot(a_near, pn_hi)
        oc_ref[h * NSA_DV:(h + 1) * NSA_DV, :] = res[:NSA_DV].astype(oc_ref.dtype)
        pf_lo = (pf - pf_hi.astype(_F32)).astype(_BF)
        pn_lo = (pn - pn_hi.astype(_F32)).astype(_BF)
        imp = imp + res[NSA_DV:] + (_dot(a_far[NSA_DV:], pf_lo) + _dot(a_near[NSA_DV:], pn_lo))

    jj = lax.broadcasted_iota(jnp.int32, (N_SLC_PAD, t), 0)
    tq = i * t + lax.broadcasted_iota(jnp.int32, (N_SLC_PAD, t), 1)
    cur = tq >> SLC_SHIFT
    valid = jj * SLC_LEN <= tq
    forced = (jj == 0) | ((jj <= cur) & (jj >= cur - 1))
    work = jnp.where(valid, jnp.where(forced, jnp.inf, imp), _NEG_INF)
    blk = jj.astype(_F32)
    sel = jnp.zeros((N_SLC_PAD, t), _F32)
    for _ in range(SLC_TOPN):
        mx = jnp.max(work, axis=0, keepdims=True)
        first = jnp.min(jnp.where(work == mx, blk, float(N_SLC_PAD)), axis=0, keepdims=True)
        hit = blk == first
        sel = jnp.where(hit & (mx > _NEG_INF), 1.0, sel)
        work = jnp.where(hit, _NEG_INF, work)
    sel_ref[0] = sel


def _nsa_cmp(proj, kc_pad, vc_pad, a_far, tc):
    s = proj.shape[0]
    t = ATT_TILE
    gw = NSA_HPG * DKP
    return pl.pallas_call(
        _nsa_cmp_kernel,
        out_shape=(
            jax.ShapeDtypeStruct((NSA_HEADS * NSA_DV, s), _BF),
            jax.ShapeDtypeStruct((NSA_GROUPS, N_SLC_PAD, s), _F32),
        ),
        grid=(NSA_GROUPS, s // t),
        in_specs=[
            pl.BlockSpec((t, gw), lambda g, i: (i, COL_QN // gw + g)),
            pl.BlockSpec((1,) + kc_pad.shape[1:], lambda g, i: (g, 0, 0)),
            pl.BlockSpec((1,) + vc_pad.shape[1:], lambda g, i: (g, 0, 0)),
            pl.BlockSpec((1,) + a_far.shape[1:], lambda g, i: (g, 0, 0)),
            pl.BlockSpec((NSA_HPG, 1, CMP_PAD, t), lambda g, i: (g, 0, 0, 0)),
        ],
        out_specs=(
            pl.BlockSpec((NSA_HPG * NSA_DV, t), lambda g, i: (g, i)),
            pl.BlockSpec((1, N_SLC_PAD, t), lambda g, i: (g, 0, i)),
        ),
        compiler_params=_params(("parallel", "arbitrary")),
        name="nsa_cmp",
    )(proj, kc_pad, vc_pad, a_far, tc)


def _nsa_main_kernel(q_ref, ks_ref, vs_ref, kw_ref, vw_ref, selt_ref, tw_ref, oc_ref, gate_ref,
                     o_ref, m_ref, l_ref, acc_ref, *, n_sub):
    t = q_ref.shape[0]

    def value_tile(ref, tile):
        return ref[pl.ds(pl.multiple_of(tile * t, t), t), :]

    tk = n_sub * t
    g = pl.program_id(0)
    i = pl.program_id(1)
    blocks_per_tile = t // SLC_LEN
    qs = [q_ref[:, h * DKP:(h + 1) * DKP] for h in range(NSA_HPG)]

    m_ref[...] = jnp.full(m_ref.shape, -1e30, _F32)
    l_ref[...] = jnp.zeros(l_ref.shape, _F32)
    acc_ref[...] = jnp.zeros(acc_ref.shape, _F32)

    def selected(tile0, n_tiles):
        rows = [jnp.broadcast_to(selt_ref[0, pl.ds(tile0 * blocks_per_tile + b, 1), :], (SLC_LEN, t))
                for b in range(n_tiles * blocks_per_tile)]
        return jnp.concatenate(rows, axis=0) > 0.5

    def run_heads(k, vts, mask, bias):
        def logits(h):
            st = _dot_nt(k, qs[h])
            if bias is not None:
                st = st + bias(h)
            return jnp.where(mask, st, _NEG_INF)

        st = logits(0)
        for h in range(NSA_HPG):
            st_next = logits(h + 1) if h + 1 < NSA_HPG else None
            m, l, acc = _online_update_t(st, vts, m_ref[h], l_ref[h], acc_ref[h])
            m_ref[h] = m
            l_ref[h] = l
            acc_ref[h] = acc
            st = st_next

    def far_chunk(tile0, mask_extra):
        r0 = pl.multiple_of(tile0 * t, t)
        mask = selected(tile0, n_sub)
        if mask_extra is not None:
            mask = mask & mask_extra(r0)
        run_heads(ks_ref[pl.ds(r0, tk), :], [value_tile(vs_ref, tile0 + a) for a in range(n_sub)], mask, None)

    n_far = jnp.maximum(i - 1, 0)
    n_full = n_far // n_sub

    def far_body(c, carry):
        far_chunk(c * n_sub, None)
        return carry

    lax.fori_loop(0, n_full, far_body, 0)

    @pl.when(n_far > n_full * n_sub)
    def _():
        def not_yet_covered(r0):
            kpos = r0 + lax.broadcasted_iota(jnp.int32, (tk, t), 0)
            return (kpos >= n_full * tk) & (kpos < n_far * t)

        far_chunk(jnp.maximum(n_far - n_sub, 0), not_yet_covered)

    prev = jnp.maximum(i - 1, 0)
    row2 = lax.broadcasted_iota(jnp.int32, (2 * t, t), 0)
    col2 = lax.broadcasted_iota(jnp.int32, (2 * t, t), 1)
    near_mask = (jnp.concatenate([selected(prev, 1), selected(i, 1)], axis=0)
                 & (col2 - row2 + t >= 0) & (row2 >= jnp.where(i >= 1, 0, t)))
    k_near = jnp.concatenate([ks_ref[pl.ds(pl.multiple_of(prev * t, t), t), :],
                              ks_ref[pl.ds(pl.multiple_of(i * t, t), t), :]], axis=0)
    run_heads(k_near, [value_tile(vs_ref, prev), value_tile(vs_ref, i)], near_mask,
              lambda h: tw_ref[h, 0, t:, :])

    prev2 = jnp.maximum(i - 2, 0)
    w_tiles = (prev2, prev, i)
    k_win = jnp.concatenate([kw_ref[pl.ds(pl.multiple_of(a * t, t), t), :] for a in w_tiles], axis=0)
    row3 = lax.broadcasted_iota(jnp.int32, (3 * t, t), 0)
    col3 = lax.broadcasted_iota(jnp.int32, (3 * t, t), 1)
    dist = col3 - row3 + 2 * t
    first_row = jnp.where(i >= 2, 0, jnp.where(i >= 1, t, 2 * t))
    win_mask = (dist >= 0) & (dist < WINDOW) & (row3 >= first_row)

    gates_t = jax.nn.sigmoid(gate_ref[...].astype(_F32)).T
    for h in range(NSA_HPG):
        st = jnp.where(win_mask, _dot_nt(k_win, qs[h]) + tw_ref[h, 0], _NEG_INF)
        m = jnp.max(st, axis=0, keepdims=True)
        pt = jnp.exp2(st - m)
        den = jnp.sum(pt, axis=0, keepdims=True)
        pb = pt.astype(_BF)
        o_w = functools.reduce(jnp.add, [_dot_tn(value_tile(vw_ref, a), pb[n * t:(n + 1) * t, :])
                                         for n, a in enumerate(w_tiles)]) * (1.0 / den)
        o_s = acc_ref[h] * (1.0 / l_ref[h])
        o_c = oc_ref[h * NSA_DV:(h + 1) * NSA_DV, :].astype(_F32)
        gc, gs, gw = [jnp.where(g == 0, gates_t[h * N_BRANCH + b:h * N_BRANCH + b + 1, :],
                                gates_t[(NSA_HPG + h) * N_BRANCH + b:(NSA_HPG + h) * N_BRANCH + b + 1, :])
                      for b in range(N_BRANCH)]
        o_ref[:, h * NSA_DV:(h + 1) * NSA_DV] = (gc * o_c + gs * o_s + gw * o_w).T.astype(o_ref.dtype)


def _nsa_main(proj, sel_t, tw, oc_t, *, n_sub=4):
    s = proj.shape[0]
    t = ATT_TILE
    nb = s // t
    gw = NSA_HPG * DKP
    kn0 = COL_KN // DKP
    vn0 = COL_VN // NSA_DV
    return pl.pallas_call(
        functools.partial(_nsa_main_kernel, n_sub=n_sub),
        out_shape=jax.ShapeDtypeStruct((s, NSA_HEADS * NSA_DV), _BF),
        grid=(NSA_GROUPS, nb),
        in_specs=[
            pl.BlockSpec((t, gw), lambda g, i: (i, COL_QN // gw + g)),
            pl.BlockSpec((s, DKP), lambda g, i: (0, kn0 + 1 * NSA_GROUPS + g)),
            pl.BlockSpec((s, NSA_DV), lambda g, i: (0, vn0 + 1 * NSA_GROUPS + g)),
            pl.BlockSpec((s, DKP), lambda g, i: (0, kn0 + 2 * NSA_GROUPS + g)),
            pl.BlockSpec((s, NSA_DV), lambda g, i: (0, vn0 + 2 * NSA_GROUPS + g)),
            pl.BlockSpec((1, N_SLC_PAD, t), lambda g, i: (g, 0, i)),
            pl.BlockSpec((NSA_HPG, 1, 3 * t, t), lambda g, i: (g, 0, 0, 0)),
            pl.BlockSpec((NSA_HPG * NSA_DV, t), lambda g, i: (g, i)),
            pl.BlockSpec((t, LANES), lambda g, i: (i, COL_GATE // LANES)),
        ],
        out_specs=pl.BlockSpec((t, NSA_HPG * NSA_DV), lambda g, i: (i, g)),
        scratch_shapes=[
            pltpu.VMEM((NSA_HPG, 1, t), _F32),
            pltpu.VMEM((NSA_HPG, 1, t), _F32),
            pltpu.VMEM((NSA_HPG, NSA_DV, t), _F32),
        ],
        compiler_params=_params(("parallel", "arbitrary")),
        name="nsa_main",
    )(proj, proj, proj, proj, proj, sel_t, tw, oc_t, proj)


def _outproj_kernel(x_ref, mod_ref, om_ref, on_ref, w_ref, o_ref, *, mod_row):
    half = om_ref.shape[1]
    y = _dot(om_ref[...], w_ref[:half, :]) + _dot(on_ref[...], w_ref[half:, :])
    o_ref[...] = x_ref[...] + mod_ref[mod_row:mod_row + 1, :] * y


def _outproj(x, mod, o_mla, o_nsa, w_out, *, mod_row, tm=512):
    s = x.shape[0]
    return pl.pallas_call(
        functools.partial(_outproj_kernel, mod_row=mod_row),
        out_shape=jax.ShapeDtypeStruct((s, D_MODEL), _F32),
        grid=(s // tm,),
        in_specs=[
            pl.BlockSpec((tm, D_MODEL), lambda i: (i, 0)),
            pl.BlockSpec((9, D_MODEL), lambda i: (0, 0)),
            pl.BlockSpec((tm, o_mla.shape[1]), lambda i: (i, 0)),
            pl.BlockSpec((tm, o_nsa.shape[1]), lambda i: (i, 0)),
            pl.BlockSpec(w_out.shape, lambda i: (0, 0)),
        ],
        out_specs=pl.BlockSpec((tm, D_MODEL), lambda i: (i, 0)),
        compiler_params=_params(("parallel",)),
        name="outproj",
    )(x, mod, o_mla, o_nsa, w_out)


def _pad_cols(w, width):
    return jnp.pad(w, ((0, 0), (0, width - w.shape[1])))


def _pad_last(w, width):
    return jnp.pad(w, [(0, 0)] * (w.ndim - 1) + [(0, width - w.shape[-1])])


def _rotate_half_cols(w):
    half = MLA_ROPE // 2
    return jnp.concatenate([-w[..., half:], w[..., :half]], axis=-1)


def _w_in_padded(w_in):
    d = w_in.shape[0]
    sizes = (MLA_Q_RANK, MLA_KV_RANK, MLA_ROPE, NSA_HEADS * NSA_DK,
             N_BRANCH * NSA_GROUPS * NSA_DK, N_BRANCH * NSA_GROUPS * NSA_DV, NSA_HEADS * N_BRANCH)
    offs = np.concatenate([[0], np.cumsum(sizes)])
    cq, ckv, kr, qn, kn, vn, gn = [w_in[:, offs[a]:offs[a + 1]] for a in range(len(sizes))]
    qn = _pad_last(qn.reshape(d, NSA_HEADS, NSA_DK), DKP).reshape(d, NSA_HEADS * DKP)
    kn = _pad_last(kn.reshape(d, N_BRANCH * NSA_GROUPS, NSA_DK), DKP).reshape(d, -1)
    cols = [cq, _pad_cols(gn, LANES), _pad_cols(kr, LANES), ckv,
            _pad_cols(_rotate_half_cols(kr), LANES),
            jnp.zeros((d, COL_QN - COL_KR_ROT - LANES), w_in.dtype), qn, kn, vn]
    w = jnp.concatenate(cols, axis=1)
    assert w.shape[1] == D_IN_PAD
    colscale = np.ones((1, D_IN_PAD), np.float32)
    colscale[:, COL_QN:COL_KN] = NSA_DK ** -0.5 * LOG2_E
    return w.astype(_BF), jnp.asarray(colscale)


def _w_uq_padded(w_uq):
    r = w_uq.shape[0]
    w = w_uq.reshape(r, MLA_HEADS, MLA_NOPE + MLA_ROPE)
    nope = w[..., :MLA_NOPE].reshape(r, -1)
    rope = w[..., MLA_NOPE:]
    rope_p = _pad_last(rope, LANES).reshape(r, -1)
    rot_p = _pad_last(_rotate_half_cols(rope), LANES).reshape(r, -1)
    return jnp.concatenate([nope, rope_p, rot_p], axis=1).astype(_BF)


def _cmp_w1_padded(w1, d, dpad):
    return _pad_last(w1.reshape(CMP_LEN, d, CMP_HIDDEN).transpose(0, 2, 1), dpad) \
        .transpose(0, 2, 1).reshape(CMP_LEN * dpad, CMP_HIDDEN).astype(_BF)


def _rope_tables(s):
    inv = ROPE_THETA ** (-jnp.arange(0, MLA_ROPE, 2, dtype=_F32) / MLA_ROPE)
    ang = jnp.arange(s, dtype=_F32)[:, None] * inv[None, :]
    cos = _pad_cols(jnp.tile(jnp.cos(ang), (1, 2)), LANES)
    sin = _pad_cols(jnp.tile(jnp.sin(ang), (1, 2)), LANES)
    return cos, sin


def _overlap_t(s):
    s_start = np.arange(N_SLC_PAD)[:, None] * SLC_LEN
    c_start = np.arange(s // CMP_STRIDE)[None, :] * CMP_STRIDE
    ov = (c_start < s_start + SLC_LEN) & (c_start + CMP_LEN > s_start)
    return jnp.asarray(ov.astype(np.float32), _BF)


def _mixer_heads(x, mod, norm_mix, w_in, mla_q_norm, w_uq, mla_kv_norm, w_ukv, pe_cmp_k, w_cmp_k1,
                 w_cmp_k2, pe_cmp_v, w_cmp_v1, w_cmp_v2, rel_bias):
    s = x.shape[0]
    assert s % ATT_TILE == 0 and s // SLC_LEN <= N_SLC_PAD and s >= 3 * ATT_TILE
    w_pad, colscale = _w_in_padded(w_in)
    proj = _inproj(x, mod, norm_mix.reshape(1, -1), w_pad, colscale, mod_row=3)

    cos, sin = _rope_tables(s)
    q, k, v = _mla_prep(proj, mla_q_norm.reshape(1, -1), mla_kv_norm.reshape(1, -1),
                        _w_uq_padded(w_uq), w_ukv.astype(_BF), cos, sin)
    o_mla = _mla_attn(q, k, v)

    n_chunks = s // CMP_STRIDE
    kn0 = COL_KN
    k_cmp = jnp.stack([proj[:, kn0 + g * DKP:kn0 + (g + 1) * DKP] for g in range(NSA_GROUPS)])
    v_cmp = jnp.stack([proj[:, COL_VN + g * NSA_DV:COL_VN + (g + 1) * NSA_DV] for g in range(NSA_GROUPS)])
    pe_k = jnp.broadcast_to(_pad_last(pe_cmp_k, DKP).reshape(1, -1), (8, CMP_LEN * DKP)).astype(_BF)
    pe_v = jnp.broadcast_to(pe_cmp_v.reshape(1, -1), (8, CMP_LEN * NSA_DV)).astype(_BF)
    kc = _compress(k_cmp.reshape(NSA_GROUPS, n_chunks, CMP_STRIDE * DKP), pe_k,
                   _cmp_w1_padded(w_cmp_k1, NSA_DK, DKP), _pad_cols(w_cmp_k2, DKP).astype(_BF))
    vc = _compress(v_cmp.reshape(NSA_GROUPS, n_chunks, CMP_STRIDE * NSA_DV), pe_v,
                   w_cmp_v1.astype(_BF), w_cmp_v2.astype(_BF))
    front = ((0, 0), (CMP_PAD, 0), (0, 0))
    tc = _bias_tables(rel_bias, _cmp_buckets(ATT_TILE))
    tw = _bias_tables(rel_bias, _window_buckets(ATT_TILE))
    ov_t = _overlap_t(s)
    a_far = jnp.stack([jnp.concatenate([vc[g].T, ov_t], axis=0) for g in range(NSA_GROUPS)])
    oc_t, sel_t = _nsa_cmp(proj, jnp.pad(kc, front), jnp.pad(vc, front), a_far, tc)
    o_nsa = _nsa_main(proj, sel_t, tw, oc_t)
    return o_mla, o_nsa


def kernel(x, c, w_ada, b_ada, norm_ffn1, w1_gate, w1_up, w1_down, norm_mix, w_in, mla_q_norm, w_uq, mla_kv_norm, w_ukv, pe_cmp_k, w_cmp_k1, w_cmp_k2, pe_cmp_v, w_cmp_v1, w_cmp_v2, rel_bias, w_out, norm_ffn2, w2_gate, w2_up, w2_down, norm_final):
    assert x.shape[0] == 1 and w_ada.shape[0] == 1
    xs = x[0]
    fin = norm_final.reshape(1, -1)
    mod = _ada_mod(c, w_ada[0], b_ada[0]).reshape(9, D_MODEL)
    xs = _ffn(xs, mod, norm_ffn1[0].reshape(1, -1), w1_gate[0].astype(_BF), w1_up[0].astype(_BF),
              w1_down[0].astype(_BF), fin, mod_row=0, final_norm=False)
    o_mla, o_nsa = _mixer_heads(xs, mod, norm_mix[0], w_in[0], mla_q_norm[0], w_uq[0], mla_kv_norm[0],
                                w_ukv[0], pe_cmp_k[0], w_cmp_k1[0], w_cmp_k2[0], pe_cmp_v[0],
                                w_cmp_v1[0], w_cmp_v2[0], rel_bias)
    xs = _outproj(xs, mod, o_mla, o_nsa, w_out[0].astype(_BF), mod_row=5)
    xs = _ffn(xs, mod, norm_ffn2[0].reshape(1, -1), w2_gate[0].astype(_BF), w2_up[0].astype(_BF),
              w2_down[0].astype(_BF), fin, mod_row=6, final_norm=True)
    return xs[None]
```

```python
import functools
import math

import numpy as np
import jax
import jax.numpy as jnp
from jax import lax
from jax.experimental import pallas as pl
from jax.experimental.pallas import tpu as pltpu

D_MODEL = 2048
D_FF = 5632
EPS = 1e-6
MLA_HEADS = 8
MLA_Q_RANK = 768
MLA_KV_RANK = 512
MLA_NOPE = 128
MLA_ROPE = 64
MLA_V = 128
ROPE_THETA = 10000.0
NSA_HEADS = 8
NSA_GROUPS = 2
NSA_HPG = NSA_HEADS // NSA_GROUPS
NSA_DK = 192
NSA_DV = 128
CMP_LEN = 32
CMP_STRIDE = 16
CMP_HIDDEN = 256
SLC_LEN = 64
SLC_SHIFT = 6
SLC_TOPN = 16
WINDOW = 512
N_BRANCH = 3
REL_BUCKETS = 32
REL_MAX_DIST = 128

LANES = 128
VMEM_LIMIT = 60000 * 1024

DKP = 256
ATT_TILE = 256
CMP_PAD = 128
N_SLC_PAD = 128

_BF = jnp.bfloat16
_F32 = jnp.float32
_NEG_INF = float("-inf")
LOG2_E = math.log2(math.e)

COL_CQ = 0
COL_GATE = 768
COL_KR = 896
COL_CKV = 1024
COL_KR_ROT = 1536
COL_QN = 2048
COL_KN = COL_QN + NSA_HEADS * DKP
COL_VN = COL_KN + N_BRANCH * NSA_GROUPS * DKP
D_IN_PAD = COL_VN + N_BRANCH * NSA_GROUPS * NSA_DV


def _dot(a, b):
    return jnp.dot(a, b, preferred_element_type=_F32)


def _dot_nt(a, b):
    return lax.dot_general(a, b, (((1,), (1,)), ((), ())), preferred_element_type=_F32)


def _rms(x, gain):
    return x * lax.rsqrt(jnp.mean(x * x, axis=-1, keepdims=True) + EPS) * gain


def _params(semantics):
    return pltpu.CompilerParams(dimension_semantics=semantics, vmem_limit_bytes=VMEM_LIMIT)


ADA_ROWS = 256


def _ada_kernel(c_ref, w_ref, b_ref, o_ref):
    tn = o_ref.shape[1]

    def body(k, acc):
        r = pl.multiple_of(k * ADA_ROWS, ADA_ROWS)
        c = c_ref[pl.ds(r, ADA_ROWS), :]
        s = c * jax.nn.sigmoid(c)
        prod = w_ref[pl.ds(r, ADA_ROWS), :] * s
        return acc + jnp.sum(prod.reshape(ADA_ROWS // 8, 8, tn), axis=0)

    acc = lax.fori_loop(0, D_MODEL // ADA_ROWS, body, jnp.zeros((8, tn), _F32))
    o_ref[...] = jnp.sum(acc, axis=0, keepdims=True) + b_ref[...]


def _ada_mod(c, w_ada, b_ada):
    n = w_ada.shape[1]
    tn = 1024
    return pl.pallas_call(
        _ada_kernel,
        out_shape=jax.ShapeDtypeStruct((1, n), _F32),
        grid=(n // tn,),
        in_specs=[
            pl.BlockSpec((D_MODEL, 1), lambda j: (0, 0)),
            pl.BlockSpec((D_MODEL, tn), lambda j: (0, j)),
            pl.BlockSpec((1, tn), lambda j: (0, j)),
        ],
        out_specs=pl.BlockSpec((1, tn), lambda j: (0, j)),
        compiler_params=_params(("arbitrary",)),
        name="ada_mod",
    )(c.reshape(D_MODEL, 1), w_ada, b_ada.reshape(1, n))


def _ffn_kernel(x_ref, mod_ref, gain_ref, wg_ref, wu_ref, wd_ref, fin_ref, o_ref, h_ref,
                *, mod_row, final_norm):
    j = pl.program_id(1)

    @pl.when(j == 0)
    def _():
        y = _rms(x_ref[...], gain_ref[...])
        h = y * (1.0 + mod_ref[mod_row + 1:mod_row + 2, :]) + mod_ref[mod_row:mod_row + 1, :]
        h_ref[...] = h.astype(_BF)
        o_ref[...] = jnp.zeros_like(o_ref)

    h = h_ref[...]
    g = _dot(h, wg_ref[...].astype(_BF))
    u = _dot(h, wu_ref[...].astype(_BF))
    a = (g * jax.nn.sigmoid(g)) * u
    o_ref[...] += _dot(a.astype(_BF), wd_ref[...].astype(_BF))

    @pl.when(j == pl.num_programs(1) - 1)
    def _():
        x2 = x_ref[...] + (0.5 * mod_ref[mod_row + 2:mod_row + 3, :]) * o_ref[...]
        if final_norm:
            x2 = _rms(x2, fin_ref[...])
        o_ref[...] = x2


def _ffn(x, mod, gain, wg, wu, wd, fin, *, mod_row, final_norm, tm=1024, tf=256):
    s = x.shape[0]
    tm = min(tm, s)
    kern = functools.partial(_ffn_kernel, mod_row=mod_row, final_norm=final_norm)
    return pl.pallas_call(
        kern,
        out_shape=jax.ShapeDtypeStruct((s, D_MODEL), _F32),
        grid=(s // tm, D_FF // tf),
        in_specs=[
            pl.BlockSpec((tm, D_MODEL), lambda i, j: (i, 0)),
            pl.BlockSpec((9, D_MODEL), lambda i, j: (0, 0)),
            pl.BlockSpec((1, D_MODEL), lambda i, j: (0, 0)),
            pl.BlockSpec((D_MODEL, tf), lambda i, j: (0, j)),
            pl.BlockSpec((D_MODEL, tf), lambda i, j: (0, j)),
            pl.BlockSpec((tf, D_MODEL), lambda i, j: (j, 0)),
            pl.BlockSpec((1, D_MODEL), lambda i, j: (0, 0)),
        ],
        out_specs=pl.BlockSpec((tm, D_MODEL), lambda i, j: (i, 0)),
        scratch_shapes=[pltpu.VMEM((tm, D_MODEL), _BF)],
        compiler_params=_params(("parallel", "arbitrary")),
        name="ffn_final" if final_norm else "ffn",
    )(x, mod, gain, wg, wu, wd, fin)


def _inproj_kernel(x_ref, mod_ref, gain_ref, w_ref, cs_ref, o_ref, h_ref, *, mod_row):
    @pl.when(pl.program_id(1) == 0)
    def _():
        y = _rms(x_ref[...], gain_ref[...])
        h = y * (1.0 + mod_ref[mod_row + 1:mod_row + 2, :]) + mod_ref[mod_row:mod_row + 1, :]
        h_ref[...] = h.astype(_BF)

    o_ref[...] = (_dot(h_ref[...], w_ref[...]) * cs_ref[...]).astype(o_ref.dtype)


def _inproj(x, mod, gain, w_pad, colscale, *, mod_row, tm=1024, tn=1280):
    s = x.shape[0]
    tm = min(tm, s)
    n = w_pad.shape[1]
    return pl.pallas_call(
        functools.partial(_inproj_kernel, mod_row=mod_row),
        out_shape=jax.ShapeDtypeStruct((s, n), _BF),
        grid=(s // tm, n // tn),
        in_specs=[
            pl.BlockSpec((tm, D_MODEL), lambda i, j: (i, 0)),
            pl.BlockSpec((9, D_MODEL), lambda i, j: (0, 0)),
            pl.BlockSpec((1, D_MODEL), lambda i, j: (0, 0)),
            pl.BlockSpec((D_MODEL, tn), lambda i, j: (0, j)),
            pl.BlockSpec((1, tn), lambda i, j: (0, j)),
        ],
        out_specs=pl.BlockSpec((tm, tn), lambda i, j: (i, j)),
        scratch_shapes=[pltpu.VMEM((tm, D_MODEL), _BF)],
        compiler_params=_params(("parallel", "arbitrary")),
        name="inproj",
    )(x, mod, gain, w_pad, colscale)


def _mla_prep_kernel(cq_ref, ckv_ref, kr_ref, krr_ref, gq_ref, gkv_ref, wq_ref, wkv_ref,
                     cos_ref, sin_ref, q_ref, k_ref, v_ref):
    scale = (MLA_NOPE + MLA_ROPE) ** -0.5 * LOG2_E
    cos = cos_ref[...]
    sin = sin_ref[...]
    hq = _rms(cq_ref[...].astype(_F32), gq_ref[...]).astype(_BF)
    qa = _dot(hq, wq_ref[...])
    nh = MLA_HEADS * LANES
    for h in range(MLA_HEADS):
        lo = h * LANES
        nope = qa[:, lo:lo + LANES]
        a = qa[:, nh + lo:nh + lo + LANES]
        b = qa[:, 2 * nh + lo:2 * nh + lo + LANES]
        q_ref[:, h * DKP:h * DKP + LANES] = (nope * scale).astype(_BF)
        q_ref[:, h * DKP + LANES:(h + 1) * DKP] = ((a * cos + b * sin) * scale).astype(_BF)
    hkv = _rms(ckv_ref[...].astype(_F32), gkv_ref[...]).astype(_BF)
    kv = _dot(hkv, wkv_ref[...])
    k_rope = (kr_ref[...].astype(_F32) * cos + krr_ref[...].astype(_F32) * sin).astype(_BF)
    for h in range(MLA_HEADS):
        lo = h * (MLA_NOPE + MLA_V)
        k_ref[:, h * DKP:h * DKP + LANES] = kv[:, lo:lo + MLA_NOPE].astype(_BF)
        k_ref[:, h * DKP + LANES:(h + 1) * DKP] = k_rope
        v_ref[:, h * MLA_V:(h + 1) * MLA_V] = kv[:, lo + MLA_NOPE:lo + MLA_NOPE + MLA_V].astype(_BF)


def _mla_prep(proj, gq, gkv, wq, wkv, cos, sin, *, tm=512):
    s = proj.shape[0]
    const = lambda i: (0, 0)
    return pl.pallas_call(
        _mla_prep_kernel,
        out_shape=(
            jax.ShapeDtypeStruct((s, MLA_HEADS * DKP), _BF),
            jax.ShapeDtypeStruct((s, MLA_HEADS * DKP), _BF),
            jax.ShapeDtypeStruct((s, MLA_HEADS * MLA_V), _BF),
        ),
        grid=(s // tm,),
        in_specs=[
            pl.BlockSpec((tm, MLA_Q_RANK), lambda i: (i, COL_CQ // MLA_Q_RANK)),
            pl.BlockSpec((tm, MLA_KV_RANK), lambda i: (i, COL_CKV // MLA_KV_RANK)),
            pl.BlockSpec((tm, LANES), lambda i: (i, COL_KR // LANES)),
            pl.BlockSpec((tm, LANES), lambda i: (i, COL_KR_ROT // LANES)),
            pl.BlockSpec((1, MLA_Q_RANK), const),
            pl.BlockSpec((1, MLA_KV_RANK), const),
            pl.BlockSpec(wq.shape, const),
            pl.BlockSpec(wkv.shape, const),
            pl.BlockSpec((tm, LANES), lambda i: (i, 0)),
            pl.BlockSpec((tm, LANES), lambda i: (i, 0)),
        ],
        out_specs=(
            pl.BlockSpec((tm, MLA_HEADS * DKP), lambda i: (i, 0)),
            pl.BlockSpec((tm, MLA_HEADS * DKP), lambda i: (i, 0)),
            pl.BlockSpec((tm, MLA_HEADS * MLA_V), lambda i: (i, 0)),
        ),
        compiler_params=_params(("parallel",)),
        name="mla_prep",
    )(proj, proj, proj, proj, gq, gkv, wq, wkv, cos, sin)


def _dot_tn(a, b):
    return lax.dot_general(a, b, (((0,), (0,)), ((), ())), preferred_element_type=_F32)


def _online_update_t(st, vs, m, l, acc):
    t = vs[0].shape[0]
    m_new = jnp.maximum(m, jnp.max(st, axis=0, keepdims=True))
    alpha = jnp.exp2(m - m_new)
    pt = jnp.exp2(st - m_new)
    l = alpha * l + jnp.sum(pt, axis=0, keepdims=True)
    pb = pt.astype(_BF)
    pv = functools.reduce(jnp.add, [_dot_tn(v, pb[a * t:(a + 1) * t, :]) for a, v in enumerate(vs)])
    return m_new, l, alpha * acc + pv


def _mla_attn_kernel(q_ref, k_ref, v_ref, o_ref, m_ref, l_ref, acc_ref, *, n_sub):
    t = q_ref.shape[0]
    tk = n_sub * t
    hp = q_ref.shape[1] // DKP
    i = pl.program_id(1)
    qs = [q_ref[:, h * DKP:(h + 1) * DKP] for h in range(hp)]
    n_full = i // n_sub
    tail_b0 = jnp.maximum(i + 1 - n_sub, 0)

    m_ref[...] = jnp.full(m_ref.shape, -1e30, _F32)
    l_ref[...] = jnp.zeros(l_ref.shape, _F32)
    acc_ref[...] = jnp.zeros(acc_ref.shape, _F32)

    def first_tile(c):
        return jnp.where(c < n_full, c * n_sub, tail_b0)

    def logits(c, h):
        r0 = pl.multiple_of(first_tile(c) * t, t)
        return _dot_nt(k_ref[pl.ds(r0, tk), h * DKP:(h + 1) * DKP], qs[h])

    def update(c, h, st):
        b0 = first_tile(c)
        vs = [v_ref[pl.ds(pl.multiple_of((b0 + a) * t, t), t), h * MLA_V:(h + 1) * MLA_V] for a in range(n_sub)]
        m, l, acc = _online_update_t(st, vs, m_ref[h], l_ref[h], acc_ref[h])
        m_ref[h] = m
        l_ref[h] = l
        acc_ref[h] = acc

    def body(c, st):
        for h in range(hp):
            st_next = logits(c, h + 1) if h + 1 < hp else logits(c + 1, 0)
            update(c, h, st)
            st = st_next
        return st

    st = lax.fori_loop(0, n_full, body, logits(0, 0))
    kpos = tail_b0 * t + lax.broadcasted_iota(jnp.int32, (tk, t), 0)
    qpos = i * t + lax.broadcasted_iota(jnp.int32, (tk, t), 1)
    tail_mask = (kpos >= n_full * tk) & (kpos <= qpos)
    for h in range(hp):
        st_next = logits(n_full, h + 1) if h + 1 < hp else None
        update(n_full, h, jnp.where(tail_mask, st, _NEG_INF))
        st = st_next
    for h in range(hp):
        o_ref[:, h * MLA_V:(h + 1) * MLA_V] = (acc_ref[h] * (1.0 / l_ref[h])).T.astype(o_ref.dtype)


def _mla_attn(q, k, v, *, n_sub=4, hp=4):
    s = q.shape[0]
    t = ATT_TILE
    nb = s // t
    return pl.pallas_call(
        functools.partial(_mla_attn_kernel, n_sub=n_sub),
        out_shape=jax.ShapeDtypeStruct((s, MLA_HEADS * MLA_V), _BF),
        grid=(MLA_HEADS // hp, nb),
        in_specs=[
            pl.BlockSpec((t, hp * DKP), lambda h, i: (i, h)),
            pl.BlockSpec((s, hp * DKP), lambda h, i: (0, h), pipeline_mode=pl.Buffered(1)),
            pl.BlockSpec((s, hp * MLA_V), lambda h, i: (0, h), pipeline_mode=pl.Buffered(1)),
        ],
        out_specs=pl.BlockSpec((t, hp * MLA_V), lambda h, i: (i, h)),
        scratch_shapes=[
            pltpu.VMEM((hp, 1, t), _F32),
            pltpu.VMEM((hp, 1, t), _F32),
            pltpu.VMEM((hp, MLA_V, t), _F32),
        ],
        compiler_params=_params(("parallel", "arbitrary")),
        name="mla_attn",
    )(q, k, v)


def _t5_bucket_np(dist):
    n = np.maximum(dist, 0)
    max_exact = REL_BUCKETS // 2
    nf = np.maximum(n, 1).astype(np.float32)
    large = max_exact + (np.log(nf / np.float32(max_exact)) / np.float32(math.log(REL_MAX_DIST / max_exact))
                         * np.float32(REL_BUCKETS - max_exact)).astype(np.int32)
    large = np.minimum(large, REL_BUCKETS - 1)
    return np.where(n < max_exact, n, large).astype(np.int32)


def _bias_table_kernel(rb_ref, bk_ref, o_ref):
    h = pl.program_id(0)
    bk = bk_ref[0]
    base = rb_ref[REL_BUCKETS - 1, h]
    acc = jnp.zeros(bk.shape, _F32)
    for b in range(REL_BUCKETS - 1):
        acc = jnp.where(bk == b, (rb_ref[b, h] - base) * LOG2_E, acc)
    o_ref[0, 0] = acc


def _bias_tables(rel_bias, buckets):
    nt, r, c = buckets.shape
    return pl.pallas_call(
        _bias_table_kernel,
        out_shape=jax.ShapeDtypeStruct((NSA_HEADS, nt, r, c), _F32),
        grid=(NSA_HEADS, nt),
        in_specs=[
            pl.BlockSpec(memory_space=pltpu.SMEM),
            pl.BlockSpec((1, r, c), lambda h, t: (t, 0, 0)),
        ],
        out_specs=pl.BlockSpec((1, 1, r, c), lambda h, t: (h, t, 0, 0)),
        compiler_params=_params(("arbitrary", "arbitrary")),
        name="bias_tables",
    )(rel_bias, jnp.asarray(buckets))


def _window_buckets(t):
    r = np.arange(3 * t)[:, None]
    c = np.arange(t)[None, :]
    return _t5_bucket_np(c - r + 2 * t)[None]


def _cmp_buckets(t):
    c = np.arange(CMP_PAD)[:, None]
    r = np.arange(t)[None, :]
    return _t5_bucket_np(r - (CMP_LEN - 1) - t + CMP_STRIDE * (CMP_PAD - c))[None]


def _compress_kernel(ch_ref, pe_ref, w1_ref, w2_ref, o_ref):
    half = w1_ref.shape[0] // 2
    ch = ch_ref[0]
    a = _dot(ch, w1_ref[:half, :])
    b = _dot(ch, w1_ref[half:, :])
    n = ch.shape[0]
    b_next = pltpu.roll(b, n - 1, axis=0)
    c0 = _dot(pe_ref[...], w1_ref[...])[0:1, :]
    hid = a + b_next + c0
    hid = hid * jax.nn.sigmoid(hid)
    o_ref[0] = _dot(hid.astype(_BF), w2_ref[...]).astype(o_ref.dtype)


def _compress(chunks, pe_flat, w1, w2):
    g, n, kd = chunks.shape
    d = w2.shape[1]
    return pl.pallas_call(
        _compress_kernel,
        out_shape=jax.ShapeDtypeStruct((g, n, d), _BF),
        grid=(g,),
        in_specs=[
            pl.BlockSpec((1, n, kd), lambda i: (i, 0, 0)),
            pl.BlockSpec(pe_flat.shape, lambda i: (0, 0)),
            pl.BlockSpec(w1.shape, lambda i: (0, 0)),
            pl.BlockSpec(w2.shape, lambda i: (0, 0)),
        ],
        out_specs=pl.BlockSpec((1, n, d), lambda i: (i, 0, 0)),
        compiler_params=_params(("arbitrary",)),
        name="nsa_compress",
    )(chunks, pe_flat, w1, w2)


def _nsa_cmp_kernel(q_ref, kc_ref, vc_ref, afar_ref, tc_ref, oc_ref, sel_ref):
    t = q_ref.shape[0]
    nc = kc_ref.shape[1] - CMP_PAD
    per_tile = t // CMP_STRIDE
    i = pl.program_id(1)
    thr = (i + 1) * per_tile - CMP_PAD
    near0 = pl.multiple_of((i + 1) * per_tile, per_tile)

    k_ext = jnp.concatenate([kc_ref[0, CMP_PAD:, :], kc_ref[0, pl.ds(near0, CMP_PAD), :]], axis=0)
    vcn_t = vc_ref[0, pl.ds(near0, CMP_PAD), :].astype(_F32).T.astype(_BF)
    jn = lax.broadcasted_iota(jnp.int32, (N_SLC_PAD, CMP_PAD), 0) * SLC_LEN
    cn = (thr + lax.broadcasted_iota(jnp.int32, (N_SLC_PAD, CMP_PAD), 1)) * CMP_STRIDE
    ovn_t = jnp.where((cn < jn + SLC_LEN) & (cn + CMP_LEN > jn), 1.0, 0.0).astype(_BF)
    a_near = jnp.concatenate([vcn_t, ovn_t], axis=0)
    a_far = afar_ref[0]

    far_mask = lax.broadcasted_iota(jnp.int32, (nc, t), 0) < thr
    c = lax.broadcasted_iota(jnp.int32, (CMP_PAD, t), 0)
    r = lax.broadcasted_iota(jnp.int32, (CMP_PAD, t), 1)
    near_mask = (CMP_STRIDE * c <= r - (CMP_LEN - 1) - t + CMP_STRIDE * CMP_PAD) & (c >= -thr)

    imp = jnp.zeros((N_SLC_PAD, t), _F32)
    for h in range(NSA_HPG):
        st = _dot_nt(k_ext, q_ref[:, h * DKP:(h + 1) * DKP])
        sf = jnp.where(far_mask, st[:nc], _NEG_INF)
        sn = jnp.where(near_mask, st[nc:] + tc_ref[h, 0], _NEG_INF)
        m = jnp.maximum(jnp.max(sf, axis=0, keepdims=True), jnp.max(sn, axis=0, keepdims=True))
        m = jnp.where(m == _NEG_INF, 0.0, m)
        pf = jnp.exp2(sf - m)
        pn = jnp.exp2(sn - m)
        den = jnp.sum(pf, axis=0, keepdims=True) + jnp.sum(pn, axis=0, keepdims=True)
        inv = 1.0 / jnp.maximum(den, 1e-30)
        pf = pf * inv
        pn = pn * inv
        pf_hi = pf.astype(_BF)
        pn_hi = pn.astype(_BF)
        res = _dot(a_far, pf_hi) + _dot(a_near, pn_hi)
        oc_ref[h * NSA_DV:(h + 1) * NSA_DV, :] = res[:NSA_DV].astype(oc_ref.dtype)
        pf_lo = (pf - pf_hi.astype(_F32)).astype(_BF)
        pn_lo = (pn - pn_hi.astype(_F32)).astype(_BF)
        imp = imp + res[NSA_DV:] + (_dot(a_far[NSA_DV:], pf_lo) + _dot(a_near[NSA_DV:], pn_lo))

    jj = lax.broadcasted_iota(jnp.int32, (N_SLC_PAD, t), 0)
    tq = i * t + lax.broadcasted_iota(jnp.int32, (N_SLC_PAD, t), 1)
    cur = tq >> SLC_SHIFT
    valid = jj * SLC_LEN <= tq
    forced = (jj == 0) | ((jj <= cur) & (jj >= cur - 1))
    work = jnp.where(valid, jnp.where(forced, jnp.inf, imp), _NEG_INF)
    blk = jj.astype(_F32)
    sel = jnp.zeros((N_SLC_PAD, t), _F32)
    for _ in range(SLC_TOPN):
        mx = jnp.max(work, axis=0, keepdims=True)
        first = jnp.min(jnp.where(work == mx, blk, float(N_SLC_PAD)), axis=0, keepdims=True)
        hit = blk == first
        sel = jnp.where(hit & (mx > _NEG_INF), 1.0, sel)
        work = jnp.where(hit, _NEG_INF, work)
    sel_ref[0] = sel


def _nsa_cmp(proj, kc_pad, vc_pad, a_far, tc):
    s = proj.shape[0]
    t = ATT_TILE
    gw = NSA_HPG * DKP
    return pl.pallas_call(
        _nsa_cmp_kernel,
        out_shape=(
            jax.ShapeDtypeStruct((NSA_HEADS * NSA_DV, s), _BF),
            jax.ShapeDtypeStruct((NSA_GROUPS, N_SLC_PAD, s), _F32),
        ),
        grid=(NSA_GROUPS, s // t),
        in_specs=[
            pl.BlockSpec((t, gw), lambda g, i: (i, COL_QN // gw + g)),
            pl.BlockSpec((1,) + kc_pad.shape[1:], lambda g, i: (g, 0, 0)),
            pl.BlockSpec((1,) + vc_pad.shape[1:], lambda g, i: (g, 0, 0)),
            pl.BlockSpec((1,) + a_far.shape[1:], lambda g, i: (g, 0, 0)),
            pl.BlockSpec((NSA_HPG, 1, CMP_PAD, t), lambda g, i: (g, 0, 0, 0)),
        ],
        out_specs=(
            pl.BlockSpec((NSA_HPG * NSA_DV, t), lambda g, i: (g, i)),
            pl.BlockSpec((1, N_SLC_PAD, t), lambda g, i: (g, 0, i)),
        ),
        compiler_params=_params(("parallel", "arbitrary")),
        name="nsa_cmp",
    )(proj, kc_pad, vc_pad, a_far, tc)


def _nsa_main_kernel(q_ref, ks_ref, vs_ref, kw_ref, vw_ref, selt_ref, tw_ref, oc_ref, gate_ref,
                     o_ref, m_ref, l_ref, acc_ref, *, n_sub):
    t = q_ref.shape[0]

    def value_tile(ref, tile):
        return ref[pl.ds(pl.multiple_of(tile * t, t), t), :]

    tk = n_sub * t
    g = pl.program_id(0)
    i = pl.program_id(1)
    blocks_per_tile = t // SLC_LEN
    qs = [q_ref[:, h * DKP:(h + 1) * DKP] for h in range(NSA_HPG)]

    m_ref[...] = jnp.full(m_ref.shape, -1e30, _F32)
    l_ref[...] = jnp.zeros(l_ref.shape, _F32)
    acc_ref[...] = jnp.zeros(acc_ref.shape, _F32)

    def selected(tile0, n_tiles):
        rows = [jnp.broadcast_to(selt_ref[0, pl.ds(tile0 * blocks_per_tile + b, 1), :], (SLC_LEN, t))
                for b in range(n_tiles * blocks_per_tile)]
        return jnp.concatenate(rows, axis=0) > 0.5

    def run_heads(k, vts, mask, bias):
        def logits(h):
            st = _dot_nt(k, qs[h])
            if bias is not None:
                st = st + bias(h)
            return jnp.where(mask, st, _NEG_INF)

        st = logits(0)
        for h in range(NSA_HPG):
            st_next = logits(h + 1) if h + 1 < NSA_HPG else None
            m, l, acc = _online_update_t(st, vts, m_ref[h], l_ref[h], acc_ref[h])
            m_ref[h] = m
            l_ref[h] = l
            acc_ref[h] = acc
            st = st_next

    def far_chunk(tile0, mask_extra):
        r0 = pl.multiple_of(tile0 * t, t)
        mask = selected(tile0, n_sub)
        if mask_extra is not None:
            mask = mask & mask_extra(r0)
        run_heads(ks_ref[pl.ds(r0, tk), :], [value_tile(vs_ref, tile0 + a) for a in range(n_sub)], mask, None)

    n_far = jnp.maximum(i - 1, 0)
    n_full = n_far // n_sub

    def far_body(c, carry):
        far_chunk(c * n_sub, None)
        return carry

    lax.fori_loop(0, n_full, far_body, 0)

    @pl.when(n_far > n_full * n_sub)
    def _():
        def not_yet_covered(r0):
            kpos = r0 + lax.broadcasted_iota(jnp.int32, (tk, t), 0)
            return (kpos >= n_full * tk) & (kpos < n_far * t)

        far_chunk(jnp.maximum(n_far - n_sub, 0), not_yet_covered)

    prev = jnp.maximum(i - 1, 0)
    row2 = lax.broadcasted_iota(jnp.int32, (2 * t, t), 0)
    col2 = lax.broadcasted_iota(jnp.int32, (2 * t, t), 1)
    near_mask = (jnp.concatenate([selected(prev, 1), selected(i, 1)], axis=0)
                 & (col2 - row2 + t >= 0) & (row2 >= jnp.where(i >= 1, 0, t)))
    k_near = jnp.concatenate([ks_ref[pl.ds(pl.multiple_of(prev * t, t), t), :],
                              ks_ref[pl.ds(pl.multiple_of(i * t, t), t), :]], axis=0)
    run_heads(k_near, [value_tile(vs_ref, prev), value_tile(vs_ref, i)], near_mask,
              lambda h: tw_ref[h, 0, t:, :])

    prev2 = jnp.maximum(i - 2, 0)
    w_tiles = (prev2, prev, i)
    k_win = jnp.concatenate([kw_ref[pl.ds(pl.multiple_of(a * t, t), t), :] for a in w_tiles], axis=0)
    row3 = lax.broadcasted_iota(jnp.int32, (3 * t, t), 0)
    col3 = lax.broadcasted_iota(jnp.int32, (3 * t, t), 1)
    dist = col3 - row3 + 2 * t
    first_row = jnp.where(i >= 2, 0, jnp.where(i >= 1, t, 2 * t))
    win_mask = (dist >= 0) & (dist < WINDOW) & (row3 >= first_row)

    gates_t = jax.nn.sigmoid(gate_ref[...].astype(_F32)).T
    for h in range(NSA_HPG):
        st = jnp.where(win_mask, _dot_nt(k_win, qs[h]) + tw_ref[h, 0], _NEG_INF)
        m = jnp.max(st, axis=0, keepdims=True)
        pt = jnp.exp2(st - m)
        den = jnp.sum(pt, axis=0, keepdims=True)
        pb = pt.astype(_BF)
        o_w = functools.reduce(jnp.add, [_dot_tn(value_tile(vw_ref, a), pb[n * t:(n + 1) * t, :])
                                         for n, a in enumerate(w_tiles)]) * (1.0 / den)
        o_s = acc_ref[h] * (1.0 / l_ref[h])
        o_c = oc_ref[h * NSA_DV:(h + 1) * NSA_DV, :].astype(_F32)
        gc, gs, gw = [jnp.where(g == 0, gates_t[h * N_BRANCH + b:h * N_BRANCH + b + 1, :],
                                gates_t[(NSA_HPG + h) * N_BRANCH + b:(NSA_HPG + h) * N_BRANCH + b + 1, :])
                      for b in range(N_BRANCH)]
        o_ref[:, h * NSA_DV:(h + 1) * NSA_DV] = (gc * o_c + gs * o_s + gw * o_w).T.astype(o_ref.dtype)


def _nsa_main(proj, sel_t, tw, oc_t, *, n_sub=4):
    s = proj.shape[0]
    t = ATT_TILE
    nb = s // t
    gw = NSA_HPG * DKP
    kn0 = COL_KN // DKP
    vn0 = COL_VN // NSA_DV
    return pl.pallas_call(
        functools.partial(_nsa_main_kernel, n_sub=n_sub),
        out_shape=jax.ShapeDtypeStruct((s, NSA_HEADS * NSA_DV), _BF),
        grid=(NSA_GROUPS, nb),
        in_specs=[
            pl.BlockSpec((t, gw), lambda g, i: (i, COL_QN // gw + g)),
            pl.BlockSpec((s, DKP), lambda g, i: (0, kn0 + 1 * NSA_GROUPS + g)),
            pl.BlockSpec((s, NSA_DV), lambda g, i: (0, vn0 + 1 * NSA_GROUPS + g)),
            pl.BlockSpec((s, DKP), lambda g, i: (0, kn0 + 2 * NSA_GROUPS + g)),
            pl.BlockSpec((s, NSA_DV), lambda g, i: (0, vn0 + 2 * NSA_GROUPS + g)),
            pl.BlockSpec((1, N_SLC_PAD, t), lambda g, i: (g, 0, i)),
            pl.BlockSpec((NSA_HPG, 1, 3 * t, t), lambda g, i: (g, 0, 0, 0)),
            pl.BlockSpec((NSA_HPG * NSA_DV, t), lambda g, i: (g, i)),
            pl.BlockSpec((t, LANES), lambda g, i: (i, COL_GATE // LANES)),
        ],
        out_specs=pl.BlockSpec((t, NSA_HPG * NSA_DV), lambda g, i: (i, g)),
        scratch_shapes=[
            pltpu.VMEM((NSA_HPG, 1, t), _F32),
            pltpu.VMEM((NSA_HPG, 1, t), _F32),
            pltpu.VMEM((NSA_HPG, NSA_DV, t), _F32),
        ],
        compiler_params=_params(("parallel", "arbitrary")),
        name="nsa_main",
    )(proj, proj, proj, proj, proj, sel_t, tw, oc_t, proj)


def _outproj_kernel(x_ref, mod_ref, om_ref, on_ref, w_ref, o_ref, *, mod_row):
    half = om_ref.shape[1]
    y = _dot(om_ref[...], w_ref[:half, :]) + _dot(on_ref[...], w_ref[half:, :])
    o_ref[...] = x_ref[...] + mod_ref[mod_row:mod_row + 1, :] * y


def _outproj(x, mod, o_mla, o_nsa, w_out, *, mod_row, tm=512):
    s = x.shape[0]
    return pl.pallas_call(
        functools.partial(_outproj_kernel, mod_row=mod_row),
        out_shape=jax.ShapeDtypeStruct((s, D_MODEL), _F32),
        grid=(s // tm,),
        in_specs=[
            pl.BlockSpec((tm, D_MODEL), lambda i: (i, 0)),
            pl.BlockSpec((9, D_MODEL), lambda i: (0, 0)),
            pl.BlockSpec((tm, o_mla.shape[1]), lambda i: (i, 0)),
            pl.BlockSpec((tm, o_nsa.shape[1]), lambda i: (i, 0)),
            pl.BlockSpec(w_out.shape, lambda i: (0, 0)),
        ],
        out_specs=pl.BlockSpec((tm, D_MODEL), lambda i: (i, 0)),
        compiler_params=_params(("parallel",)),
        name="outproj",
    )(x, mod, o_mla, o_nsa, w_out)


def _pad_cols(w, width):
    return jnp.pad(w, ((0, 0), (0, width - w.shape[1])))


def _pad_last(w, width):
    return jnp.pad(w, [(0, 0)] * (w.ndim - 1) + [(0, width - w.shape[-1])])


def _rotate_half_cols(w):
    half = MLA_ROPE // 2
    return jnp.concatenate([-w[..., half:], w[..., :half]], axis=-1)


def _w_in_padded(w_in):
    d = w_in.shape[0]
    w_in = w_in.astype(_BF)
    sizes = (MLA_Q_RANK, MLA_KV_RANK, MLA_ROPE, NSA_HEADS * NSA_DK,
             N_BRANCH * NSA_GROUPS * NSA_DK, N_BRANCH * NSA_GROUPS * NSA_DV, NSA_HEADS * N_BRANCH)
    offs = np.concatenate([[0], np.cumsum(sizes)])
    cq, ckv, kr, qn, kn, vn, gn = [w_in[:, offs[a]:offs[a + 1]] for a in range(len(sizes))]
    qn = _pad_last(qn.reshape(d, NSA_HEADS, NSA_DK), DKP).reshape(d, NSA_HEADS * DKP)
    kn = _pad_last(kn.reshape(d, N_BRANCH * NSA_GROUPS, NSA_DK), DKP).reshape(d, -1)
    cols = [cq, _pad_cols(gn, LANES), _pad_cols(kr, LANES), ckv,
            _pad_cols(_rotate_half_cols(kr), LANES),
            jnp.zeros((d, COL_QN - COL_KR_ROT - LANES), w_in.dtype), qn, kn, vn]
    w = jnp.concatenate(cols, axis=1)
    assert w.shape[1] == D_IN_PAD
    colscale = np.ones((1, D_IN_PAD), np.float32)
    colscale[:, COL_QN:COL_KN] = NSA_DK ** -0.5 * LOG2_E
    return w, jnp.asarray(colscale)


def _w_uq_padded(w_uq):
    r = w_uq.shape[0]
    w = w_uq.reshape(r, MLA_HEADS, MLA_NOPE + MLA_ROPE)
    nope = w[..., :MLA_NOPE].reshape(r, -1)
    rope = w[..., MLA_NOPE:]
    rope_p = _pad_last(rope, LANES).reshape(r, -1)
    rot_p = _pad_last(_rotate_half_cols(rope), LANES).reshape(r, -1)
    return jnp.concatenate([nope, rope_p, rot_p], axis=1).astype(_BF)


def _cmp_w1_padded(w1, d, dpad):
    return _pad_last(w1.reshape(CMP_LEN, d, CMP_HIDDEN).transpose(0, 2, 1), dpad) \
        .transpose(0, 2, 1).reshape(CMP_LEN * dpad, CMP_HIDDEN).astype(_BF)


def _rope_tables(s):
    inv = ROPE_THETA ** (-jnp.arange(0, MLA_ROPE, 2, dtype=_F32) / MLA_ROPE)
    ang = jnp.arange(s, dtype=_F32)[:, None] * inv[None, :]
    cos = _pad_cols(jnp.tile(jnp.cos(ang), (1, 2)), LANES)
    sin = _pad_cols(jnp.tile(jnp.sin(ang), (1, 2)), LANES)
    return cos, sin


def _overlap_t(s):
    s_start = np.arange(N_SLC_PAD)[:, None] * SLC_LEN
    c_start = np.arange(s // CMP_STRIDE)[None, :] * CMP_STRIDE
    ov = (c_start < s_start + SLC_LEN) & (c_start + CMP_LEN > s_start)
    return jnp.asarray(ov.astype(np.float32), _BF)


def _mixer_heads(x, mod, norm_mix, w_in, mla_q_norm, w_uq, mla_kv_norm, w_ukv, pe_cmp_k, w_cmp_k1,
                 w_cmp_k2, pe_cmp_v, w_cmp_v1, w_cmp_v2, rel_bias):
    s = x.shape[0]
    assert s % ATT_TILE == 0 and s // SLC_LEN <= N_SLC_PAD and s >= 3 * ATT_TILE
    w_pad, colscale = _w_in_padded(w_in)
    proj = _inproj(x, mod, norm_mix.reshape(1, -1), w_pad, colscale, mod_row=3)

    cos, sin = _rope_tables(s)
    q, k, v = _mla_prep(proj, mla_q_norm.reshape(1, -1), mla_kv_norm.reshape(1, -1),
                        _w_uq_padded(w_uq), w_ukv.astype(_BF), cos, sin)
    o_mla = _mla_attn(q, k, v)

    n_chunks = s // CMP_STRIDE
    kn0 = COL_KN
    k_cmp = jnp.stack([proj[:, kn0 + g * DKP:kn0 + (g + 1) * DKP] for g in range(NSA_GROUPS)])
    v_cmp = jnp.stack([proj[:, COL_VN + g * NSA_DV:COL_VN + (g + 1) * NSA_DV] for g in range(NSA_GROUPS)])
    pe_k = jnp.broadcast_to(_pad_last(pe_cmp_k, DKP).reshape(1, -1), (8, CMP_LEN * DKP)).astype(_BF)
    pe_v = jnp.broadcast_to(pe_cmp_v.reshape(1, -1), (8, CMP_LEN * NSA_DV)).astype(_BF)
    kc = _compress(k_cmp.reshape(NSA_GROUPS, n_chunks, CMP_STRIDE * DKP), pe_k,
                   _cmp_w1_padded(w_cmp_k1, NSA_DK, DKP), _pad_cols(w_cmp_k2, DKP).astype(_BF))
    vc = _compress(v_cmp.reshape(NSA_GROUPS, n_chunks, CMP_STRIDE * NSA_DV), pe_v,
                   w_cmp_v1.astype(_BF), w_cmp_v2.astype(_BF))
    front = ((0, 0), (CMP_PAD, 0), (0, 0))
    tc = _bias_tables(rel_bias, _cmp_buckets(ATT_TILE))
    tw = _bias_tables(rel_bias, _window_buckets(ATT_TILE))
    ov_t = _overlap_t(s)
    a_far = jnp.stack([jnp.concatenate([vc[g].T, ov_t], axis=0) for g in range(NSA_GROUPS)])
    oc_t, sel_t = _nsa_cmp(proj, jnp.pad(kc, front), jnp.pad(vc, front), a_far, tc)
    o_nsa = _nsa_main(proj, sel_t, tw, oc_t)
    return o_mla, o_nsa


def kernel(x, c, w_ada, b_ada, norm_ffn1, w1_gate, w1_up, w1_down, norm_mix, w_in, mla_q_norm, w_uq, mla_kv_norm, w_ukv, pe_cmp_k, w_cmp_k1, w_cmp_k2, pe_cmp_v, w_cmp_v1, w_cmp_v2, rel_bias, w_out, norm_ffn2, w2_gate, w2_up, w2_down, norm_final):
    assert x.shape[0] == 1 and w_ada.shape[0] == 1
    xs = x[0]
    fin = norm_final.reshape(1, -1)
    mod = _ada_mod(c, w_ada[0], b_ada[0]).reshape(9, D_MODEL)
    xs = _ffn(xs, mod, norm_ffn1[0].reshape(1, -1), w1_gate[0], w1_up[0], w1_down[0], fin,
              mod_row=0, final_norm=False)
    o_mla, o_nsa = _mixer_heads(xs, mod, norm_mix[0], w_in[0], mla_q_norm[0], w_uq[0], mla_kv_norm[0],
                                w_ukv[0], pe_cmp_k[0], w_cmp_k1[0], w_cmp_k2[0], pe_cmp_v[0],
                                w_cmp_v1[0], w_cmp_v2[0], rel_bias)
    xs = _outproj(xs, mod, o_mla, o_nsa, w_out[0].astype(_BF), mod_row=5)
    xs = _ffn(xs, mod, norm_ffn2[0].reshape(1, -1), w2_gate[0], w2_up[0], w2_down[0], fin,
              mod_row=6, final_norm=True)
    return xs[None]
```

```python
import functools
import math

import numpy as np
import jax
import jax.numpy as jnp
from jax import lax
from jax.experimental import pallas as pl
from jax.experimental.pallas import tpu as pltpu

D_MODEL = 2048
D_FF = 5632
EPS = 1e-6
MLA_HEADS = 8
MLA_Q_RANK = 768
MLA_KV_RANK = 512
MLA_NOPE = 128
MLA_ROPE = 64
MLA_V = 128
ROPE_THETA = 10000.0
NSA_HEADS = 8
NSA_GROUPS = 2
NSA_HPG = NSA_HEADS // NSA_GROUPS
NSA_DK = 192
NSA_DV = 128
CMP_LEN = 32
CMP_STRIDE = 16
CMP_HIDDEN = 256
SLC_LEN = 64
SLC_SHIFT = 6
SLC_TOPN = 16
WINDOW = 512
N_BRANCH = 3
REL_BUCKETS = 32
REL_MAX_DIST = 128

LANES = 128
VMEM_LIMIT = 60000 * 1024

DKP = 256
ATT_TILE = 256
CMP_PAD = 128
N_SLC_PAD = 128

_BF = jnp.bfloat16
_F32 = jnp.float32
_NEG_INF = float("-inf")
LOG2_E = math.log2(math.e)

COL_CQ = 0
COL_GATE = 768
COL_KR = 896
COL_CKV = 1024
COL_KR_ROT = 1536
COL_QN = 2048
COL_KN = COL_QN + NSA_HEADS * DKP
COL_VN = COL_KN + N_BRANCH * NSA_GROUPS * DKP
D_IN_PAD = COL_VN + N_BRANCH * NSA_GROUPS * NSA_DV


def _dot(a, b):
    return jnp.dot(a, b, preferred_element_type=_F32)


def _dot_nt(a, b):
    return lax.dot_general(a, b, (((1,), (1,)), ((), ())), preferred_element_type=_F32)


def _rms(x, gain):
    return x * lax.rsqrt(jnp.mean(x * x, axis=-1, keepdims=True) + EPS) * gain


def _params(semantics):
    return pltpu.CompilerParams(dimension_semantics=semantics, vmem_limit_bytes=VMEM_LIMIT)


ADA_ROWS = 256


def _ada_kernel(c_ref, w_ref, b_ref, o_ref):
    tn = o_ref.shape[1]

    def body(k, acc):
        r = pl.multiple_of(k * ADA_ROWS, ADA_ROWS)
        c = c_ref[pl.ds(r, ADA_ROWS), :]
        s = c * jax.nn.sigmoid(c)
        prod = w_ref[pl.ds(r, ADA_ROWS), :] * s
        return acc + jnp.sum(prod.reshape(ADA_ROWS // 8, 8, tn), axis=0)

    acc = lax.fori_loop(0, D_MODEL // ADA_ROWS, body, jnp.zeros((8, tn), _F32))
    o_ref[...] = jnp.sum(acc, axis=0, keepdims=True) + b_ref[...]


def _ada_mod(c, w_ada, b_ada):
    n = w_ada.shape[1]
    tn = 1024
    return pl.pallas_call(
        _ada_kernel,
        out_shape=jax.ShapeDtypeStruct((1, n), _F32),
        grid=(n // tn,),
        in_specs=[
            pl.BlockSpec((D_MODEL, 1), lambda j: (0, 0)),
            pl.BlockSpec((D_MODEL, tn), lambda j: (0, j)),
            pl.BlockSpec((1, tn), lambda j: (0, j)),
        ],
        out_specs=pl.BlockSpec((1, tn), lambda j: (0, j)),
        compiler_params=_params(("arbitrary",)),
        name="ada_mod",
    )(c.reshape(D_MODEL, 1), w_ada, b_ada.reshape(1, n))


def _ffn_kernel(x_ref, mod_ref, gain_ref, wg_ref, wu_ref, wd_ref, fin_ref, o_ref, h_ref,
                *, mod_row, final_norm):
    j = pl.program_id(1)

    @pl.when(j == 0)
    def _():
        y = _rms(x_ref[...], gain_ref[...])
        h = y * (1.0 + mod_ref[mod_row + 1:mod_row + 2, :]) + mod_ref[mod_row:mod_row + 1, :]
        h_ref[...] = h.astype(_BF)
        o_ref[...] = jnp.zeros_like(o_ref)

    h = h_ref[...]
    g = _dot(h, wg_ref[...].astype(_BF))
    u = _dot(h, wu_ref[...].astype(_BF))
    a = (g * jax.nn.sigmoid(g)) * u
    o_ref[...] += _dot(a.astype(_BF), wd_ref[...].astype(_BF))

    @pl.when(j == pl.num_programs(1) - 1)
    def _():
        x2 = x_ref[...] + (0.5 * mod_ref[mod_row + 2:mod_row + 3, :]) * o_ref[...]
        if final_norm:
            x2 = _rms(x2, fin_ref[...])
        o_ref[...] = x2


def _ffn(x, mod, gain, wg, wu, wd, fin, *, mod_row, final_norm, tm=1024, tf=256):
    s = x.shape[0]
    tm = min(tm, s)
    kern = functools.partial(_ffn_kernel, mod_row=mod_row, final_norm=final_norm)
    return pl.pallas_call(
        kern,
        out_shape=jax.ShapeDtypeStruct((s, D_MODEL), _F32),
        grid=(s // tm, D_FF // tf),
        in_specs=[
            pl.BlockSpec((tm, D_MODEL), lambda i, j: (i, 0)),
            pl.BlockSpec((9, D_MODEL), lambda i, j: (0, 0)),
            pl.BlockSpec((1, D_MODEL), lambda i, j: (0, 0)),
            pl.BlockSpec((D_MODEL, tf), lambda i, j: (0, j)),
            pl.BlockSpec((D_MODEL, tf), lambda i, j: (0, j)),
            pl.BlockSpec((tf, D_MODEL), lambda i, j: (j, 0)),
            pl.BlockSpec((1, D_MODEL), lambda i, j: (0, 0)),
        ],
        out_specs=pl.BlockSpec((tm, D_MODEL), lambda i, j: (i, 0)),
        scratch_shapes=[pltpu.VMEM((tm, D_MODEL), _BF)],
        compiler_params=_params(("parallel", "arbitrary")),
        name="ffn_final" if final_norm else "ffn",
    )(x, mod, gain, wg, wu, wd, fin)


def _inproj_kernel(x_ref, mod_ref, gain_ref, w_ref, cs_ref, o_ref, h_ref, *, mod_row):
    @pl.when(pl.program_id(1) == 0)
    def _():
        y = _rms(x_ref[...], gain_ref[...])
        h = y * (1.0 + mod_ref[mod_row + 1:mod_row + 2, :]) + mod_ref[mod_row:mod_row + 1, :]
        h_ref[...] = h.astype(_BF)

    o_ref[...] = (_dot(h_ref[...], w_ref[...]) * cs_ref[...]).astype(o_ref.dtype)


def _inproj(x, mod, gain, w_pad, colscale, *, mod_row, tm=1024, tn=1280):
    s = x.shape[0]
    tm = min(tm, s)
    n = w_pad.shape[1]
    return pl.pallas_call(
        functools.partial(_inproj_kernel, mod_row=mod_row),
        out_shape=jax.ShapeDtypeStruct((s, n), _BF),
        grid=(s // tm, n // tn),
        in_specs=[
            pl.BlockSpec((tm, D_MODEL), lambda i, j: (i, 0)),
            pl.BlockSpec((9, D_MODEL), lambda i, j: (0, 0)),
            pl.BlockSpec((1, D_MODEL), lambda i, j: (0, 0)),
            pl.BlockSpec((D_MODEL, tn), lambda i, j: (0, j)),
            pl.BlockSpec((1, tn), lambda i, j: (0, j)),
        ],
        out_specs=pl.BlockSpec((tm, tn), lambda i, j: (i, j)),
        scratch_shapes=[pltpu.VMEM((tm, D_MODEL), _BF)],
        compiler_params=_params(("parallel", "arbitrary")),
        name="inproj",
    )(x, mod, gain, w_pad, colscale)


def _mla_prep_kernel(cq_ref, ckv_ref, kr_ref, krr_ref, gq_ref, gkv_ref, wq_ref, wkv_ref,
                     cos_ref, sin_ref, q_ref, k_ref, v_ref):
    scale = (MLA_NOPE + MLA_ROPE) ** -0.5 * LOG2_E
    cos = cos_ref[...]
    sin = sin_ref[...]
    hq = _rms(cq_ref[...].astype(_F32), gq_ref[...]).astype(_BF)
    qa = _dot(hq, wq_ref[...])
    nh = MLA_HEADS * LANES
    for h in range(MLA_HEADS):
        lo = h * LANES
        nope = qa[:, lo:lo + LANES]
        a = qa[:, nh + lo:nh + lo + LANES]
        b = qa[:, 2 * nh + lo:2 * nh + lo + LANES]
        q_ref[:, h * DKP:h * DKP + LANES] = (nope * scale).astype(_BF)
        q_ref[:, h * DKP + LANES:(h + 1) * DKP] = ((a * cos + b * sin) * scale).astype(_BF)
    hkv = _rms(ckv_ref[...].astype(_F32), gkv_ref[...]).astype(_BF)
    kv = _dot(hkv, wkv_ref[...])
    k_rope = (kr_ref[...].astype(_F32) * cos + krr_ref[...].astype(_F32) * sin).astype(_BF)
    for h in range(MLA_HEADS):
        lo = h * (MLA_NOPE + MLA_V)
        k_ref[:, h * DKP:h * DKP + LANES] = kv[:, lo:lo + MLA_NOPE].astype(_BF)
        k_ref[:, h * DKP + LANES:(h + 1) * DKP] = k_rope
        v_ref[:, h * MLA_V:(h + 1) * MLA_V] = kv[:, lo + MLA_NOPE:lo + MLA_NOPE + MLA_V].astype(_BF)


def _mla_prep(proj, gq, gkv, wq, wkv, cos, sin, *, tm=512):
    s = proj.shape[0]
    const = lambda i: (0, 0)
    return pl.pallas_call(
        _mla_prep_kernel,
        out_shape=(
            jax.ShapeDtypeStruct((s, MLA_HEADS * DKP), _BF),
            jax.ShapeDtypeStruct((s, MLA_HEADS * DKP), _BF),
            jax.ShapeDtypeStruct((s, MLA_HEADS * MLA_V), _BF),
        ),
        grid=(s // tm,),
        in_specs=[
            pl.BlockSpec((tm, MLA_Q_RANK), lambda i: (i, COL_CQ // MLA_Q_RANK)),
            pl.BlockSpec((tm, MLA_KV_RANK), lambda i: (i, COL_CKV // MLA_KV_RANK)),
            pl.BlockSpec((tm, LANES), lambda i: (i, COL_KR // LANES)),
            pl.BlockSpec((tm, LANES), lambda i: (i, COL_KR_ROT // LANES)),
            pl.BlockSpec((1, MLA_Q_RANK), const),
            pl.BlockSpec((1, MLA_KV_RANK), const),
            pl.BlockSpec(wq.shape, const),
            pl.BlockSpec(wkv.shape, const),
            pl.BlockSpec((tm, LANES), lambda i: (i, 0)),
            pl.BlockSpec((tm, LANES), lambda i: (i, 0)),
        ],
        out_specs=(
            pl.BlockSpec((tm, MLA_HEADS * DKP), lambda i: (i, 0)),
            pl.BlockSpec((tm, MLA_HEADS * DKP), lambda i: (i, 0)),
            pl.BlockSpec((tm, MLA_HEADS * MLA_V), lambda i: (i, 0)),
        ),
        compiler_params=_params(("parallel",)),
        name="mla_prep",
    )(proj, proj, proj, proj, gq, gkv, wq, wkv, cos, sin)


def _dot_tn(a, b):
    return lax.dot_general(a, b, (((0,), (0,)), ((), ())), preferred_element_type=_F32)


def _online_update_t(st, vs, m, l, acc):
    t = vs[0].shape[0]
    m_new = jnp.maximum(m, jnp.max(st, axis=0, keepdims=True))
    alpha = jnp.exp2(m - m_new)
    pt = jnp.exp2(st - m_new)
    l = alpha * l + jnp.sum(pt, axis=0, keepdims=True)
    pb = pt.astype(_BF)
    pv = functools.reduce(jnp.add, [_dot_tn(v, pb[a * t:(a + 1) * t, :]) for a, v in enumerate(vs)])
    return m_new, l, alpha * acc + pv


def _mla_attn_kernel(q_ref, k_ref, v_ref, o_ref, m_ref, l_ref, acc_ref, *, n_sub):
    t = q_ref.shape[0]
    tk = n_sub * t
    hp = q_ref.shape[1] // DKP
    i = pl.program_id(1)
    qs = [q_ref[:, h * DKP:(h + 1) * DKP] for h in range(hp)]
    n_full = i // n_sub
    tail_b0 = jnp.maximum(i + 1 - n_sub, 0)

    m_ref[...] = jnp.full(m_ref.shape, -1e30, _F32)
    l_ref[...] = jnp.zeros(l_ref.shape, _F32)
    acc_ref[...] = jnp.zeros(acc_ref.shape, _F32)

    def first_tile(c):
        return jnp.where(c < n_full, c * n_sub, tail_b0)

    def logits(c, h):
        r0 = pl.multiple_of(first_tile(c) * t, t)
        return _dot_nt(k_ref[pl.ds(r0, tk), h * DKP:(h + 1) * DKP], qs[h])

    def update(c, h, st):
        b0 = first_tile(c)
        vs = [v_ref[pl.ds(pl.multiple_of((b0 + a) * t, t), t), h * MLA_V:(h + 1) * MLA_V] for a in range(n_sub)]
        m, l, acc = _online_update_t(st, vs, m_ref[h], l_ref[h], acc_ref[h])
        m_ref[h] = m
        l_ref[h] = l
        acc_ref[h] = acc

    def body(c, st):
        for h in range(hp):
            st_next = logits(c, h + 1) if h + 1 < hp else logits(c + 1, 0)
            update(c, h, st)
            st = st_next
        return st

    st = lax.fori_loop(0, n_full, body, logits(0, 0))
    kpos = tail_b0 * t + lax.broadcasted_iota(jnp.int32, (tk, t), 0)
    qpos = i * t + lax.broadcasted_iota(jnp.int32, (tk, t), 1)
    tail_mask = (kpos >= n_full * tk) & (kpos <= qpos)
    for h in range(hp):
        st_next = logits(n_full, h + 1) if h + 1 < hp else None
        update(n_full, h, jnp.where(tail_mask, st, _NEG_INF))
        st = st_next
    for h in range(hp):
        o_ref[:, h * MLA_V:(h + 1) * MLA_V] = (acc_ref[h] * (1.0 / l_ref[h])).T.astype(o_ref.dtype)


def _mla_attn(q, k, v, *, n_sub=4, hp=4):
    s = q.shape[0]
    t = ATT_TILE
    nb = s // t
    return pl.pallas_call(
        functools.partial(_mla_attn_kernel, n_sub=n_sub),
        out_shape=jax.ShapeDtypeStruct((s, MLA_HEADS * MLA_V), _BF),
        grid=(MLA_HEADS // hp, nb),
        in_specs=[
            pl.BlockSpec((t, hp * DKP), lambda h, i: (i, h)),
            pl.BlockSpec((s, hp * DKP), lambda h, i: (0, h), pipeline_mode=pl.Buffered(1)),
            pl.BlockSpec((s, hp * MLA_V), lambda h, i: (0, h), pipeline_mode=pl.Buffered(1)),
        ],
        out_specs=pl.BlockSpec((t, hp * MLA_V), lambda h, i: (i, h)),
        scratch_shapes=[
            pltpu.VMEM((hp, 1, t), _F32),
            pltpu.VMEM((hp, 1, t), _F32),
            pltpu.VMEM((hp, MLA_V, t), _F32),
        ],
        compiler_params=_params(("parallel", "arbitrary")),
        name="mla_attn",
    )(q, k, v)


def _t5_bucket_np(dist):
    n = np.maximum(dist, 0)
    max_exact = REL_BUCKETS // 2
    nf = np.maximum(n, 1).astype(np.float32)
    large = max_exact + (np.log(nf / np.float32(max_exact)) / np.float32(math.log(REL_MAX_DIST / max_exact))
                         * np.float32(REL_BUCKETS - max_exact)).astype(np.int32)
    large = np.minimum(large, REL_BUCKETS - 1)
    return np.where(n < max_exact, n, large).astype(np.int32)


def _bias_table_kernel(rb_ref, bk_ref, o_ref):
    h = pl.program_id(0)
    bk = bk_ref[0]
    base = rb_ref[REL_BUCKETS - 1, h]
    acc = jnp.zeros(bk.shape, _F32)
    for b in range(REL_BUCKETS - 1):
        acc = jnp.where(bk == b, (rb_ref[b, h] - base) * LOG2_E, acc)
    o_ref[0, 0] = acc


def _bias_tables(rel_bias, buckets):
    nt, r, c = buckets.shape
    return pl.pallas_call(
        _bias_table_kernel,
        out_shape=jax.ShapeDtypeStruct((NSA_HEADS, nt, r, c), _F32),
        grid=(NSA_HEADS, nt),
        in_specs=[
            pl.BlockSpec(memory_space=pltpu.SMEM),
            pl.BlockSpec((1, r, c), lambda h, t: (t, 0, 0)),
        ],
        out_specs=pl.BlockSpec((1, 1, r, c), lambda h, t: (h, t, 0, 0)),
        compiler_params=_params(("arbitrary", "arbitrary")),
        name="bias_tables",
    )(rel_bias, jnp.asarray(buckets))


def _window_buckets(t):
    r = np.arange(3 * t)[:, None]
    c = np.arange(t)[None, :]
    return _t5_bucket_np(c - r + 2 * t)[None]


def _cmp_buckets(t):
    c = np.arange(CMP_PAD)[:, None]
    r = np.arange(t)[None, :]
    return _t5_bucket_np(r - (CMP_LEN - 1) - t + CMP_STRIDE * (CMP_PAD - c))[None]


def _compress_kernel(ch_ref, pe_ref, w1_ref, w2_ref, o_ref):
    half = w1_ref.shape[0] // 2
    ch = ch_ref[0]
    a = _dot(ch, w1_ref[:half, :])
    b = _dot(ch, w1_ref[half:, :])
    n = ch.shape[0]
    b_next = pltpu.roll(b, n - 1, axis=0)
    c0 = _dot(pe_ref[...], w1_ref[...])[0:1, :]
    hid = a + b_next + c0
    hid = hid * jax.nn.sigmoid(hid)
    o_ref[0, :CMP_PAD, :] = jnp.zeros((CMP_PAD, o_ref.shape[2]), o_ref.dtype)
    o_ref[0, CMP_PAD:, :] = _dot(hid.astype(_BF), w2_ref[...]).astype(o_ref.dtype)


def _compress(chunks, pe_flat, w1, w2):
    g, n, kd = chunks.shape
    d = w2.shape[1]
    return pl.pallas_call(
        _compress_kernel,
        out_shape=jax.ShapeDtypeStruct((g, CMP_PAD + n, d), _BF),
        grid=(g,),
        in_specs=[
            pl.BlockSpec((1, n, kd), lambda i: (i, 0, 0)),
            pl.BlockSpec(pe_flat.shape, lambda i: (0, 0)),
            pl.BlockSpec(w1.shape, lambda i: (0, 0)),
            pl.BlockSpec(w2.shape, lambda i: (0, 0)),
        ],
        out_specs=pl.BlockSpec((1, CMP_PAD + n, d), lambda i: (i, 0, 0)),
        compiler_params=_params(("arbitrary",)),
        name="nsa_compress",
    )(chunks, pe_flat, w1, w2)


def _nsa_cmp_kernel(q_ref, kc_ref, vc_ref, afar_ref, tc_ref, oc_ref, sel_ref):
    t = q_ref.shape[0]
    nc = kc_ref.shape[1] - CMP_PAD
    per_tile = t // CMP_STRIDE
    i = pl.program_id(1)
    thr = (i + 1) * per_tile - CMP_PAD
    near0 = pl.multiple_of((i + 1) * per_tile, per_tile)

    k_ext = jnp.concatenate([kc_ref[0, CMP_PAD:, :], kc_ref[0, pl.ds(near0, CMP_PAD), :]], axis=0)
    vcn_t = vc_ref[0, pl.ds(near0, CMP_PAD), :].astype(_F32).T.astype(_BF)
    jn = lax.broadcasted_iota(jnp.int32, (N_SLC_PAD, CMP_PAD), 0) * SLC_LEN
    cn = (thr + lax.broadcasted_iota(jnp.int32, (N_SLC_PAD, CMP_PAD), 1)) * CMP_STRIDE
    ovn_t = jnp.where((cn < jn + SLC_LEN) & (cn + CMP_LEN > jn), 1.0, 0.0).astype(_BF)
    a_near = jnp.concatenate([vcn_t, ovn_t], axis=0)
    a_far = afar_ref[0]

    far_mask = lax.broadcasted_iota(jnp.int32, (nc, t), 0) < thr
    c = lax.broadcasted_iota(jnp.int32, (CMP_PAD, t), 0)
    r = lax.broadcasted_iota(jnp.int32, (CMP_PAD, t), 1)
    near_mask = (CMP_STRIDE * c <= r - (CMP_LEN - 1) - t + CMP_STRIDE * CMP_PAD) & (c >= -thr)

    def logits(h):
        return _dot_nt(k_ext, q_ref[:, h * DKP:(h + 1) * DKP])

    imp = jnp.zeros((N_SLC_PAD, t), _F32)
    st_next = logits(0)
    for h in range(NSA_HPG):
        st = st_next
        if h + 1 < NSA_HPG:
            st_next = logits(h + 1)
        sf = jnp.where(far_mask, st[:nc], _NEG_INF)
        sn = jnp.where(near_mask, st[nc:] + tc_ref[h, 0], _NEG_INF)
        m = jnp.maximum(jnp.max(sf, axis=0, keepdims=True), jnp.max(sn, axis=0, keepdims=True))
        m = jnp.where(m == _NEG_INF, 0.0, m)
        pf = jnp.exp2(sf - m)
        pn = jnp.exp2(sn - m)
        den = jnp.sum(pf, axis=0, keepdims=True) + jnp.sum(pn, axis=0, keepdims=True)
        inv = 1.0 / jnp.maximum(den, 1e-30)
        pf = pf * inv
        pn = pn * inv
        pf_hi = pf.astype(_BF)
        pn_hi = pn.astype(_BF)
        res = _dot(a_far, pf_hi) + _dot(a_near, pn_hi)
        oc_ref[h * NSA_DV:(h + 1) * NSA_DV, :] = res[:NSA_DV].astype(oc_ref.dtype)
        pf_lo = (pf - pf_hi.astype(_F32)).astype(_BF)
        pn_lo = (pn - pn_hi.astype(_F32)).astype(_BF)
        imp = imp + res[NSA_DV:] + (_dot(a_far[NSA_DV:], pf_lo) + _dot(a_near[NSA_DV:], pn_lo))

    jj = lax.broadcasted_iota(jnp.int32, (N_SLC_PAD, t), 0)
    tq = i * t + lax.broadcasted_iota(jnp.int32, (N_SLC_PAD, t), 1)
    cur = tq >> SLC_SHIFT
    valid = jj * SLC_LEN <= tq
    forced = (jj == 0) | ((jj <= cur) & (jj >= cur - 1))
    work = jnp.where(valid, jnp.where(forced, jnp.inf, imp), _NEG_INF)
    blk = jj.astype(_F32)
    sel = jnp.zeros((N_SLC_PAD, t), _F32)
    for _ in range(SLC_TOPN):
        mx = jnp.max(work, axis=0, keepdims=True)
        first = jnp.min(jnp.where(work == mx, blk, float(N_SLC_PAD)), axis=0, keepdims=True)
        hit = blk == first
        sel = jnp.where(hit & (mx > _NEG_INF), 1.0, sel)
        work = jnp.where(hit, _NEG_INF, work)
    sel_ref[0] = sel


def _nsa_cmp(proj, kc_pad, vc_pad, a_far, tc):
    s = proj.shape[0]
    t = ATT_TILE
    gw = NSA_HPG * DKP
    return pl.pallas_call(
        _nsa_cmp_kernel,
        out_shape=(
            jax.ShapeDtypeStruct((NSA_HEADS * NSA_DV, s), _BF),
            jax.ShapeDtypeStruct((NSA_GROUPS, N_SLC_PAD, s), _F32),
        ),
        grid=(NSA_GROUPS, s // t),
        in_specs=[
            pl.BlockSpec((t, gw), lambda g, i: (i, COL_QN // gw + g)),
            pl.BlockSpec((1,) + kc_pad.shape[1:], lambda g, i: (g, 0, 0)),
            pl.BlockSpec((1,) + vc_pad.shape[1:], lambda g, i: (g, 0, 0)),
            pl.BlockSpec((1,) + a_far.shape[1:], lambda g, i: (g, 0, 0)),
            pl.BlockSpec((NSA_HPG, 1, CMP_PAD, t), lambda g, i: (g, 0, 0, 0)),
        ],
        out_specs=(
            pl.BlockSpec((NSA_HPG * NSA_DV, t), lambda g, i: (g, i)),
            pl.BlockSpec((1, N_SLC_PAD, t), lambda g, i: (g, 0, i)),
        ),
        compiler_params=_params(("parallel", "arbitrary")),
        name="nsa_cmp",
    )(proj, kc_pad, vc_pad, a_far, tc)


def _nsa_main_kernel(q_ref, ks_ref, vs_ref, kw_ref, vw_ref, selt_ref, tw_ref, oc_ref, gate_ref,
                     o_ref, m_ref, l_ref, acc_ref, *, n_sub):
    t = q_ref.shape[0]

    def value_tile(ref, tile):
        return ref[pl.ds(pl.multiple_of(tile * t, t), t), :]

    tk = n_sub * t
    g = pl.program_id(0)
    i = pl.program_id(1)
    blocks_per_tile = t // SLC_LEN
    qs = [q_ref[:, h * DKP:(h + 1) * DKP] for h in range(NSA_HPG)]

    m_ref[...] = jnp.full(m_ref.shape, -1e30, _F32)
    l_ref[...] = jnp.zeros(l_ref.shape, _F32)
    acc_ref[...] = jnp.zeros(acc_ref.shape, _F32)

    def selected(tile0, n_tiles):
        rows = [jnp.broadcast_to(selt_ref[0, pl.ds(tile0 * blocks_per_tile + b, 1), :], (SLC_LEN, t))
                for b in range(n_tiles * blocks_per_tile)]
        return jnp.concatenate(rows, axis=0) > 0.5

    def run_heads(k, vts, mask, bias):
        def logits(h):
            st = _dot_nt(k, qs[h])
            if bias is not None:
                st = st + bias(h)
            return jnp.where(mask, st, _NEG_INF)

        st = logits(0)
        for h in range(NSA_HPG):
            st_next = logits(h + 1) if h + 1 < NSA_HPG else None
            m, l, acc = _online_update_t(st, vts, m_ref[h], l_ref[h], acc_ref[h])
            m_ref[h] = m
            l_ref[h] = l
            acc_ref[h] = acc
            st = st_next

    def far_chunk(tile0, n_tiles, first_new):
        r0 = pl.multiple_of(tile0 * t, t)
        mask = selected(tile0, n_tiles)
        if first_new is not None:
            kpos = r0 + lax.broadcasted_iota(jnp.int32, (n_tiles * t, t), 0)
            mask = mask & (kpos >= first_new) & (kpos < n_far * t)
        run_heads(ks_ref[pl.ds(r0, n_tiles * t), :],
                  [value_tile(vs_ref, tile0 + a) for a in range(n_tiles)], mask, None)

    n_far = jnp.maximum(i - 1, 0)
    n_full = n_far // n_sub
    n_left = n_far - n_full * n_sub

    def far_body(c, carry):
        far_chunk(c * n_sub, n_sub, None)
        return carry

    lax.fori_loop(0, n_full, far_body, 0)

    half = n_sub // 2

    @pl.when((n_left > 0) & (n_left <= half))
    def _():
        far_chunk(jnp.maximum(n_far - half, 0), half, n_full * tk)

    @pl.when(n_left > half)
    def _():
        far_chunk(jnp.maximum(n_far - n_sub, 0), n_sub, n_full * tk)

    prev = jnp.maximum(i - 1, 0)
    row2 = lax.broadcasted_iota(jnp.int32, (2 * t, t), 0)
    col2 = lax.broadcasted_iota(jnp.int32, (2 * t, t), 1)
    near_mask = (jnp.concatenate([selected(prev, 1), selected(i, 1)], axis=0)
                 & (col2 - row2 + t >= 0) & (row2 >= jnp.where(i >= 1, 0, t)))
    k_near = jnp.concatenate([ks_ref[pl.ds(pl.multiple_of(prev * t, t), t), :],
                              ks_ref[pl.ds(pl.multiple_of(i * t, t), t), :]], axis=0)
    run_heads(k_near, [value_tile(vs_ref, prev), value_tile(vs_ref, i)], near_mask,
              lambda h: tw_ref[h, 0, t:, :])

    prev2 = jnp.maximum(i - 2, 0)
    w_tiles = (prev2, prev, i)
    k_win = jnp.concatenate([kw_ref[pl.ds(pl.multiple_of(a * t, t), t), :] for a in w_tiles], axis=0)
    row3 = lax.broadcasted_iota(jnp.int32, (3 * t, t), 0)
    col3 = lax.broadcasted_iota(jnp.int32, (3 * t, t), 1)
    dist = col3 - row3 + 2 * t
    first_row = jnp.where(i >= 2, 0, jnp.where(i >= 1, t, 2 * t))
    win_mask = (dist >= 0) & (dist < WINDOW) & (row3 >= first_row)

    def window_logits(h):
        return jnp.where(win_mask, _dot_nt(k_win, qs[h]) + tw_ref[h, 0], _NEG_INF)

    gates_t = jax.nn.sigmoid(gate_ref[...].astype(_F32)).T
    st_next = window_logits(0)
    for h in range(NSA_HPG):
        st = st_next
        if h + 1 < NSA_HPG:
            st_next = window_logits(h + 1)
        m = jnp.max(st, axis=0, keepdims=True)
        pt = jnp.exp2(st - m)
        den = jnp.sum(pt, axis=0, keepdims=True)
        pb = pt.astype(_BF)
        o_w = functools.reduce(jnp.add, [_dot_tn(value_tile(vw_ref, a), pb[n * t:(n + 1) * t, :])
                                         for n, a in enumerate(w_tiles)]) * (1.0 / den)
        o_s = acc_ref[h] * (1.0 / l_ref[h])
        o_c = oc_ref[h * NSA_DV:(h + 1) * NSA_DV, :].astype(_F32)
        gc, gs, gw = [jnp.where(g == 0, gates_t[h * N_BRANCH + b:h * N_BRANCH + b + 1, :],
                                gates_t[(NSA_HPG + h) * N_BRANCH + b:(NSA_HPG + h) * N_BRANCH + b + 1, :])
                      for b in range(N_BRANCH)]
        o_ref[:, h * NSA_DV:(h + 1) * NSA_DV] = (gc * o_c + gs * o_s + gw * o_w).T.astype(o_ref.dtype)


def _nsa_main(proj, sel_t, tw, oc_t, *, n_sub=4):
    s = proj.shape[0]
    t = ATT_TILE
    nb = s // t
    gw = NSA_HPG * DKP
    kn0 = COL_KN // DKP
    vn0 = COL_VN // NSA_DV
    return pl.pallas_call(
        functools.partial(_nsa_main_kernel, n_sub=n_sub),
        out_shape=jax.ShapeDtypeStruct((s, NSA_HEADS * NSA_DV), _BF),
        grid=(NSA_GROUPS, nb),
        in_specs=[
            pl.BlockSpec((t, gw), lambda g, i: (i, COL_QN // gw + g)),
            pl.BlockSpec((s, DKP), lambda g, i: (0, kn0 + 1 * NSA_GROUPS + g)),
            pl.BlockSpec((s, NSA_DV), lambda g, i: (0, vn0 + 1 * NSA_GROUPS + g)),
            pl.BlockSpec((s, DKP), lambda g, i: (0, kn0 + 2 * NSA_GROUPS + g)),
            pl.BlockSpec((s, NSA_DV), lambda g, i: (0, vn0 + 2 * NSA_GROUPS + g)),
            pl.BlockSpec((1, N_SLC_PAD, t), lambda g, i: (g, 0, i)),
            pl.BlockSpec((NSA_HPG, 1, 3 * t, t), lambda g, i: (g, 0, 0, 0)),
            pl.BlockSpec((NSA_HPG * NSA_DV, t), lambda g, i: (g, i)),
            pl.BlockSpec((t, LANES), lambda g, i: (i, COL_GATE // LANES)),
        ],
        out_specs=pl.BlockSpec((t, NSA_HPG * NSA_DV), lambda g, i: (i, g)),
        scratch_shapes=[
            pltpu.VMEM((NSA_HPG, 1, t), _F32),
            pltpu.VMEM((NSA_HPG, 1, t), _F32),
            pltpu.VMEM((NSA_HPG, NSA_DV, t), _F32),
        ],
        compiler_params=_params(("parallel", "arbitrary")),
        name="nsa_main",
    )(proj, proj, proj, proj, proj, sel_t, tw, oc_t, proj)


def _outproj_kernel(x_ref, mod_ref, om_ref, on_ref, w_ref, o_ref, *, mod_row):
    half = om_ref.shape[1]
    y = _dot(om_ref[...], w_ref[:half, :]) + _dot(on_ref[...], w_ref[half:, :])
    o_ref[...] = x_ref[...] + mod_ref[mod_row:mod_row + 1, :] * y


def _outproj(x, mod, o_mla, o_nsa, w_out, *, mod_row, tm=512):
    s = x.shape[0]
    return pl.pallas_call(
        functools.partial(_outproj_kernel, mod_row=mod_row),
        out_shape=jax.ShapeDtypeStruct((s, D_MODEL), _F32),
        grid=(s // tm,),
        in_specs=[
            pl.BlockSpec((tm, D_MODEL), lambda i: (i, 0)),
            pl.BlockSpec((9, D_MODEL), lambda i: (0, 0)),
            pl.BlockSpec((tm, o_mla.shape[1]), lambda i: (i, 0)),
            pl.BlockSpec((tm, o_nsa.shape[1]), lambda i: (i, 0)),
            pl.BlockSpec(w_out.shape, lambda i: (0, 0)),
        ],
        out_specs=pl.BlockSpec((tm, D_MODEL), lambda i: (i, 0)),
        compiler_params=_params(("parallel",)),
        name="outproj",
    )(x, mod, o_mla, o_nsa, w_out)


def _pad_cols(w, width):
    return jnp.pad(w, ((0, 0), (0, width - w.shape[1])))


def _pad_last(w, width):
    return jnp.pad(w, [(0, 0)] * (w.ndim - 1) + [(0, width - w.shape[-1])])


def _rotate_half_cols(w):
    half = MLA_ROPE // 2
    return jnp.concatenate([-w[..., half:], w[..., :half]], axis=-1)


def _w_in_padded(w_in):
    d = w_in.shape[0]
    w_in = w_in.astype(_BF)
    sizes = (MLA_Q_RANK, MLA_KV_RANK, MLA_ROPE, NSA_HEADS * NSA_DK,
             N_BRANCH * NSA_GROUPS * NSA_DK, N_BRANCH * NSA_GROUPS * NSA_DV, NSA_HEADS * N_BRANCH)
    offs = np.concatenate([[0], np.cumsum(sizes)])
    cq, ckv, kr, qn, kn, vn, gn = [w_in[:, offs[a]:offs[a + 1]] for a in range(len(sizes))]
    qn = _pad_last(qn.reshape(d, NSA_HEADS, NSA_DK), DKP).reshape(d, NSA_HEADS * DKP)
    kn = _pad_last(kn.reshape(d, N_BRANCH * NSA_GROUPS, NSA_DK), DKP).reshape(d, -1)
    cols = [cq, _pad_cols(gn, LANES), _pad_cols(kr, LANES), ckv,
            _pad_cols(_rotate_half_cols(kr), LANES),
            jnp.zeros((d, COL_QN - COL_KR_ROT - LANES), w_in.dtype), qn, kn, vn]
    w = jnp.concatenate(cols, axis=1)
    assert w.shape[1] == D_IN_PAD
    colscale = np.ones((1, D_IN_PAD), np.float32)
    colscale[:, COL_QN:COL_KN] = NSA_DK ** -0.5 * LOG2_E
    return w, jnp.asarray(colscale)


def _w_uq_padded(w_uq):
    r = w_uq.shape[0]
    w = w_uq.reshape(r, MLA_HEADS, MLA_NOPE + MLA_ROPE)
    nope = w[..., :MLA_NOPE].reshape(r, -1)
    rope = w[..., MLA_NOPE:]
    rope_p = _pad_last(rope, LANES).reshape(r, -1)
    rot_p = _pad_last(_rotate_half_cols(rope), LANES).reshape(r, -1)
    return jnp.concatenate([nope, rope_p, rot_p], axis=1).astype(_BF)


def _cmp_w1_padded(w1, d, dpad):
    return _pad_last(w1.reshape(CMP_LEN, d, CMP_HIDDEN).transpose(0, 2, 1), dpad) \
        .transpose(0, 2, 1).reshape(CMP_LEN * dpad, CMP_HIDDEN).astype(_BF)


def _rope_tables(s):
    inv = ROPE_THETA ** (-jnp.arange(0, MLA_ROPE, 2, dtype=_F32) / MLA_ROPE)
    ang = jnp.arange(s, dtype=_F32)[:, None] * inv[None, :]
    cos = _pad_cols(jnp.tile(jnp.cos(ang), (1, 2)), LANES)
    sin = _pad_cols(jnp.tile(jnp.sin(ang), (1, 2)), LANES)
    return cos, sin


def _overlap_t(s):
    s_start = np.arange(N_SLC_PAD)[:, None] * SLC_LEN
    c_start = np.arange(s // CMP_STRIDE)[None, :] * CMP_STRIDE
    ov = (c_start < s_start + SLC_LEN) & (c_start + CMP_LEN > s_start)
    return jnp.asarray(ov.astype(np.float32), _BF)


def _mixer_heads(x, mod, norm_mix, w_in, mla_q_norm, w_uq, mla_kv_norm, w_ukv, pe_cmp_k, w_cmp_k1,
                 w_cmp_k2, pe_cmp_v, w_cmp_v1, w_cmp_v2, rel_bias):
    s = x.shape[0]
    assert s % ATT_TILE == 0 and s // SLC_LEN <= N_SLC_PAD and s >= 3 * ATT_TILE
    w_pad, colscale = _w_in_padded(w_in)
    proj = _inproj(x, mod, norm_mix.reshape(1, -1), w_pad, colscale, mod_row=3)

    cos, sin = _rope_tables(s)
    q, k, v = _mla_prep(proj, mla_q_norm.reshape(1, -1), mla_kv_norm.reshape(1, -1),
                        _w_uq_padded(w_uq), w_ukv.astype(_BF), cos, sin)
    o_mla = _mla_attn(q, k, v)

    n_chunks = s // CMP_STRIDE
    kn0 = COL_KN
    k_cmp = jnp.stack([proj[:, kn0 + g * DKP:kn0 + (g + 1) * DKP] for g in range(NSA_GROUPS)])
    v_cmp = jnp.stack([proj[:, COL_VN + g * NSA_DV:COL_VN + (g + 1) * NSA_DV] for g in range(NSA_GROUPS)])
    pe_k = jnp.broadcast_to(_pad_last(pe_cmp_k, DKP).reshape(1, -1), (8, CMP_LEN * DKP)).astype(_BF)
    pe_v = jnp.broadcast_to(pe_cmp_v.reshape(1, -1), (8, CMP_LEN * NSA_DV)).astype(_BF)
    kc = _compress(k_cmp.reshape(NSA_GROUPS, n_chunks, CMP_STRIDE * DKP), pe_k,
                   _cmp_w1_padded(w_cmp_k1, NSA_DK, DKP), _pad_cols(w_cmp_k2, DKP).astype(_BF))
    vc = _compress(v_cmp.reshape(NSA_GROUPS, n_chunks, CMP_STRIDE * NSA_DV), pe_v,
                   w_cmp_v1.astype(_BF), w_cmp_v2.astype(_BF))
    tc = _bias_tables(rel_bias, _cmp_buckets(ATT_TILE))
    tw = _bias_tables(rel_bias, _window_buckets(ATT_TILE))
    ov_t = _overlap_t(s)
    a_far = jnp.stack([jnp.concatenate([vc[g, CMP_PAD:].T, ov_t], axis=0) for g in range(NSA_GROUPS)])
    oc_t, sel_t = _nsa_cmp(proj, kc, vc, a_far, tc)
    o_nsa = _nsa_main(proj, sel_t, tw, oc_t)
    return o_mla, o_nsa


def kernel(x, c, w_ada, b_ada, norm_ffn1, w1_gate, w1_up, w1_down, norm_mix, w_in, mla_q_norm, w_uq, mla_kv_norm, w_ukv, pe_cmp_k, w_cmp_k1, w_cmp_k2, pe_cmp_v, w_cmp_v1, w_cmp_v2, rel_bias, w_out, norm_ffn2, w2_gate, w2_up, w2_down, norm_final):
    assert x.shape[0] == 1 and w_ada.shape[0] == 1
    xs = x[0]
    fin = norm_final.reshape(1, -1)
    mod = _ada_mod(c, w_ada[0], b_ada[0]).reshape(9, D_MODEL)
    xs = _ffn(xs, mod, norm_ffn1[0].reshape(1, -1), w1_gate[0], w1_up[0], w1_down[0], fin,
              mod_row=0, final_norm=False)
    o_mla, o_nsa = _mixer_heads(xs, mod, norm_mix[0], w_in[0], mla_q_norm[0], w_uq[0], mla_kv_norm[0],
                                w_ukv[0], pe_cmp_k[0], w_cmp_k1[0], w_cmp_k2[0], pe_cmp_v[0],
                                w_cmp_v1[0], w_cmp_v2[0], rel_bias)
    xs = _outproj(xs, mod, o_mla, o_nsa, w_out[0].astype(_BF), mod_row=5)
    xs = _ffn(xs, mod, norm_ffn2[0].reshape(1, -1), w2_gate[0], w2_up[0], w2_down[0], fin,
              mod_row=6, final_norm=True)
    return xs[None]
```

```python
import functools
import math

import numpy as np
import jax
import jax.numpy as jnp
from jax import lax
from jax.experimental import pallas as pl
from jax.experimental.pallas import tpu as pltpu

D_MODEL = 2048
D_FF = 5632
EPS = 1e-6
MLA_HEADS = 8
MLA_Q_RANK = 768
MLA_KV_RANK = 512
MLA_NOPE = 128
MLA_ROPE = 64
MLA_V = 128
ROPE_THETA = 10000.0
NSA_HEADS = 8
NSA_GROUPS = 2
NSA_HPG = NSA_HEADS // NSA_GROUPS
NSA_DK = 192
NSA_DV = 128
CMP_LEN = 32
CMP_STRIDE = 16
CMP_HIDDEN = 256
SLC_LEN = 64
SLC_SHIFT = 6
SLC_TOPN = 16
WINDOW = 512
N_BRANCH = 3
REL_BUCKETS = 32
REL_MAX_DIST = 128

LANES = 128
VMEM_LIMIT = 60000 * 1024

DKP = 256
ATT_TILE = 256
MLA_TILE = 512
CMP_PAD = 128
N_SLC_PAD = 128

_BF = jnp.bfloat16
_F32 = jnp.float32
_NEG_INF = float("-inf")
LOG2_E = math.log2(math.e)

COL_CQ = 0
COL_GATE = 768
COL_KR = 896
COL_CKV = 1024
COL_KR_ROT = 1536
COL_QN = 2048
COL_KN = COL_QN + NSA_HEADS * DKP
COL_VN = COL_KN + N_BRANCH * NSA_GROUPS * DKP
D_IN_PAD = COL_VN + N_BRANCH * NSA_GROUPS * NSA_DV


def _dot(a, b):
    return jnp.dot(a, b, preferred_element_type=_F32)


def _dot_nt(a, b):
    return lax.dot_general(a, b, (((1,), (1,)), ((), ())), preferred_element_type=_F32)


def _rms(x, gain):
    return x * lax.rsqrt(jnp.mean(x * x, axis=-1, keepdims=True) + EPS) * gain


def _params(semantics):
    return pltpu.CompilerParams(dimension_semantics=semantics, vmem_limit_bytes=VMEM_LIMIT)


ADA_ROWS = 256


def _ada_kernel(c_ref, w_ref, b_ref, o_ref):
    tn = o_ref.shape[1]

    def body(k, acc):
        r = pl.multiple_of(k * ADA_ROWS, ADA_ROWS)
        c = c_ref[pl.ds(r, ADA_ROWS), :]
        s = c * jax.nn.sigmoid(c)
        prod = w_ref[pl.ds(r, ADA_ROWS), :] * s
        return acc + jnp.sum(prod.reshape(ADA_ROWS // 8, 8, tn), axis=0)

    acc = lax.fori_loop(0, D_MODEL // ADA_ROWS, body, jnp.zeros((8, tn), _F32))
    o_ref[...] = jnp.sum(acc, axis=0, keepdims=True) + b_ref[...]


def _ada_mod(c, w_ada, b_ada):
    n = w_ada.shape[1]
    tn = 1024
    return pl.pallas_call(
        _ada_kernel,
        out_shape=jax.ShapeDtypeStruct((1, n), _F32),
        grid=(n // tn,),
        in_specs=[
            pl.BlockSpec((D_MODEL, 1), lambda j: (0, 0)),
            pl.BlockSpec((D_MODEL, tn), lambda j: (0, j)),
            pl.BlockSpec((1, tn), lambda j: (0, j)),
        ],
        out_specs=pl.BlockSpec((1, tn), lambda j: (0, j)),
        compiler_params=_params(("arbitrary",)),
        name="ada_mod",
    )(c.reshape(D_MODEL, 1), w_ada, b_ada.reshape(1, n))


def _ffn_kernel(x_ref, mod_ref, gain_ref, wg_ref, wu_ref, wd_ref, fin_ref, o_ref, h_ref,
                *, mod_row, final_norm):
    j = pl.program_id(1)

    @pl.when(j == 0)
    def _():
        y = _rms(x_ref[...], gain_ref[...])
        h = y * (1.0 + mod_ref[mod_row + 1:mod_row + 2, :]) + mod_ref[mod_row:mod_row + 1, :]
        h_ref[...] = h.astype(_BF)
        o_ref[...] = jnp.zeros_like(o_ref)

    h = h_ref[...]
    g = _dot(h, wg_ref[...].astype(_BF))
    u = _dot(h, wu_ref[...].astype(_BF))
    a = (g * jax.nn.sigmoid(g)) * u
    o_ref[...] += _dot(a.astype(_BF), wd_ref[...].astype(_BF))

    @pl.when(j == pl.num_programs(1) - 1)
    def _():
        x2 = x_ref[...] + (0.5 * mod_ref[mod_row + 2:mod_row + 3, :]) * o_ref[...]
        if final_norm:
            x2 = _rms(x2, fin_ref[...])
        o_ref[...] = x2


def _ffn(x, mod, gain, wg, wu, wd, fin, *, mod_row, final_norm, tm=1024, tf=256):
    s = x.shape[0]
    tm = min(tm, s)
    kern = functools.partial(_ffn_kernel, mod_row=mod_row, final_norm=final_norm)
    return pl.pallas_call(
        kern,
        out_shape=jax.ShapeDtypeStruct((s, D_MODEL), _F32),
        grid=(s // tm, D_FF // tf),
        in_specs=[
            pl.BlockSpec((tm, D_MODEL), lambda i, j: (i, 0)),
            pl.BlockSpec((9, D_MODEL), lambda i, j: (0, 0)),
            pl.BlockSpec((1, D_MODEL), lambda i, j: (0, 0)),
            pl.BlockSpec((D_MODEL, tf), lambda i, j: (0, j)),
            pl.BlockSpec((D_MODEL, tf), lambda i, j: (0, j)),
            pl.BlockSpec((tf, D_MODEL), lambda i, j: (j, 0)),
            pl.BlockSpec((1, D_MODEL), lambda i, j: (0, 0)),
        ],
        out_specs=pl.BlockSpec((tm, D_MODEL), lambda i, j: (i, 0)),
        scratch_shapes=[pltpu.VMEM((tm, D_MODEL), _BF)],
        compiler_params=_params(("parallel", "arbitrary")),
        name="ffn_final" if final_norm else "ffn",
    )(x, mod, gain, wg, wu, wd, fin)


def _inproj_kernel(x_ref, mod_ref, gain_ref, w_ref, cs_ref, o_ref, h_ref, *, mod_row):
    @pl.when(pl.program_id(1) == 0)
    def _():
        y = _rms(x_ref[...], gain_ref[...])
        h = y * (1.0 + mod_ref[mod_row + 1:mod_row + 2, :]) + mod_ref[mod_row:mod_row + 1, :]
        h_ref[...] = h.astype(_BF)

    o_ref[...] = (_dot(h_ref[...], w_ref[...]) * cs_ref[...]).astype(o_ref.dtype)


def _inproj(x, mod, gain, w_pad, colscale, *, mod_row, tm=1024, tn=1280):
    s = x.shape[0]
    tm = min(tm, s)
    n = w_pad.shape[1]
    return pl.pallas_call(
        functools.partial(_inproj_kernel, mod_row=mod_row),
        out_shape=jax.ShapeDtypeStruct((s, n), _BF),
        grid=(s // tm, n // tn),
        in_specs=[
            pl.BlockSpec((tm, D_MODEL), lambda i, j: (i, 0)),
            pl.BlockSpec((9, D_MODEL), lambda i, j: (0, 0)),
            pl.BlockSpec((1, D_MODEL), lambda i, j: (0, 0)),
            pl.BlockSpec((D_MODEL, tn), lambda i, j: (0, j)),
            pl.BlockSpec((1, tn), lambda i, j: (0, j)),
        ],
        out_specs=pl.BlockSpec((tm, tn), lambda i, j: (i, j)),
        scratch_shapes=[pltpu.VMEM((tm, D_MODEL), _BF)],
        compiler_params=_params(("parallel", "arbitrary")),
        name="inproj",
    )(x, mod, gain, w_pad, colscale)


def _mla_prep_kernel(cq_ref, ckv_ref, kr_ref, krr_ref, gq_ref, gkv_ref, wq_ref, wkv_ref,
                     cos_ref, sin_ref, q_ref, k_ref, v_ref):
    scale = (MLA_NOPE + MLA_ROPE) ** -0.5 * LOG2_E
    cos = cos_ref[...]
    sin = sin_ref[...]
    hq = _rms(cq_ref[...].astype(_F32), gq_ref[...]).astype(_BF)
    qa = _dot(hq, wq_ref[...])
    nh = MLA_HEADS * LANES
    for h in range(MLA_HEADS):
        lo = h * LANES
        nope = qa[:, lo:lo + LANES]
        a = qa[:, nh + lo:nh + lo + LANES]
        b = qa[:, 2 * nh + lo:2 * nh + lo + LANES]
        q_ref[:, h * DKP:h * DKP + LANES] = (nope * scale).astype(_BF)
        q_ref[:, h * DKP + LANES:(h + 1) * DKP] = ((a * cos + b * sin) * scale).astype(_BF)
    hkv = _rms(ckv_ref[...].astype(_F32), gkv_ref[...]).astype(_BF)
    kv = _dot(hkv, wkv_ref[...])
    k_rope = (kr_ref[...].astype(_F32) * cos + krr_ref[...].astype(_F32) * sin).astype(_BF)
    for h in range(MLA_HEADS):
        lo = h * (MLA_NOPE + MLA_V)
        k_ref[:, h * DKP:h * DKP + LANES] = kv[:, lo:lo + MLA_NOPE].astype(_BF)
        k_ref[:, h * DKP + LANES:(h + 1) * DKP] = k_rope
        v_ref[:, h * MLA_V:(h + 1) * MLA_V] = kv[:, lo + MLA_NOPE:lo + MLA_NOPE + MLA_V].astype(_BF)


def _mla_prep(proj, gq, gkv, wq, wkv, cos, sin, *, tm=512):
    s = proj.shape[0]
    const = lambda i: (0, 0)
    return pl.pallas_call(
        _mla_prep_kernel,
        out_shape=(
            jax.ShapeDtypeStruct((s, MLA_HEADS * DKP), _BF),
            jax.ShapeDtypeStruct((s, MLA_HEADS * DKP), _BF),
            jax.ShapeDtypeStruct((s, MLA_HEADS * MLA_V), _BF),
        ),
        grid=(s // tm,),
        in_specs=[
            pl.BlockSpec((tm, MLA_Q_RANK), lambda i: (i, COL_CQ // MLA_Q_RANK)),
            pl.BlockSpec((tm, MLA_KV_RANK), lambda i: (i, COL_CKV // MLA_KV_RANK)),
            pl.BlockSpec((tm, LANES), lambda i: (i, COL_KR // LANES)),
            pl.BlockSpec((tm, LANES), lambda i: (i, COL_KR_ROT // LANES)),
            pl.BlockSpec((1, MLA_Q_RANK), const),
            pl.BlockSpec((1, MLA_KV_RANK), const),
            pl.BlockSpec(wq.shape, const),
            pl.BlockSpec(wkv.shape, const),
            pl.BlockSpec((tm, LANES), lambda i: (i, 0)),
            pl.BlockSpec((tm, LANES), lambda i: (i, 0)),
        ],
        out_specs=(
            pl.BlockSpec((tm, MLA_HEADS * DKP), lambda i: (i, 0)),
            pl.BlockSpec((tm, MLA_HEADS * DKP), lambda i: (i, 0)),
            pl.BlockSpec((tm, MLA_HEADS * MLA_V), lambda i: (i, 0)),
        ),
        compiler_params=_params(("parallel",)),
        name="mla_prep",
    )(proj, proj, proj, proj, gq, gkv, wq, wkv, cos, sin)


def _dot_tn(a, b):
    return lax.dot_general(a, b, (((0,), (0,)), ((), ())), preferred_element_type=_F32)


def _online_update_t(st, vs, m, l, acc):
    t = vs[0].shape[0]
    m_new = jnp.maximum(m, jnp.max(st, axis=0, keepdims=True))
    alpha = jnp.exp2(m - m_new)
    pt = jnp.exp2(st - m_new)
    l = alpha * l + jnp.sum(pt, axis=0, keepdims=True)
    pb = pt.astype(_BF)
    pv = functools.reduce(jnp.add, [_dot_tn(v, pb[a * t:(a + 1) * t, :]) for a, v in enumerate(vs)])
    return m_new, l, alpha * acc + pv


def _mla_attn_kernel(q_ref, k_ref, v_ref, o_ref, m_ref, l_ref, acc_ref, *, n_sub):
    t = q_ref.shape[0]
    tk = n_sub * t
    hp = q_ref.shape[1] // DKP
    i = pl.program_id(1)
    qs = [q_ref[:, h * DKP:(h + 1) * DKP] for h in range(hp)]
    n_full = i // n_sub
    tail_b0 = jnp.maximum(i + 1 - n_sub, 0)

    m_ref[...] = jnp.full(m_ref.shape, -1e30, _F32)
    l_ref[...] = jnp.zeros(l_ref.shape, _F32)
    acc_ref[...] = jnp.zeros(acc_ref.shape, _F32)

    def first_tile(c):
        return jnp.where(c < n_full, c * n_sub, tail_b0)

    def logits(c, h):
        r0 = pl.multiple_of(first_tile(c) * t, t)
        return _dot_nt(k_ref[pl.ds(r0, tk), h * DKP:(h + 1) * DKP], qs[h])

    def update(c, h, st):
        b0 = first_tile(c)
        vs = [v_ref[pl.ds(pl.multiple_of((b0 + a) * t, t), t), h * MLA_V:(h + 1) * MLA_V] for a in range(n_sub)]
        m, l, acc = _online_update_t(st, vs, m_ref[h], l_ref[h], acc_ref[h])
        m_ref[h] = m
        l_ref[h] = l
        acc_ref[h] = acc

    def body(c, st):
        for h in range(hp):
            st_next = logits(c, h + 1) if h + 1 < hp else logits(c + 1, 0)
            update(c, h, st)
            st = st_next
        return st

    st = lax.fori_loop(0, n_full, body, logits(0, 0))
    kpos = tail_b0 * t + lax.broadcasted_iota(jnp.int32, (tk, t), 0)
    qpos = i * t + lax.broadcasted_iota(jnp.int32, (tk, t), 1)
    tail_mask = (kpos >= n_full * tk) & (kpos <= qpos)
    for h in range(hp):
        st_next = logits(n_full, h + 1) if h + 1 < hp else None
        update(n_full, h, jnp.where(tail_mask, st, _NEG_INF))
        st = st_next
    for h in range(hp):
        o_ref[:, h * MLA_V:(h + 1) * MLA_V] = (acc_ref[h] * (1.0 / l_ref[h])).T.astype(o_ref.dtype)


def _mla_attn(q, k, v, *, n_sub=2, hp=4):
    s = q.shape[0]
    t = min(MLA_TILE, s // n_sub)
    nb = s // t
    return pl.pallas_call(
        functools.partial(_mla_attn_kernel, n_sub=n_sub),
        out_shape=jax.ShapeDtypeStruct((s, MLA_HEADS * MLA_V), _BF),
        grid=(MLA_HEADS // hp, nb),
        in_specs=[
            pl.BlockSpec((t, hp * DKP), lambda h, i: (i, h)),
            pl.BlockSpec((s, hp * DKP), lambda h, i: (0, h), pipeline_mode=pl.Buffered(1)),
            pl.BlockSpec((s, hp * MLA_V), lambda h, i: (0, h), pipeline_mode=pl.Buffered(1)),
        ],
        out_specs=pl.BlockSpec((t, hp * MLA_V), lambda h, i: (i, h)),
        scratch_shapes=[
            pltpu.VMEM((hp, 1, t), _F32),
            pltpu.VMEM((hp, 1, t), _F32),
            pltpu.VMEM((hp, MLA_V, t), _F32),
        ],
        compiler_params=_params(("parallel", "arbitrary")),
        name="mla_attn",
    )(q, k, v)


def _t5_bucket_np(dist):
    n = np.maximum(dist, 0)
    max_exact = REL_BUCKETS // 2
    nf = np.maximum(n, 1).astype(np.float32)
    large = max_exact + (np.log(nf / np.float32(max_exact)) / np.float32(math.log(REL_MAX_DIST / max_exact))
                         * np.float32(REL_BUCKETS - max_exact)).astype(np.int32)
    large = np.minimum(large, REL_BUCKETS - 1)
    return np.where(n < max_exact, n, large).astype(np.int32)


def _bias_table_kernel(rb_ref, bk_ref, o_ref):
    h = pl.program_id(0)
    bk = bk_ref[0]
    base = rb_ref[REL_BUCKETS - 1, h]
    acc = jnp.zeros(bk.shape, _F32)
    for b in range(REL_BUCKETS - 1):
        acc = jnp.where(bk == b, (rb_ref[b, h] - base) * LOG2_E, acc)
    o_ref[0, 0] = acc


def _bias_tables(rel_bias, buckets):
    nt, r, c = buckets.shape
    return pl.pallas_call(
        _bias_table_kernel,
        out_shape=jax.ShapeDtypeStruct((NSA_HEADS, nt, r, c), _F32),
        grid=(NSA_HEADS, nt),
        in_specs=[
            pl.BlockSpec(memory_space=pltpu.SMEM),
            pl.BlockSpec((1, r, c), lambda h, t: (t, 0, 0)),
        ],
        out_specs=pl.BlockSpec((1, 1, r, c), lambda h, t: (h, t, 0, 0)),
        compiler_params=_params(("arbitrary", "arbitrary")),
        name="bias_tables",
    )(rel_bias, jnp.asarray(buckets))


def _window_buckets(t):
    r = np.arange(3 * t)[:, None]
    c = np.arange(t)[None, :]
    return _t5_bucket_np(c - r + 2 * t)[None]


def _cmp_buckets(t):
    c = np.arange(CMP_PAD)[:, None]
    r = np.arange(t)[None, :]
    return _t5_bucket_np(r - (CMP_LEN - 1) - t + CMP_STRIDE * (CMP_PAD - c))[None]


def _compress_kernel(ch_ref, pe_ref, w1_ref, w2_ref, o_ref):
    half = w1_ref.shape[0] // 2
    ch = ch_ref[0]
    a = _dot(ch, w1_ref[:half, :])
    b = _dot(ch, w1_ref[half:, :])
    n = ch.shape[0]
    b_next = pltpu.roll(b, n - 1, axis=0)
    c0 = _dot(pe_ref[...], w1_ref[...])[0:1, :]
    hid = a + b_next + c0
    hid = hid * jax.nn.sigmoid(hid)
    o_ref[0, :CMP_PAD, :] = jnp.zeros((CMP_PAD, o_ref.shape[2]), o_ref.dtype)
    o_ref[0, CMP_PAD:, :] = _dot(hid.astype(_BF), w2_ref[...]).astype(o_ref.dtype)


def _compress(chunks, pe_flat, w1, w2):
    g, n, kd = chunks.shape
    d = w2.shape[1]
    return pl.pallas_call(
        _compress_kernel,
        out_shape=jax.ShapeDtypeStruct((g, CMP_PAD + n, d), _BF),
        grid=(g,),
        in_specs=[
            pl.BlockSpec((1, n, kd), lambda i: (i, 0, 0)),
            pl.BlockSpec(pe_flat.shape, lambda i: (0, 0)),
            pl.BlockSpec(w1.shape, lambda i: (0, 0)),
            pl.BlockSpec(w2.shape, lambda i: (0, 0)),
        ],
        out_specs=pl.BlockSpec((1, CMP_PAD + n, d), lambda i: (i, 0, 0)),
        compiler_params=_params(("arbitrary",)),
        name="nsa_compress",
    )(chunks, pe_flat, w1, w2)


def _nsa_cmp_kernel(q_ref, kc_ref, vc_ref, afar_ref, tc_ref, oc_ref, sel_ref):
    t = q_ref.shape[0]
    nc = kc_ref.shape[1] - CMP_PAD
    per_tile = t // CMP_STRIDE
    i = pl.program_id(1)
    thr = (i + 1) * per_tile - CMP_PAD
    near0 = pl.multiple_of((i + 1) * per_tile, per_tile)

    k_ext = jnp.concatenate([kc_ref[0, CMP_PAD:, :], kc_ref[0, pl.ds(near0, CMP_PAD), :]], axis=0)
    vcn_t = vc_ref[0, pl.ds(near0, CMP_PAD), :].astype(_F32).T.astype(_BF)
    jn = lax.broadcasted_iota(jnp.int32, (N_SLC_PAD, CMP_PAD), 0) * SLC_LEN
    cn = (thr + lax.broadcasted_iota(jnp.int32, (N_SLC_PAD, CMP_PAD), 1)) * CMP_STRIDE
    ovn_t = jnp.where((cn < jn + SLC_LEN) & (cn + CMP_LEN > jn), 1.0, 0.0).astype(_BF)
    a_near = jnp.concatenate([vcn_t, ovn_t], axis=0)
    a_far = afar_ref[0]

    far_mask = lax.broadcasted_iota(jnp.int32, (nc, t), 0) < thr
    c = lax.broadcasted_iota(jnp.int32, (CMP_PAD, t), 0)
    r = lax.broadcasted_iota(jnp.int32, (CMP_PAD, t), 1)
    near_mask = (CMP_STRIDE * c <= r - (CMP_LEN - 1) - t + CMP_STRIDE * CMP_PAD) & (c >= -thr)

    def logits(h):
        return _dot_nt(k_ext, q_ref[:, h * DKP:(h + 1) * DKP])

    imp = jnp.zeros((N_SLC_PAD, t), _F32)
    st_next = logits(0)
    for h in range(NSA_HPG):
        st = st_next
        if h + 1 < NSA_HPG:
            st_next = logits(h + 1)
        sf = jnp.where(far_mask, st[:nc], _NEG_INF)
        sn = jnp.where(near_mask, st[nc:] + tc_ref[h, 0], _NEG_INF)
        m = jnp.maximum(jnp.max(sf, axis=0, keepdims=True), jnp.max(sn, axis=0, keepdims=True))
        m = jnp.where(m == _NEG_INF, 0.0, m)
        pf = jnp.exp2(sf - m)
        pn = jnp.exp2(sn - m)
        den = jnp.sum(pf, axis=0, keepdims=True) + jnp.sum(pn, axis=0, keepdims=True)
        inv = 1.0 / jnp.maximum(den, 1e-30)
        pf = pf * inv
        pn = pn * inv
        pf_hi = pf.astype(_BF)
        pn_hi = pn.astype(_BF)
        res = _dot(a_far, pf_hi) + _dot(a_near, pn_hi)
        oc_ref[h * NSA_DV:(h + 1) * NSA_DV, :] = res[:NSA_DV].astype(oc_ref.dtype)
        pf_lo = (pf - pf_hi.astype(_F32)).astype(_BF)
        pn_lo = (pn - pn_hi.astype(_F32)).astype(_BF)
        imp = imp + res[NSA_DV:] + (_dot(a_far[NSA_DV:], pf_lo) + _dot(a_near[NSA_DV:], pn_lo))

    jj = lax.broadcasted_iota(jnp.int32, (N_SLC_PAD, t), 0)
    tq = i * t + lax.broadcasted_iota(jnp.int32, (N_SLC_PAD, t), 1)
    cur = tq >> SLC_SHIFT
    valid = jj * SLC_LEN <= tq
    forced = (jj == 0) | ((jj <= cur) & (jj >= cur - 1))
    work = jnp.where(valid, jnp.where(forced, jnp.inf, imp), _NEG_INF)
    blk = jj.astype(_F32)
    sel = jnp.zeros((N_SLC_PAD, t), _F32)
    for _ in range(SLC_TOPN):
        mx = jnp.max(work, axis=0, keepdims=True)
        first = jnp.min(jnp.where(work == mx, blk, float(N_SLC_PAD)), axis=0, keepdims=True)
        hit = blk == first
        sel = jnp.where(hit & (mx > _NEG_INF), 1.0, sel)
        work = jnp.where(hit, _NEG_INF, work)
    sel_ref[0] = sel


def _nsa_cmp(proj, kc_pad, vc_pad, a_far, tc):
    s = proj.shape[0]
    t = ATT_TILE
    gw = NSA_HPG * DKP
    return pl.pallas_call(
        _nsa_cmp_kernel,
        out_shape=(
            jax.ShapeDtypeStruct((NSA_HEADS * NSA_DV, s), _BF),
            jax.ShapeDtypeStruct((NSA_GROUPS, N_SLC_PAD, s), _F32),
        ),
        grid=(NSA_GROUPS, s // t),
        in_specs=[
            pl.BlockSpec((t, gw), lambda g, i: (i, COL_QN // gw + g)),
            pl.BlockSpec((1,) + kc_pad.shape[1:], lambda g, i: (g, 0, 0)),
            pl.BlockSpec((1,) + vc_pad.shape[1:], lambda g, i: (g, 0, 0)),
            pl.BlockSpec((1,) + a_far.shape[1:], lambda g, i: (g, 0, 0)),
            pl.BlockSpec((NSA_HPG, 1, CMP_PAD, t), lambda g, i: (g, 0, 0, 0)),
        ],
        out_specs=(
            pl.BlockSpec((NSA_HPG * NSA_DV, t), lambda g, i: (g, i)),
            pl.BlockSpec((1, N_SLC_PAD, t), lambda g, i: (g, 0, i)),
        ),
        compiler_params=_params(("parallel", "arbitrary")),
        name="nsa_cmp",
    )(proj, kc_pad, vc_pad, a_far, tc)


def _nsa_main_kernel(q_ref, ks_ref, vs_ref, kw_ref, vw_ref, selt_ref, tw_ref, oc_ref, gate_ref,
                     o_ref, m_ref, l_ref, acc_ref, *, n_sub):
    t = q_ref.shape[0]

    def value_tile(ref, tile):
        return ref[pl.ds(pl.multiple_of(tile * t, t), t), :]

    tk = n_sub * t
    g = pl.program_id(0)
    i = pl.program_id(1)
    blocks_per_tile = t // SLC_LEN
    qs = [q_ref[:, h * DKP:(h + 1) * DKP] for h in range(NSA_HPG)]

    m_ref[...] = jnp.full(m_ref.shape, -1e30, _F32)
    l_ref[...] = jnp.zeros(l_ref.shape, _F32)
    acc_ref[...] = jnp.zeros(acc_ref.shape, _F32)

    def selected(tile0, n_tiles):
        rows = [jnp.broadcast_to(selt_ref[0, pl.ds(tile0 * blocks_per_tile + b, 1), :], (SLC_LEN, t))
                for b in range(n_tiles * blocks_per_tile)]
        return jnp.concatenate(rows, axis=0) > 0.5

    def run_heads(k, vts, mask, bias):
        def logits(h):
            st = _dot_nt(k, qs[h])
            if bias is not None:
                st = st + bias(h)
            return jnp.where(mask, st, _NEG_INF)

        st = logits(0)
        for h in range(NSA_HPG):
            st_next = logits(h + 1) if h + 1 < NSA_HPG else None
            m, l, acc = _online_update_t(st, vts, m_ref[h], l_ref[h], acc_ref[h])
            m_ref[h] = m
            l_ref[h] = l
            acc_ref[h] = acc
            st = st_next

    def far_chunk(tile0, n_tiles, first_new):
        r0 = pl.multiple_of(tile0 * t, t)
        mask = selected(tile0, n_tiles)
        if first_new is not None:
            kpos = r0 + lax.broadcasted_iota(jnp.int32, (n_tiles * t, t), 0)
            mask = mask & (kpos >= first_new) & (kpos < n_far * t)
        run_heads(ks_ref[pl.ds(r0, n_tiles * t), :],
                  [value_tile(vs_ref, tile0 + a) for a in range(n_tiles)], mask, None)

    n_far = jnp.maximum(i - 1, 0)
    n_full = n_far // n_sub
    n_left = n_far - n_full * n_sub

    def far_body(c, carry):
        far_chunk(c * n_sub, n_sub, None)
        return carry

    lax.fori_loop(0, n_full, far_body, 0)

    half = n_sub // 2

    @pl.when((n_left > 0) & (n_left <= half))
    def _():
        far_chunk(jnp.maximum(n_far - half, 0), half, n_full * tk)

    @pl.when(n_left > half)
    def _():
        far_chunk(jnp.maximum(n_far - n_sub, 0), n_sub, n_full * tk)

    prev = jnp.maximum(i - 1, 0)
    row2 = lax.broadcasted_iota(jnp.int32, (2 * t, t), 0)
    col2 = lax.broadcasted_iota(jnp.int32, (2 * t, t), 1)
    near_mask = (jnp.concatenate([selected(prev, 1), selected(i, 1)], axis=0)
                 & (col2 - row2 + t >= 0) & (row2 >= jnp.where(i >= 1, 0, t)))
    k_near = jnp.concatenate([ks_ref[pl.ds(pl.multiple_of(prev * t, t), t), :],
                              ks_ref[pl.ds(pl.multiple_of(i * t, t), t), :]], axis=0)
    run_heads(k_near, [value_tile(vs_ref, prev), value_tile(vs_ref, i)], near_mask,
              lambda h: tw_ref[h, 0, t:, :])

    prev2 = jnp.maximum(i - 2, 0)
    w_tiles = (prev2, prev, i)
    k_win = jnp.concatenate([kw_ref[pl.ds(pl.multiple_of(a * t, t), t), :] for a in w_tiles], axis=0)
    row3 = lax.broadcasted_iota(jnp.int32, (3 * t, t), 0)
    col3 = lax.broadcasted_iota(jnp.int32, (3 * t, t), 1)
    dist = col3 - row3 + 2 * t
    first_row = jnp.where(i >= 2, 0, jnp.where(i >= 1, t, 2 * t))
    win_mask = (dist >= 0) & (dist < WINDOW) & (row3 >= first_row)

    def window_logits(h):
        return jnp.where(win_mask, _dot_nt(k_win, qs[h]) + tw_ref[h, 0], _NEG_INF)

    gates_t = jax.nn.sigmoid(gate_ref[...].astype(_F32)).T
    st_next = window_logits(0)
    for h in range(NSA_HPG):
        st = st_next
        if h + 1 < NSA_HPG:
            st_next = window_logits(h + 1)
        m = jnp.max(st, axis=0, keepdims=True)
        pt = jnp.exp2(st - m)
        den = jnp.sum(pt, axis=0, keepdims=True)
        pb = pt.astype(_BF)
        o_w = functools.reduce(jnp.add, [_dot_tn(value_tile(vw_ref, a), pb[n * t:(n + 1) * t, :])
                                         for n, a in enumerate(w_tiles)]) * (1.0 / den)
        o_s = acc_ref[h] * (1.0 / l_ref[h])
        o_c = oc_ref[h * NSA_DV:(h + 1) * NSA_DV, :].astype(_F32)
        gc, gs, gw = [jnp.where(g == 0, gates_t[h * N_BRANCH + b:h * N_BRANCH + b + 1, :],
                                gates_t[(NSA_HPG + h) * N_BRANCH + b:(NSA_HPG + h) * N_BRANCH + b + 1, :])
                      for b in range(N_BRANCH)]
        o_ref[:, h * NSA_DV:(h + 1) * NSA_DV] = (gc * o_c + gs * o_s + gw * o_w).T.astype(o_ref.dtype)


def _nsa_main(proj, sel_t, tw, oc_t, *, n_sub=4):
    s = proj.shape[0]
    t = ATT_TILE
    nb = s // t
    gw = NSA_HPG * DKP
    kn0 = COL_KN // DKP
    vn0 = COL_VN // NSA_DV
    return pl.pallas_call(
        functools.partial(_nsa_main_kernel, n_sub=n_sub),
        out_shape=jax.ShapeDtypeStruct((s, NSA_HEADS * NSA_DV), _BF),
        grid=(NSA_GROUPS, nb),
        in_specs=[
            pl.BlockSpec((t, gw), lambda g, i: (i, COL_QN // gw + g)),
            pl.BlockSpec((s, DKP), lambda g, i: (0, kn0 + 1 * NSA_GROUPS + g)),
            pl.BlockSpec((s, NSA_DV), lambda g, i: (0, vn0 + 1 * NSA_GROUPS + g)),
            pl.BlockSpec((s, DKP), lambda g, i: (0, kn0 + 2 * NSA_GROUPS + g)),
            pl.BlockSpec((s, NSA_DV), lambda g, i: (0, vn0 + 2 * NSA_GROUPS + g)),
            pl.BlockSpec((1, N_SLC_PAD, t), lambda g, i: (g, 0, i)),
            pl.BlockSpec((NSA_HPG, 1, 3 * t, t), lambda g, i: (g, 0, 0, 0)),
            pl.BlockSpec((NSA_HPG * NSA_DV, t), lambda g, i: (g, i)),
            pl.BlockSpec((t, LANES), lambda g, i: (i, COL_GATE // LANES)),
        ],
        out_specs=pl.BlockSpec((t, NSA_HPG * NSA_DV), lambda g, i: (i, g)),
        scratch_shapes=[
            pltpu.VMEM((NSA_HPG, 1, t), _F32),
            pltpu.VMEM((NSA_HPG, 1, t), _F32),
            pltpu.VMEM((NSA_HPG, NSA_DV, t), _F32),
        ],
        compiler_params=_params(("parallel", "arbitrary")),
        name="nsa_main",
    )(proj, proj, proj, proj, proj, sel_t, tw, oc_t, proj)


def _outproj_kernel(x_ref, mod_ref, om_ref, on_ref, w_ref, o_ref, *, mod_row):
    half = om_ref.shape[1]
    y = _dot(om_ref[...], w_ref[:half, :]) + _dot(on_ref[...], w_ref[half:, :])
    o_ref[...] = x_ref[...] + mod_ref[mod_row:mod_row + 1, :] * y


def _outproj(x, mod, o_mla, o_nsa, w_out, *, mod_row, tm=512):
    s = x.shape[0]
    return pl.pallas_call(
        functools.partial(_outproj_kernel, mod_row=mod_row),
        out_shape=jax.ShapeDtypeStruct((s, D_MODEL), _F32),
        grid=(s // tm,),
        in_specs=[
            pl.BlockSpec((tm, D_MODEL), lambda i: (i, 0)),
            pl.BlockSpec((9, D_MODEL), lambda i: (0, 0)),
            pl.BlockSpec((tm, o_mla.shape[1]), lambda i: (i, 0)),
            pl.BlockSpec((tm, o_nsa.shape[1]), lambda i: (i, 0)),
            pl.BlockSpec(w_out.shape, lambda i: (0, 0)),
        ],
        out_specs=pl.BlockSpec((tm, D_MODEL), lambda i: (i, 0)),
        compiler_params=_params(("parallel",)),
        name="outproj",
    )(x, mod, o_mla, o_nsa, w_out)


def _pad_cols(w, width):
    return jnp.pad(w, ((0, 0), (0, width - w.shape[1])))


def _pad_last(w, width):
    return jnp.pad(w, [(0, 0)] * (w.ndim - 1) + [(0, width - w.shape[-1])])


def _rotate_half_cols(w):
    half = MLA_ROPE // 2
    return jnp.concatenate([-w[..., half:], w[..., :half]], axis=-1)


def _w_in_padded(w_in):
    d = w_in.shape[0]
    w_in = w_in.astype(_BF)
    sizes = (MLA_Q_RANK, MLA_KV_RANK, MLA_ROPE, NSA_HEADS * NSA_DK,
             N_BRANCH * NSA_GROUPS * NSA_DK, N_BRANCH * NSA_GROUPS * NSA_DV, NSA_HEADS * N_BRANCH)
    offs = np.concatenate([[0], np.cumsum(sizes)])
    cq, ckv, kr, qn, kn, vn, gn = [w_in[:, offs[a]:offs[a + 1]] for a in range(len(sizes))]
    qn = _pad_last(qn.reshape(d, NSA_HEADS, NSA_DK), DKP).reshape(d, NSA_HEADS * DKP)
    kn = _pad_last(kn.reshape(d, N_BRANCH * NSA_GROUPS, NSA_DK), DKP).reshape(d, -1)
    cols = [cq, _pad_cols(gn, LANES), _pad_cols(kr, LANES), ckv,
            _pad_cols(_rotate_half_cols(kr), LANES),
            jnp.zeros((d, COL_QN - COL_KR_ROT - LANES), w_in.dtype), qn, kn, vn]
    w = jnp.concatenate(cols, axis=1)
    assert w.shape[1] == D_IN_PAD
    colscale = np.ones((1, D_IN_PAD), np.float32)
    colscale[:, COL_QN:COL_KN] = NSA_DK ** -0.5 * LOG2_E
    return w, jnp.asarray(colscale)


def _w_uq_padded(w_uq):
    r = w_uq.shape[0]
    w = w_uq.reshape(r, MLA_HEADS, MLA_NOPE + MLA_ROPE)
    nope = w[..., :MLA_NOPE].reshape(r, -1)
    rope = w[..., MLA_NOPE:]
    rope_p = _pad_last(rope, LANES).reshape(r, -1)
    rot_p = _pad_last(_rotate_half_cols(rope), LANES).reshape(r, -1)
    return jnp.concatenate([nope, rope_p, rot_p], axis=1).astype(_BF)


def _cmp_w1_padded(w1, d, dpad):
    return _pad_last(w1.reshape(CMP_LEN, d, CMP_HIDDEN).transpose(0, 2, 1), dpad) \
        .transpose(0, 2, 1).reshape(CMP_LEN * dpad, CMP_HIDDEN).astype(_BF)


def _rope_tables(s):
    inv = ROPE_THETA ** (-jnp.arange(0, MLA_ROPE, 2, dtype=_F32) / MLA_ROPE)
    ang = jnp.arange(s, dtype=_F32)[:, None] * inv[None, :]
    cos = _pad_cols(jnp.tile(jnp.cos(ang), (1, 2)), LANES)
    sin = _pad_cols(jnp.tile(jnp.sin(ang), (1, 2)), LANES)
    return cos, sin


def _overlap_t(s):
    s_start = np.arange(N_SLC_PAD)[:, None] * SLC_LEN
    c_start = np.arange(s // CMP_STRIDE)[None, :] * CMP_STRIDE
    ov = (c_start < s_start + SLC_LEN) & (c_start + CMP_LEN > s_start)
    return jnp.asarray(ov.astype(np.float32), _BF)


def _mixer_heads(x, mod, norm_mix, w_in, mla_q_norm, w_uq, mla_kv_norm, w_ukv, pe_cmp_k, w_cmp_k1,
                 w_cmp_k2, pe_cmp_v, w_cmp_v1, w_cmp_v2, rel_bias):
    s = x.shape[0]
    assert s % ATT_TILE == 0 and s // SLC_LEN <= N_SLC_PAD and s >= 3 * ATT_TILE
    w_pad, colscale = _w_in_padded(w_in)
    proj = _inproj(x, mod, norm_mix.reshape(1, -1), w_pad, colscale, mod_row=3)

    cos, sin = _rope_tables(s)
    q, k, v = _mla_prep(proj, mla_q_norm.reshape(1, -1), mla_kv_norm.reshape(1, -1),
                        _w_uq_padded(w_uq), w_ukv.astype(_BF), cos, sin)
    o_mla = _mla_attn(q, k, v)

    n_chunks = s // CMP_STRIDE
    kn0 = COL_KN
    k_cmp = jnp.stack([proj[:, kn0 + g * DKP:kn0 + (g + 1) * DKP] for g in range(NSA_GROUPS)])
    v_cmp = jnp.stack([proj[:, COL_VN + g * NSA_DV:COL_VN + (g + 1) * NSA_DV] for g in range(NSA_GROUPS)])
    pe_k = jnp.broadcast_to(_pad_last(pe_cmp_k, DKP).reshape(1, -1), (8, CMP_LEN * DKP)).astype(_BF)
    pe_v = jnp.broadcast_to(pe_cmp_v.reshape(1, -1), (8, CMP_LEN * NSA_DV)).astype(_BF)
    kc = _compress(k_cmp.reshape(NSA_GROUPS, n_chunks, CMP_STRIDE * DKP), pe_k,
                   _cmp_w1_padded(w_cmp_k1, NSA_DK, DKP), _pad_cols(w_cmp_k2, DKP).astype(_BF))
    vc = _compress(v_cmp.reshape(NSA_GROUPS, n_chunks, CMP_STRIDE * NSA_DV), pe_v,
                   w_cmp_v1.astype(_BF), w_cmp_v2.astype(_BF))
    tc = _bias_tables(rel_bias, _cmp_buckets(ATT_TILE))
    tw = _bias_tables(rel_bias, _window_buckets(ATT_TILE))
    ov_t = _overlap_t(s)
    a_far = jnp.stack([jnp.concatenate([vc[g, CMP_PAD:].T, ov_t], axis=0) for g in range(NSA_GROUPS)])
    oc_t, sel_t = _nsa_cmp(proj, kc, vc, a_far, tc)
    o_nsa = _nsa_main(proj, sel_t, tw, oc_t)
    return o_mla, o_nsa


def kernel(x, c, w_ada, b_ada, norm_ffn1, w1_gate, w1_up, w1_down, norm_mix, w_in, mla_q_norm, w_uq, mla_kv_norm, w_ukv, pe_cmp_k, w_cmp_k1, w_cmp_k2, pe_cmp_v, w_cmp_v1, w_cmp_v2, rel_bias, w_out, norm_ffn2, w2_gate, w2_up, w2_down, norm_final):
    assert x.shape[0] == 1 and w_ada.shape[0] == 1
    xs = x[0]
    fin = norm_final.reshape(1, -1)
    mod = _ada_mod(c, w_ada[0], b_ada[0]).reshape(9, D_MODEL)
    xs = _ffn(xs, mod, norm_ffn1[0].reshape(1, -1), w1_gate[0], w1_up[0], w1_down[0], fin,
              mod_row=0, final_norm=False)
    o_mla, o_nsa = _mixer_heads(xs, mod, norm_mix[0], w_in[0], mla_q_norm[0], w_uq[0], mla_kv_norm[0],
                                w_ukv[0], pe_cmp_k[0], w_cmp_k1[0], w_cmp_k2[0], pe_cmp_v[0],
                                w_cmp_v1[0], w_cmp_v2[0], rel_bias)
    xs = _outproj(xs, mod, o_mla, o_nsa, w_out[0].astype(_BF), mod_row=5)
    xs = _ffn(xs, mod, norm_ffn2[0].reshape(1, -1), w2_gate[0], w2_up[0], w2_down[0], fin,
              mod_row=6, final_norm=True)
    return xs[None]
```

```python
import functools
import math

import numpy as np
import jax
import jax.numpy as jnp
from jax import lax
from jax.experimental import pallas as pl
from jax.experimental.pallas import tpu as pltpu

D_MODEL = 2048
D_FF = 5632
EPS = 1e-6
MLA_HEADS = 8
MLA_Q_RANK = 768
MLA_KV_RANK = 512
MLA_NOPE = 128
MLA_ROPE = 64
MLA_V = 128
ROPE_THETA = 10000.0
NSA_HEADS = 8
NSA_GROUPS = 2
NSA_HPG = NSA_HEADS // NSA_GROUPS
NSA_DK = 192
NSA_DV = 128
CMP_LEN = 32
CMP_STRIDE = 16
CMP_HIDDEN = 256
SLC_LEN = 64
SLC_SHIFT = 6
SLC_TOPN = 16
WINDOW = 512
N_BRANCH = 3
REL_BUCKETS = 32
REL_MAX_DIST = 128

LANES = 128
SUBLANES = 8
VMEM_LIMIT = 60000 * 1024

DKP = 256
ATT_TILE = 256
MLA_TILE = 512
CMP_TILE = 512
CMP_PAD = 128
N_SLC_PAD = 128

_BF = jnp.bfloat16
_F32 = jnp.float32
_NEG_INF = float("-inf")
LOG2_E = math.log2(math.e)

COL_CQ = 0
COL_GATE = 768
COL_KR = 896
COL_CKV = 1024
COL_KR_ROT = 1536
COL_QN = 2048
COL_KN = COL_QN + NSA_HEADS * DKP
COL_VN = COL_KN + N_BRANCH * NSA_GROUPS * DKP
D_IN_PAD = COL_VN + N_BRANCH * NSA_GROUPS * NSA_DV


def _dot(a, b):
    return jnp.dot(a, b, preferred_element_type=_F32)


def _dot_nt(a, b):
    return lax.dot_general(a, b, (((1,), (1,)), ((), ())), preferred_element_type=_F32)


def _rms(x, gain):
    return x * lax.rsqrt(jnp.mean(x * x, axis=-1, keepdims=True) + EPS) * gain


def _params(semantics):
    return pltpu.CompilerParams(dimension_semantics=semantics, vmem_limit_bytes=VMEM_LIMIT)


ADA_ROWS = 256


def _ada_kernel(c_ref, w_ref, b_ref, o_ref):
    tn = o_ref.shape[1]

    def body(k, acc):
        r = pl.multiple_of(k * ADA_ROWS, ADA_ROWS)
        c = c_ref[pl.ds(r, ADA_ROWS), :]
        s = c * jax.nn.sigmoid(c)
        prod = w_ref[pl.ds(r, ADA_ROWS), :] * s
        return acc + jnp.sum(prod.reshape(ADA_ROWS // SUBLANES, SUBLANES, tn), axis=0)

    acc = lax.fori_loop(0, D_MODEL // ADA_ROWS, body, jnp.zeros((SUBLANES, tn), _F32))
    o_ref[...] = jnp.sum(acc, axis=0, keepdims=True) + b_ref[...]


def _ada_mod(c, w_ada, b_ada):
    n = w_ada.shape[1]
    tn = 1024
    return pl.pallas_call(
        _ada_kernel,
        out_shape=jax.ShapeDtypeStruct((1, n), _F32),
        grid=(n // tn,),
        in_specs=[
            pl.BlockSpec((D_MODEL, 1), lambda j: (0, 0)),
            pl.BlockSpec((D_MODEL, tn), lambda j: (0, j)),
            pl.BlockSpec((1, tn), lambda j: (0, j)),
        ],
        out_specs=pl.BlockSpec((1, tn), lambda j: (0, j)),
        compiler_params=_params(("arbitrary",)),
        name="ada_mod",
    )(c.reshape(D_MODEL, 1), w_ada, b_ada.reshape(1, n))


def _ffn_kernel(x_ref, mod_ref, gain_ref, wg_ref, wu_ref, wd_ref, fin_ref, o_ref, h_ref,
                *, mod_row, final_norm):
    j = pl.program_id(1)

    @pl.when(j == 0)
    def _():
        y = _rms(x_ref[...], gain_ref[...])
        h = y * (1.0 + mod_ref[mod_row + 1:mod_row + 2, :]) + mod_ref[mod_row:mod_row + 1, :]
        h_ref[...] = h.astype(_BF)
        o_ref[...] = jnp.zeros_like(o_ref)

    h = h_ref[...]
    g = _dot(h, wg_ref[...].astype(_BF))
    u = _dot(h, wu_ref[...].astype(_BF))
    a = (g * jax.nn.sigmoid(g)) * u
    o_ref[...] += _dot(a.astype(_BF), wd_ref[...].astype(_BF))

    @pl.when(j == pl.num_programs(1) - 1)
    def _():
        x2 = x_ref[...] + (0.5 * mod_ref[mod_row + 2:mod_row + 3, :]) * o_ref[...]
        if final_norm:
            x2 = _rms(x2, fin_ref[...])
        o_ref[...] = x2


def _ffn(x, mod, gain, wg, wu, wd, fin, *, mod_row, final_norm, tm=1024, tf=256):
    s = x.shape[0]
    tm = min(tm, s)
    kern = functools.partial(_ffn_kernel, mod_row=mod_row, final_norm=final_norm)
    return pl.pallas_call(
        kern,
        out_shape=jax.ShapeDtypeStruct((s, D_MODEL), _F32),
        grid=(s // tm, D_FF // tf),
        in_specs=[
            pl.BlockSpec((tm, D_MODEL), lambda i, j: (i, 0)),
            pl.BlockSpec((9, D_MODEL), lambda i, j: (0, 0)),
            pl.BlockSpec((1, D_MODEL), lambda i, j: (0, 0)),
            pl.BlockSpec((D_MODEL, tf), lambda i, j: (0, j)),
            pl.BlockSpec((D_MODEL, tf), lambda i, j: (0, j)),
            pl.BlockSpec((tf, D_MODEL), lambda i, j: (j, 0)),
            pl.BlockSpec((1, D_MODEL), lambda i, j: (0, 0)),
        ],
        out_specs=pl.BlockSpec((tm, D_MODEL), lambda i, j: (i, 0)),
        scratch_shapes=[pltpu.VMEM((tm, D_MODEL), _BF)],
        compiler_params=_params(("parallel", "arbitrary")),
        name="ffn_final" if final_norm else "ffn",
    )(x, mod, gain, wg, wu, wd, fin)


def _inproj_kernel(x_ref, mod_ref, gain_ref, w_ref, cs_ref, o_ref, h_ref, *, mod_row):
    @pl.when(pl.program_id(1) == 0)
    def _():
        y = _rms(x_ref[...], gain_ref[...])
        h = y * (1.0 + mod_ref[mod_row + 1:mod_row + 2, :]) + mod_ref[mod_row:mod_row + 1, :]
        h_ref[...] = h.astype(_BF)

    o_ref[...] = (_dot(h_ref[...], w_ref[...]) * cs_ref[...]).astype(o_ref.dtype)


def _inproj(x, mod, gain, w_pad, colscale, *, mod_row, tm=1024, tn=1280):
    s = x.shape[0]
    tm = min(tm, s)
    n = w_pad.shape[1]
    return pl.pallas_call(
        functools.partial(_inproj_kernel, mod_row=mod_row),
        out_shape=jax.ShapeDtypeStruct((s, n), _BF),
        grid=(s // tm, n // tn),
        in_specs=[
            pl.BlockSpec((tm, D_MODEL), lambda i, j: (i, 0)),
            pl.BlockSpec((9, D_MODEL), lambda i, j: (0, 0)),
            pl.BlockSpec((1, D_MODEL), lambda i, j: (0, 0)),
            pl.BlockSpec((D_MODEL, tn), lambda i, j: (0, j)),
            pl.BlockSpec((1, tn), lambda i, j: (0, j)),
        ],
        out_specs=pl.BlockSpec((tm, tn), lambda i, j: (i, j)),
        scratch_shapes=[pltpu.VMEM((tm, D_MODEL), _BF)],
        compiler_params=_params(("parallel", "arbitrary")),
        name="inproj",
    )(x, mod, gain, w_pad, colscale)


def _mla_prep_kernel(cq_ref, ckv_ref, kr_ref, krr_ref, gq_ref, gkv_ref, wq_ref, wkv_ref,
                     cos_ref, sin_ref, q_ref, k_ref, v_ref):
    scale = (MLA_NOPE + MLA_ROPE) ** -0.5 * LOG2_E
    cos = cos_ref[...]
    sin = sin_ref[...]
    hq = _rms(cq_ref[...].astype(_F32), gq_ref[...]).astype(_BF)
    qa = _dot(hq, wq_ref[...])
    nh = MLA_HEADS * LANES
    for h in range(MLA_HEADS):
        lo = h * LANES
        nope = qa[:, lo:lo + LANES]
        a = qa[:, nh + lo:nh + lo + LANES]
        b = qa[:, 2 * nh + lo:2 * nh + lo + LANES]
        q_ref[:, h * DKP:h * DKP + LANES] = (nope * scale).astype(_BF)
        q_ref[:, h * DKP + LANES:(h + 1) * DKP] = ((a * cos + b * sin) * scale).astype(_BF)
    hkv = _rms(ckv_ref[...].astype(_F32), gkv_ref[...]).astype(_BF)
    kv = _dot(hkv, wkv_ref[...])
    k_rope = (kr_ref[...].astype(_F32) * cos + krr_ref[...].astype(_F32) * sin).astype(_BF)
    for h in range(MLA_HEADS):
        lo = h * (MLA_NOPE + MLA_V)
        k_ref[:, h * DKP:h * DKP + LANES] = kv[:, lo:lo + MLA_NOPE].astype(_BF)
        k_ref[:, h * DKP + LANES:(h + 1) * DKP] = k_rope
        v_ref[:, h * MLA_V:(h + 1) * MLA_V] = kv[:, lo + MLA_NOPE:lo + MLA_NOPE + MLA_V].astype(_BF)


def _mla_prep(proj, gq, gkv, wq, wkv, cos, sin, *, tm=512):
    s = proj.shape[0]
    const = lambda i: (0, 0)
    return pl.pallas_call(
        _mla_prep_kernel,
        out_shape=(
            jax.ShapeDtypeStruct((s, MLA_HEADS * DKP), _BF),
            jax.ShapeDtypeStruct((s, MLA_HEADS * DKP), _BF),
            jax.ShapeDtypeStruct((s, MLA_HEADS * MLA_V), _BF),
        ),
        grid=(s // tm,),
        in_specs=[
            pl.BlockSpec((tm, MLA_Q_RANK), lambda i: (i, COL_CQ // MLA_Q_RANK)),
            pl.BlockSpec((tm, MLA_KV_RANK), lambda i: (i, COL_CKV // MLA_KV_RANK)),
            pl.BlockSpec((tm, LANES), lambda i: (i, COL_KR // LANES)),
            pl.BlockSpec((tm, LANES), lambda i: (i, COL_KR_ROT // LANES)),
            pl.BlockSpec((1, MLA_Q_RANK), const),
            pl.BlockSpec((1, MLA_KV_RANK), const),
            pl.BlockSpec(wq.shape, const),
            pl.BlockSpec(wkv.shape, const),
            pl.BlockSpec((tm, LANES), lambda i: (i, 0)),
            pl.BlockSpec((tm, LANES), lambda i: (i, 0)),
        ],
        out_specs=(
            pl.BlockSpec((tm, MLA_HEADS * DKP), lambda i: (i, 0)),
            pl.BlockSpec((tm, MLA_HEADS * DKP), lambda i: (i, 0)),
            pl.BlockSpec((tm, MLA_HEADS * MLA_V), lambda i: (i, 0)),
        ),
        compiler_params=_params(("parallel",)),
        name="mla_prep",
    )(proj, proj, proj, proj, gq, gkv, wq, wkv, cos, sin)


def _dot_tn(a, b):
    return lax.dot_general(a, b, (((0,), (0,)), ((), ())), preferred_element_type=_F32)


def _online_update_t(st, vs, m, l, acc):
    t = vs[0].shape[0]
    m_new = jnp.maximum(m, jnp.max(st, axis=0, keepdims=True))
    alpha = jnp.exp2(m - m_new)
    pt = jnp.exp2(st - m_new)
    l = alpha * l + jnp.sum(pt, axis=0, keepdims=True)
    pb = pt.astype(_BF)
    pv = functools.reduce(jnp.add, [_dot_tn(v, pb[a * t:(a + 1) * t, :]) for a, v in enumerate(vs)])
    return m_new, l, alpha * acc + pv


def _mla_attn_kernel(q_ref, k_ref, v_ref, o_ref, m_ref, l_ref, acc_ref, *, n_sub):
    t = q_ref.shape[0]
    tk = n_sub * t
    hp = q_ref.shape[1] // DKP
    i = pl.program_id(1)
    qs = [q_ref[:, h * DKP:(h + 1) * DKP] for h in range(hp)]
    n_full = i // n_sub
    tail_b0 = jnp.maximum(i + 1 - n_sub, 0)

    m_ref[...] = jnp.full(m_ref.shape, -1e30, _F32)
    l_ref[...] = jnp.zeros(l_ref.shape, _F32)
    acc_ref[...] = jnp.zeros(acc_ref.shape, _F32)

    def first_tile(c):
        return jnp.where(c < n_full, c * n_sub, tail_b0)

    def logits(c, h):
        r0 = pl.multiple_of(first_tile(c) * t, t)
        return _dot_nt(k_ref[pl.ds(r0, tk), h * DKP:(h + 1) * DKP], qs[h])

    def update(c, h, st):
        b0 = first_tile(c)
        vs = [v_ref[pl.ds(pl.multiple_of((b0 + a) * t, t), t), h * MLA_V:(h + 1) * MLA_V] for a in range(n_sub)]
        m, l, acc = _online_update_t(st, vs, m_ref[h], l_ref[h], acc_ref[h])
        m_ref[h] = m
        l_ref[h] = l
        acc_ref[h] = acc

    def body(c, st):
        for h in range(hp):
            st_next = logits(c, h + 1) if h + 1 < hp else logits(c + 1, 0)
            update(c, h, st)
            st = st_next
        return st

    st = lax.fori_loop(0, n_full, body, logits(0, 0))
    kpos = tail_b0 * t + lax.broadcasted_iota(jnp.int32, (tk, t), 0)
    qpos = i * t + lax.broadcasted_iota(jnp.int32, (tk, t), 1)
    tail_mask = (kpos >= n_full * tk) & (kpos <= qpos)
    for h in range(hp):
        st_next = logits(n_full, h + 1) if h + 1 < hp else None
        update(n_full, h, jnp.where(tail_mask, st, _NEG_INF))
        st = st_next
    for h in range(hp):
        o_ref[:, h * MLA_V:(h + 1) * MLA_V] = (acc_ref[h] * (1.0 / l_ref[h])).T.astype(o_ref.dtype)


def _mla_attn(q, k, v, *, n_sub=2, hp=4):
    s = q.shape[0]
    t = min(MLA_TILE, s // n_sub)
    nb = s // t
    return pl.pallas_call(
        functools.partial(_mla_attn_kernel, n_sub=n_sub),
        out_shape=jax.ShapeDtypeStruct((s, MLA_HEADS * MLA_V), _BF),
        grid=(MLA_HEADS // hp, nb),
        in_specs=[
            pl.BlockSpec((t, hp * DKP), lambda h, i: (i, h)),
            pl.BlockSpec((s, hp * DKP), lambda h, i: (0, h), pipeline_mode=pl.Buffered(1)),
            pl.BlockSpec((s, hp * MLA_V), lambda h, i: (0, h), pipeline_mode=pl.Buffered(1)),
        ],
        out_specs=pl.BlockSpec((t, hp * MLA_V), lambda h, i: (i, h)),
        scratch_shapes=[
            pltpu.VMEM((hp, 1, t), _F32),
            pltpu.VMEM((hp, 1, t), _F32),
            pltpu.VMEM((hp, MLA_V, t), _F32),
        ],
        compiler_params=_params(("parallel", "arbitrary")),
        name="mla_attn",
    )(q, k, v)


def _t5_bucket_np(dist):
    n = np.maximum(dist, 0)
    max_exact = REL_BUCKETS // 2
    nf = np.maximum(n, 1).astype(np.float32)
    large = max_exact + (np.log(nf / np.float32(max_exact)) / np.float32(math.log(REL_MAX_DIST / max_exact))
                         * np.float32(REL_BUCKETS - max_exact)).astype(np.int32)
    large = np.minimum(large, REL_BUCKETS - 1)
    return np.where(n < max_exact, n, large).astype(np.int32)


def _bias_table_kernel(rb_ref, bk_ref, o_ref):
    h = pl.program_id(0)
    bk = bk_ref[0]
    base = rb_ref[REL_BUCKETS - 1, h]
    acc = jnp.zeros(bk.shape, _F32)
    for b in range(REL_BUCKETS - 1):
        acc = jnp.where(bk == b, (rb_ref[b, h] - base) * LOG2_E, acc)
    o_ref[0, 0] = acc


def _bias_tables(rel_bias, buckets):
    nt, r, c = buckets.shape
    return pl.pallas_call(
        _bias_table_kernel,
        out_shape=jax.ShapeDtypeStruct((NSA_HEADS, nt, r, c), _F32),
        grid=(NSA_HEADS, nt),
        in_specs=[
            pl.BlockSpec(memory_space=pltpu.SMEM),
            pl.BlockSpec((1, r, c), lambda h, t: (t, 0, 0)),
        ],
        out_specs=pl.BlockSpec((1, 1, r, c), lambda h, t: (h, t, 0, 0)),
        compiler_params=_params(("arbitrary", "arbitrary")),
        name="bias_tables",
    )(rel_bias, jnp.asarray(buckets))


def _window_buckets(t):
    r = np.arange(3 * t)[:, None]
    c = np.arange(t)[None, :]
    return _t5_bucket_np(c - r + 2 * t)[None]


def _cmp_buckets(t):
    c = np.arange(CMP_PAD)[:, None]
    r = np.arange(t)[None, :]
    return _t5_bucket_np(r - (CMP_LEN - 1) - t + CMP_STRIDE * (CMP_PAD - c))[None]


def _compress_kernel(ch_ref, pe_ref, w1_ref, w2_ref, o_ref):
    half = w1_ref.shape[0] // 2
    ch = ch_ref[0]
    a = _dot(ch, w1_ref[:half, :])
    b = _dot(ch, w1_ref[half:, :])
    n = ch.shape[0]
    b_next = pltpu.roll(b, n - 1, axis=0)
    c0 = _dot(pe_ref[...], w1_ref[...])[0:1, :]
    hid = a + b_next + c0
    hid = hid * jax.nn.sigmoid(hid)
    o_ref[0, :CMP_PAD, :] = jnp.zeros((CMP_PAD, o_ref.shape[2]), o_ref.dtype)
    o_ref[0, CMP_PAD:, :] = _dot(hid.astype(_BF), w2_ref[...]).astype(o_ref.dtype)


def _compress(chunks, pe_flat, w1, w2):
    g, n, kd = chunks.shape
    d = w2.shape[1]
    return pl.pallas_call(
        _compress_kernel,
        out_shape=jax.ShapeDtypeStruct((g, CMP_PAD + n, d), _BF),
        grid=(g,),
        in_specs=[
            pl.BlockSpec((1, n, kd), lambda i: (i, 0, 0)),
            pl.BlockSpec(pe_flat.shape, lambda i: (0, 0)),
            pl.BlockSpec(w1.shape, lambda i: (0, 0)),
            pl.BlockSpec(w2.shape, lambda i: (0, 0)),
        ],
        out_specs=pl.BlockSpec((1, CMP_PAD + n, d), lambda i: (i, 0, 0)),
        compiler_params=_params(("arbitrary",)),
        name="nsa_compress",
    )(chunks, pe_flat, w1, w2)


def _nsa_cmp_kernel(q_ref, kc_ref, vc_ref, afar_ref, tc_ref, oc_ref, sel_ref):
    t = q_ref.shape[0]
    nc = kc_ref.shape[1] - CMP_PAD
    per_tile = t // CMP_STRIDE
    i = pl.program_id(1)
    thr = (i + 1) * per_tile - CMP_PAD
    near0 = pl.multiple_of((i + 1) * per_tile, per_tile)

    k_ext = jnp.concatenate([kc_ref[0, CMP_PAD:, :], kc_ref[0, pl.ds(near0, CMP_PAD), :]], axis=0)
    vcn_t = vc_ref[0, pl.ds(near0, CMP_PAD), :].astype(_F32).T.astype(_BF)
    jn = lax.broadcasted_iota(jnp.int32, (N_SLC_PAD, CMP_PAD), 0) * SLC_LEN
    cn = (thr + lax.broadcasted_iota(jnp.int32, (N_SLC_PAD, CMP_PAD), 1)) * CMP_STRIDE
    ovn_t = jnp.where((cn < jn + SLC_LEN) & (cn + CMP_LEN > jn), 1.0, 0.0).astype(_BF)
    a_near = jnp.concatenate([vcn_t, ovn_t], axis=0)
    a_far = afar_ref[0]

    far_mask = lax.broadcasted_iota(jnp.int32, (nc, t), 0) < thr
    c = lax.broadcasted_iota(jnp.int32, (CMP_PAD, t), 0)
    r = lax.broadcasted_iota(jnp.int32, (CMP_PAD, t), 1)
    near_mask = (CMP_STRIDE * c <= r - (CMP_LEN - 1) - t + CMP_STRIDE * CMP_PAD) & (c >= -thr)

    def logits(h):
        return _dot_nt(k_ext, q_ref[:, h * DKP:(h + 1) * DKP])

    imp = jnp.zeros((N_SLC_PAD, t), _F32)
    st_next = logits(0)
    for h in range(NSA_HPG):
        st = st_next
        if h + 1 < NSA_HPG:
            st_next = logits(h + 1)
        sf = jnp.where(far_mask, st[:nc], _NEG_INF)
        sn = jnp.where(near_mask, st[nc:] + tc_ref[h, 0], _NEG_INF)
        m = jnp.maximum(jnp.max(sf, axis=0, keepdims=True), jnp.max(sn, axis=0, keepdims=True))
        m = jnp.where(m == _NEG_INF, 0.0, m)
        pf = jnp.exp2(sf - m)
        pn = jnp.exp2(sn - m)
        den = jnp.sum(pf, axis=0, keepdims=True) + jnp.sum(pn, axis=0, keepdims=True)
        inv = 1.0 / jnp.maximum(den, 1e-30)
        pf = pf * inv
        pn = pn * inv
        pf_hi = pf.astype(_BF)
        pn_hi = pn.astype(_BF)
        res = _dot(a_far, pf_hi) + _dot(a_near, pn_hi)
        oc_ref[h * NSA_DV:(h + 1) * NSA_DV, :] = res[:NSA_DV].astype(oc_ref.dtype)
        pf_lo = (pf - pf_hi.astype(_F32)).astype(_BF)
        pn_lo = (pn - pn_hi.astype(_F32)).astype(_BF)
        imp = imp + res[NSA_DV:] + (_dot(a_far[NSA_DV:], pf_lo) + _dot(a_near[NSA_DV:], pn_lo))

    jj = lax.broadcasted_iota(jnp.int32, (N_SLC_PAD, t), 0)
    tq = i * t + lax.broadcasted_iota(jnp.int32, (N_SLC_PAD, t), 1)
    cur = tq >> SLC_SHIFT
    valid = jj * SLC_LEN <= tq
    forced = (jj == 0) | ((jj <= cur) & (jj >= cur - 1))
    work = jnp.where(valid, jnp.where(forced, jnp.inf, imp), _NEG_INF)
    blk = jj.astype(_F32)
    sel = jnp.zeros((N_SLC_PAD, t), _F32)
    for _ in range(SLC_TOPN):
        mx = jnp.max(work, axis=0, keepdims=True)
        first = jnp.min(jnp.where(work == mx, blk, float(N_SLC_PAD)), axis=0, keepdims=True)
        hit = blk == first
        sel = jnp.where(hit & (mx > _NEG_INF), 1.0, sel)
        work = jnp.where(hit, _NEG_INF, work)
    sel_ref[0] = sel


def _nsa_cmp(proj, kc_pad, vc_pad, a_far, tc):
    s = proj.shape[0]
    t = tc.shape[3]
    gw = NSA_HPG * DKP
    return pl.pallas_call(
        _nsa_cmp_kernel,
        out_shape=(
            jax.ShapeDtypeStruct((NSA_HEADS * NSA_DV, s), _BF),
            jax.ShapeDtypeStruct((NSA_GROUPS, N_SLC_PAD, s), _F32),
        ),
        grid=(NSA_GROUPS, s // t),
        in_specs=[
            pl.BlockSpec((t, gw), lambda g, i: (i, COL_QN // gw + g)),
            pl.BlockSpec((1,) + kc_pad.shape[1:], lambda g, i: (g, 0, 0)),
            pl.BlockSpec((1,) + vc_pad.shape[1:], lambda g, i: (g, 0, 0)),
            pl.BlockSpec((1,) + a_far.shape[1:], lambda g, i: (g, 0, 0)),
            pl.BlockSpec((NSA_HPG, 1, CMP_PAD, t), lambda g, i: (g, 0, 0, 0)),
        ],
        out_specs=(
            pl.BlockSpec((NSA_HPG * NSA_DV, t), lambda g, i: (g, i)),
            pl.BlockSpec((1, N_SLC_PAD, t), lambda g, i: (g, 0, i)),
        ),
        compiler_params=_params(("parallel", "arbitrary")),
        name="nsa_cmp",
    )(proj, kc_pad, vc_pad, a_far, tc)


def _nsa_main_kernel(q_ref, ks_ref, vs_ref, kw_ref, vw_ref, selt_ref, tw_ref, oc_ref, gate_ref,
                     o_ref, m_ref, l_ref, acc_ref, *, n_sub):
    t = q_ref.shape[0]

    def value_tile(ref, tile):
        return ref[pl.ds(pl.multiple_of(tile * t, t), t), :]

    tk = n_sub * t
    g = pl.program_id(0)
    i = pl.program_id(1)
    blocks_per_tile = t // SLC_LEN
    qs = [q_ref[:, h * DKP:(h + 1) * DKP] for h in range(NSA_HPG)]

    m_ref[...] = jnp.full(m_ref.shape, -1e30, _F32)
    l_ref[...] = jnp.zeros(l_ref.shape, _F32)
    acc_ref[...] = jnp.zeros(acc_ref.shape, _F32)

    def selected(tile0, n_tiles):
        rows = [jnp.broadcast_to(selt_ref[0, pl.ds(tile0 * blocks_per_tile + b, 1), :], (SLC_LEN, t))
                for b in range(n_tiles * blocks_per_tile)]
        return jnp.concatenate(rows, axis=0) > 0.5

    def run_heads(k, vts, mask, bias):
        def logits(h):
            st = _dot_nt(k, qs[h])
            if bias is not None:
                st = st + bias(h)
            return jnp.where(mask, st, _NEG_INF)

        st = logits(0)
        for h in range(NSA_HPG):
            st_next = logits(h + 1) if h + 1 < NSA_HPG else None
            m, l, acc = _online_update_t(st, vts, m_ref[h], l_ref[h], acc_ref[h])
            m_ref[h] = m
            l_ref[h] = l
            acc_ref[h] = acc
            st = st_next

    def far_chunk(tile0, n_tiles, first_new):
        r0 = pl.multiple_of(tile0 * t, t)
        mask = selected(tile0, n_tiles)
        if first_new is not None:
            kpos = r0 + lax.broadcasted_iota(jnp.int32, (n_tiles * t, t), 0)
            mask = mask & (kpos >= first_new) & (kpos < n_far * t)
        run_heads(ks_ref[pl.ds(r0, n_tiles * t), :],
                  [value_tile(vs_ref, tile0 + a) for a in range(n_tiles)], mask, None)

    n_far = jnp.maximum(i - 1, 0)
    n_full = n_far // n_sub
    n_left = n_far - n_full * n_sub

    def far_body(c, carry):
        far_chunk(c * n_sub, n_sub, None)
        return carry

    lax.fori_loop(0, n_full, far_body, 0)

    half = n_sub // 2

    @pl.when((n_left > 0) & (n_left <= half))
    def _():
        far_chunk(jnp.maximum(n_far - half, 0), half, n_full * tk)

    @pl.when(n_left > half)
    def _():
        far_chunk(jnp.maximum(n_far - n_sub, 0), n_sub, n_full * tk)

    prev = jnp.maximum(i - 1, 0)
    row2 = lax.broadcasted_iota(jnp.int32, (2 * t, t), 0)
    col2 = lax.broadcasted_iota(jnp.int32, (2 * t, t), 1)
    near_mask = (jnp.concatenate([selected(prev, 1), selected(i, 1)], axis=0)
                 & (col2 - row2 + t >= 0) & (row2 >= jnp.where(i >= 1, 0, t)))
    k_near = jnp.concatenate([ks_ref[pl.ds(pl.multiple_of(prev * t, t), t), :],
                              ks_ref[pl.ds(pl.multiple_of(i * t, t), t), :]], axis=0)
    run_heads(k_near, [value_tile(vs_ref, prev), value_tile(vs_ref, i)], near_mask,
              lambda h: tw_ref[h, 0, t:, :])

    prev2 = jnp.maximum(i - 2, 0)
    w_tiles = (prev2, prev, i)
    k_win = jnp.concatenate([kw_ref[pl.ds(pl.multiple_of(a * t, t), t), :] for a in w_tiles], axis=0)
    row3 = lax.broadcasted_iota(jnp.int32, (3 * t, t), 0)
    col3 = lax.broadcasted_iota(jnp.int32, (3 * t, t), 1)
    dist = col3 - row3 + 2 * t
    first_row = jnp.where(i >= 2, 0, jnp.where(i >= 1, t, 2 * t))
    win_mask = (dist >= 0) & (dist < WINDOW) & (row3 >= first_row)

    def window_logits(h):
        return jnp.where(win_mask, _dot_nt(k_win, qs[h]) + tw_ref[h, 0], _NEG_INF)

    gates_t = jax.nn.sigmoid(gate_ref[...].astype(_F32)).T
    st_next = window_logits(0)
    for h in range(NSA_HPG):
        st = st_next
        if h + 1 < NSA_HPG:
            st_next = window_logits(h + 1)
        m = jnp.max(st, axis=0, keepdims=True)
        pt = jnp.exp2(st - m)
        den = jnp.sum(pt, axis=0, keepdims=True)
        pb = pt.astype(_BF)
        o_w = functools.reduce(jnp.add, [_dot_tn(value_tile(vw_ref, a), pb[n * t:(n + 1) * t, :])
                                         for n, a in enumerate(w_tiles)]) * (1.0 / den)
        o_s = acc_ref[h] * (1.0 / l_ref[h])
        o_c = oc_ref[h * NSA_DV:(h + 1) * NSA_DV, :].astype(_F32)
        gc, gs, gw = [jnp.where(g == 0, gates_t[h * N_BRANCH + b:h * N_BRANCH + b + 1, :],
                                gates_t[(NSA_HPG + h) * N_BRANCH + b:(NSA_HPG + h) * N_BRANCH + b + 1, :])
                      for b in range(N_BRANCH)]
        o_ref[:, h * NSA_DV:(h + 1) * NSA_DV] = (gc * o_c + gs * o_s + gw * o_w).T.astype(o_ref.dtype)


def _nsa_main(proj, sel_t, tw, oc_t, *, n_sub=4):
    s = proj.shape[0]
    t = ATT_TILE
    nb = s // t
    gw = NSA_HPG * DKP
    kn0 = COL_KN // DKP
    vn0 = COL_VN // NSA_DV
    return pl.pallas_call(
        functools.partial(_nsa_main_kernel, n_sub=n_sub),
        out_shape=jax.ShapeDtypeStruct((s, NSA_HEADS * NSA_DV), _BF),
        grid=(NSA_GROUPS, nb),
        in_specs=[
            pl.BlockSpec((t, gw), lambda g, i: (i, COL_QN // gw + g)),
            pl.BlockSpec((s, DKP), lambda g, i: (0, kn0 + 1 * NSA_GROUPS + g)),
            pl.BlockSpec((s, NSA_DV), lambda g, i: (0, vn0 + 1 * NSA_GROUPS + g)),
            pl.BlockSpec((s, DKP), lambda g, i: (0, kn0 + 2 * NSA_GROUPS + g)),
            pl.BlockSpec((s, NSA_DV), lambda g, i: (0, vn0 + 2 * NSA_GROUPS + g)),
            pl.BlockSpec((1, N_SLC_PAD, t), lambda g, i: (g, 0, i)),
            pl.BlockSpec((NSA_HPG, 1, 3 * t, t), lambda g, i: (g, 0, 0, 0)),
            pl.BlockSpec((NSA_HPG * NSA_DV, t), lambda g, i: (g, i)),
            pl.BlockSpec((t, LANES), lambda g, i: (i, COL_GATE // LANES)),
        ],
        out_specs=pl.BlockSpec((t, NSA_HPG * NSA_DV), lambda g, i: (i, g)),
        scratch_shapes=[
            pltpu.VMEM((NSA_HPG, 1, t), _F32),
            pltpu.VMEM((NSA_HPG, 1, t), _F32),
            pltpu.VMEM((NSA_HPG, NSA_DV, t), _F32),
        ],
        compiler_params=_params(("parallel", "arbitrary")),
        name="nsa_main",
    )(proj, proj, proj, proj, proj, sel_t, tw, oc_t, proj)


def _outproj_kernel(x_ref, mod_ref, om_ref, on_ref, w_ref, o_ref, *, mod_row):
    half = om_ref.shape[1]
    y = _dot(om_ref[...], w_ref[:half, :]) + _dot(on_ref[...], w_ref[half:, :])
    o_ref[...] = x_ref[...] + mod_ref[mod_row:mod_row + 1, :] * y


def _outproj(x, mod, o_mla, o_nsa, w_out, *, mod_row, tm=512):
    s = x.shape[0]
    return pl.pallas_call(
        functools.partial(_outproj_kernel, mod_row=mod_row),
        out_shape=jax.ShapeDtypeStruct((s, D_MODEL), _F32),
        grid=(s // tm,),
        in_specs=[
            pl.BlockSpec((tm, D_MODEL), lambda i: (i, 0)),
            pl.BlockSpec((9, D_MODEL), lambda i: (0, 0)),
            pl.BlockSpec((tm, o_mla.shape[1]), lambda i: (i, 0)),
            pl.BlockSpec((tm, o_nsa.shape[1]), lambda i: (i, 0)),
            pl.BlockSpec(w_out.shape, lambda i: (0, 0)),
        ],
        out_specs=pl.BlockSpec((tm, D_MODEL), lambda i: (i, 0)),
        compiler_params=_params(("parallel",)),
        name="outproj",
    )(x, mod, o_mla, o_nsa, w_out)


def _pad_cols(w, width):
    return jnp.pad(w, ((0, 0), (0, width - w.shape[1])))


def _pad_last(w, width):
    return jnp.pad(w, [(0, 0)] * (w.ndim - 1) + [(0, width - w.shape[-1])])


def _rotate_half_cols(w):
    half = MLA_ROPE // 2
    return jnp.concatenate([-w[..., half:], w[..., :half]], axis=-1)


def _w_in_padded(w_in):
    d = w_in.shape[0]
    w_in = w_in.astype(_BF)
    sizes = (MLA_Q_RANK, MLA_KV_RANK, MLA_ROPE, NSA_HEADS * NSA_DK,
             N_BRANCH * NSA_GROUPS * NSA_DK, N_BRANCH * NSA_GROUPS * NSA_DV, NSA_HEADS * N_BRANCH)
    offs = np.concatenate([[0], np.cumsum(sizes)])
    cq, ckv, kr, qn, kn, vn, gn = [w_in[:, offs[a]:offs[a + 1]] for a in range(len(sizes))]
    qn = _pad_last(qn.reshape(d, NSA_HEADS, NSA_DK), DKP).reshape(d, NSA_HEADS * DKP)
    kn = _pad_last(kn.reshape(d, N_BRANCH * NSA_GROUPS, NSA_DK), DKP).reshape(d, -1)
    cols = [cq, _pad_cols(gn, LANES), _pad_cols(kr, LANES), ckv,
            _pad_cols(_rotate_half_cols(kr), LANES),
            jnp.zeros((d, COL_QN - COL_KR_ROT - LANES), w_in.dtype), qn, kn, vn]
    w = jnp.concatenate(cols, axis=1)
    assert w.shape[1] == D_IN_PAD
    colscale = np.ones((1, D_IN_PAD), np.float32)
    colscale[:, COL_QN:COL_KN] = NSA_DK ** -0.5 * LOG2_E
    return w, jnp.asarray(colscale)


def _w_uq_padded(w_uq):
    r = w_uq.shape[0]
    w = w_uq.reshape(r, MLA_HEADS, MLA_NOPE + MLA_ROPE)
    nope = w[..., :MLA_NOPE].reshape(r, -1)
    rope = w[..., MLA_NOPE:]
    rope_p = _pad_last(rope, LANES).reshape(r, -1)
    rot_p = _pad_last(_rotate_half_cols(rope), LANES).reshape(r, -1)
    return jnp.concatenate([nope, rope_p, rot_p], axis=1).astype(_BF)


def _cmp_w1_padded(w1, d, dpad):
    return _pad_last(w1.reshape(CMP_LEN, d, CMP_HIDDEN).transpose(0, 2, 1), dpad) \
        .transpose(0, 2, 1).reshape(CMP_LEN * dpad, CMP_HIDDEN).astype(_BF)


def _rope_tables(s):
    inv = ROPE_THETA ** (-jnp.arange(0, MLA_ROPE, 2, dtype=_F32) / MLA_ROPE)
    ang = jnp.arange(s, dtype=_F32)[:, None] * inv[None, :]
    cos = _pad_cols(jnp.tile(jnp.cos(ang), (1, 2)), LANES)
    sin = _pad_cols(jnp.tile(jnp.sin(ang), (1, 2)), LANES)
    return cos, sin


def _overlap_t(s):
    s_start = np.arange(N_SLC_PAD)[:, None] * SLC_LEN
    c_start = np.arange(s // CMP_STRIDE)[None, :] * CMP_STRIDE
    ov = (c_start < s_start + SLC_LEN) & (c_start + CMP_LEN > s_start)
    return jnp.asarray(ov.astype(np.float32), _BF)


def _mixer_heads(x, mod, norm_mix, w_in, mla_q_norm, w_uq, mla_kv_norm, w_ukv, pe_cmp_k, w_cmp_k1,
                 w_cmp_k2, pe_cmp_v, w_cmp_v1, w_cmp_v2, rel_bias):
    s = x.shape[0]
    assert s % ATT_TILE == 0 and s // SLC_LEN <= N_SLC_PAD and s >= 3 * ATT_TILE
    w_pad, colscale = _w_in_padded(w_in)
    proj = _inproj(x, mod, norm_mix.reshape(1, -1), w_pad, colscale, mod_row=3)

    cos, sin = _rope_tables(s)
    q, k, v = _mla_prep(proj, mla_q_norm.reshape(1, -1), mla_kv_norm.reshape(1, -1),
                        _w_uq_padded(w_uq), w_ukv.astype(_BF), cos, sin)
    o_mla = _mla_attn(q, k, v)

    n_chunks = s // CMP_STRIDE
    kn0 = COL_KN
    k_cmp = jnp.stack([proj[:, kn0 + g * DKP:kn0 + (g + 1) * DKP] for g in range(NSA_GROUPS)])
    v_cmp = jnp.stack([proj[:, COL_VN + g * NSA_DV:COL_VN + (g + 1) * NSA_DV] for g in range(NSA_GROUPS)])
    pe_k = jnp.broadcast_to(_pad_last(pe_cmp_k, DKP).reshape(1, -1), (SUBLANES, CMP_LEN * DKP)).astype(_BF)
    pe_v = jnp.broadcast_to(pe_cmp_v.reshape(1, -1), (SUBLANES, CMP_LEN * NSA_DV)).astype(_BF)
    kc = _compress(k_cmp.reshape(NSA_GROUPS, n_chunks, CMP_STRIDE * DKP), pe_k,
                   _cmp_w1_padded(w_cmp_k1, NSA_DK, DKP), _pad_cols(w_cmp_k2, DKP).astype(_BF))
    vc = _compress(v_cmp.reshape(NSA_GROUPS, n_chunks, CMP_STRIDE * NSA_DV), pe_v,
                   w_cmp_v1.astype(_BF), w_cmp_v2.astype(_BF))
    tc = _bias_tables(rel_bias, _cmp_buckets(CMP_TILE))
    tw = _bias_tables(rel_bias, _window_buckets(ATT_TILE))
    ov_t = _overlap_t(s)
    a_far = jnp.stack([jnp.concatenate([vc[g, CMP_PAD:].T, ov_t], axis=0) for g in range(NSA_GROUPS)])
    oc_t, sel_t = _nsa_cmp(proj, kc, vc, a_far, tc)
    o_nsa = _nsa_main(proj, sel_t, tw, oc_t)
    return o_mla, o_nsa


def kernel(x, c, w_ada, b_ada, norm_ffn1, w1_gate, w1_up, w1_down, norm_mix, w_in, mla_q_norm, w_uq, mla_kv_norm, w_ukv, pe_cmp_k, w_cmp_k1, w_cmp_k2, pe_cmp_v, w_cmp_v1, w_cmp_v2, rel_bias, w_out, norm_ffn2, w2_gate, w2_up, w2_down, norm_final):
    assert x.shape[0] == 1 and w_ada.shape[0] == 1
    xs = x[0]
    fin = norm_final.reshape(1, -1)
    mod = _ada_mod(c, w_ada[0], b_ada[0]).reshape(9, D_MODEL)
    xs = _ffn(xs, mod, norm_ffn1[0].reshape(1, -1), w1_gate[0], w1_up[0], w1_down[0], fin,
              mod_row=0, final_norm=False)
    o_mla, o_nsa = _mixer_heads(xs, mod, norm_mix[0], w_in[0], mla_q_norm[0], w_uq[0], mla_kv_norm[0],
                                w_ukv[0], pe_cmp_k[0], w_cmp_k1[0], w_cmp_k2[0], pe_cmp_v[0],
                                w_cmp_v1[0], w_cmp_v2[0], rel_bias)
    xs = _outproj(xs, mod, o_mla, o_nsa, w_out[0].astype(_BF), mod_row=5)
    xs = _ffn(xs, mod, norm_ffn2[0].reshape(1, -1), w2_gate[0], w2_up[0], w2_down[0], fin,
              mod_row=6, final_norm=True)
    return xs[None]
```

```python
import functools
import math

import numpy as np
import jax
import jax.numpy as jnp
from jax import lax
from jax.experimental import pallas as pl
from jax.experimental.pallas import tpu as pltpu

D_MODEL = 2048
D_FF = 5632
EPS = 1e-6
MLA_HEADS = 8
MLA_Q_RANK = 768
MLA_KV_RANK = 512
MLA_NOPE = 128
MLA_ROPE = 64
MLA_V = 128
ROPE_THETA = 10000.0
NSA_HEADS = 8
NSA_GROUPS = 2
NSA_HPG = NSA_HEADS // NSA_GROUPS
NSA_DK = 192
NSA_DV = 128
CMP_LEN = 32
CMP_STRIDE = 16
CMP_HIDDEN = 256
SLC_LEN = 64
SLC_SHIFT = 6
SLC_TOPN = 16
WINDOW = 512
N_BRANCH = 3
REL_BUCKETS = 32
REL_MAX_DIST = 128

LANES = 128
SUBLANES = 8
VMEM_LIMIT = 60000 * 1024

DKP = 256
ATT_TILE = 256
MLA_TILE = 1024
CMP_TILE = 512
CMP_PAD = 128
N_SLC_PAD = 128

_BF = jnp.bfloat16
_F32 = jnp.float32
_NEG_INF = float("-inf")
LOG2_E = math.log2(math.e)

COL_CQ = 0
COL_GATE = 768
COL_KR = 896
COL_CKV = 1024
COL_KR_ROT = 1536
COL_QN = 2048
COL_KN = COL_QN + NSA_HEADS * DKP
COL_VN = COL_KN + N_BRANCH * NSA_GROUPS * DKP
D_IN_PAD = COL_VN + N_BRANCH * NSA_GROUPS * NSA_DV


def _dot(a, b):
    return jnp.dot(a, b, preferred_element_type=_F32)


def _dot_nt(a, b):
    return lax.dot_general(a, b, (((1,), (1,)), ((), ())), preferred_element_type=_F32)


def _rms(x, gain):
    return x * lax.rsqrt(jnp.mean(x * x, axis=-1, keepdims=True) + EPS) * gain


def _params(semantics):
    return pltpu.CompilerParams(dimension_semantics=semantics, vmem_limit_bytes=VMEM_LIMIT)


ADA_ROWS = 256


def _ada_kernel(c_ref, w_ref, b_ref, o_ref):
    tn = o_ref.shape[1]

    def body(k, acc):
        r = pl.multiple_of(k * ADA_ROWS, ADA_ROWS)
        c = c_ref[pl.ds(r, ADA_ROWS), :]
        s = c * jax.nn.sigmoid(c)
        prod = w_ref[pl.ds(r, ADA_ROWS), :] * s
        return acc + jnp.sum(prod.reshape(ADA_ROWS // SUBLANES, SUBLANES, tn), axis=0)

    acc = lax.fori_loop(0, D_MODEL // ADA_ROWS, body, jnp.zeros((SUBLANES, tn), _F32))
    o_ref[...] = jnp.sum(acc, axis=0, keepdims=True) + b_ref[...]


def _ada_mod(c, w_ada, b_ada):
    n = w_ada.shape[1]
    tn = 1024
    return pl.pallas_call(
        _ada_kernel,
        out_shape=jax.ShapeDtypeStruct((1, n), _F32),
        grid=(n // tn,),
        in_specs=[
            pl.BlockSpec((D_MODEL, 1), lambda j: (0, 0)),
            pl.BlockSpec((D_MODEL, tn), lambda j: (0, j)),
            pl.BlockSpec((1, tn), lambda j: (0, j)),
        ],
        out_specs=pl.BlockSpec((1, tn), lambda j: (0, j)),
        compiler_params=_params(("arbitrary",)),
        name="ada_mod",
    )(c.reshape(D_MODEL, 1), w_ada, b_ada.reshape(1, n))


def _ffn_kernel(x_ref, mod_ref, gain_ref, wg_ref, wu_ref, wd_ref, fin_ref, o_ref, h_ref,
                *, mod_row, final_norm):
    j = pl.program_id(1)

    @pl.when(j == 0)
    def _():
        y = _rms(x_ref[...], gain_ref[...])
        h = y * (1.0 + mod_ref[mod_row + 1:mod_row + 2, :]) + mod_ref[mod_row:mod_row + 1, :]
        h_ref[...] = h.astype(_BF)
        o_ref[...] = jnp.zeros_like(o_ref)

    h = h_ref[...]
    g = _dot(h, wg_ref[...].astype(_BF))
    u = _dot(h, wu_ref[...].astype(_BF))
    a = (g * jax.nn.sigmoid(g)) * u
    o_ref[...] += _dot(a.astype(_BF), wd_ref[...].astype(_BF))

    @pl.when(j == pl.num_programs(1) - 1)
    def _():
        x2 = x_ref[...] + (0.5 * mod_ref[mod_row + 2:mod_row + 3, :]) * o_ref[...]
        if final_norm:
            x2 = _rms(x2, fin_ref[...])
        o_ref[...] = x2


def _ffn(x, mod, gain, wg, wu, wd, fin, *, mod_row, final_norm, tm=1024, tf=256):
    s = x.shape[0]
    tm = min(tm, s)
    kern = functools.partial(_ffn_kernel, mod_row=mod_row, final_norm=final_norm)
    return pl.pallas_call(
        kern,
        out_shape=jax.ShapeDtypeStruct((s, D_MODEL), _F32),
        grid=(s // tm, D_FF // tf),
        in_specs=[
            pl.BlockSpec((tm, D_MODEL), lambda i, j: (i, 0)),
            pl.BlockSpec((9, D_MODEL), lambda i, j: (0, 0)),
            pl.BlockSpec((1, D_MODEL), lambda i, j: (0, 0)),
            pl.BlockSpec((D_MODEL, tf), lambda i, j: (0, j)),
            pl.BlockSpec((D_MODEL, tf), lambda i, j: (0, j)),
            pl.BlockSpec((tf, D_MODEL), lambda i, j: (j, 0)),
            pl.BlockSpec((1, D_MODEL), lambda i, j: (0, 0)),
        ],
        out_specs=pl.BlockSpec((tm, D_MODEL), lambda i, j: (i, 0)),
        scratch_shapes=[pltpu.VMEM((tm, D_MODEL), _BF)],
        compiler_params=_params(("parallel", "arbitrary")),
        name="ffn_final" if final_norm else "ffn",
    )(x, mod, gain, wg, wu, wd, fin)


def _inproj_kernel(x_ref, mod_ref, gain_ref, w_ref, cs_ref, o_ref, h_ref, *, mod_row):
    @pl.when(pl.program_id(1) == 0)
    def _():
        y = _rms(x_ref[...], gain_ref[...])
        h = y * (1.0 + mod_ref[mod_row + 1:mod_row + 2, :]) + mod_ref[mod_row:mod_row + 1, :]
        h_ref[...] = h.astype(_BF)

    o_ref[...] = (_dot(h_ref[...], w_ref[...]) * cs_ref[...]).astype(o_ref.dtype)


def _inproj(x, mod, gain, w_pad, colscale, *, mod_row, tm=1024, tn=1280):
    s = x.shape[0]
    tm = min(tm, s)
    n = w_pad.shape[1]
    return pl.pallas_call(
        functools.partial(_inproj_kernel, mod_row=mod_row),
        out_shape=jax.ShapeDtypeStruct((s, n), _BF),
        grid=(s // tm, n // tn),
        in_specs=[
            pl.BlockSpec((tm, D_MODEL), lambda i, j: (i, 0)),
            pl.BlockSpec((9, D_MODEL), lambda i, j: (0, 0)),
            pl.BlockSpec((1, D_MODEL), lambda i, j: (0, 0)),
            pl.BlockSpec((D_MODEL, tn), lambda i, j: (0, j)),
            pl.BlockSpec((1, tn), lambda i, j: (0, j)),
        ],
        out_specs=pl.BlockSpec((tm, tn), lambda i, j: (i, j)),
        scratch_shapes=[pltpu.VMEM((tm, D_MODEL), _BF)],
        compiler_params=_params(("parallel", "arbitrary")),
        name="inproj",
    )(x, mod, gain, w_pad, colscale)


def _mla_prep_kernel(cq_ref, ckv_ref, kr_ref, krr_ref, gq_ref, gkv_ref, wq_ref, wkv_ref,
                     cos_ref, sin_ref, q_ref, k_ref, v_ref):
    scale = (MLA_NOPE + MLA_ROPE) ** -0.5 * LOG2_E
    cos = cos_ref[...]
    sin = sin_ref[...]
    hq = _rms(cq_ref[...].astype(_F32), gq_ref[...]).astype(_BF)
    qa = _dot(hq, wq_ref[...])
    nh = MLA_HEADS * LANES
    for h in range(MLA_HEADS):
        lo = h * LANES
        nope = qa[:, lo:lo + LANES]
        a = qa[:, nh + lo:nh + lo + LANES]
        b = qa[:, 2 * nh + lo:2 * nh + lo + LANES]
        q_ref[:, h * DKP:h * DKP + LANES] = (nope * scale).astype(_BF)
        q_ref[:, h * DKP + LANES:(h + 1) * DKP] = ((a * cos + b * sin) * scale).astype(_BF)
    hkv = _rms(ckv_ref[...].astype(_F32), gkv_ref[...]).astype(_BF)
    kv = _dot(hkv, wkv_ref[...])
    k_rope = (kr_ref[...].astype(_F32) * cos + krr_ref[...].astype(_F32) * sin).astype(_BF)
    for h in range(MLA_HEADS):
        lo = h * (MLA_NOPE + MLA_V)
        k_ref[:, h * DKP:h * DKP + LANES] = kv[:, lo:lo + MLA_NOPE].astype(_BF)
        k_ref[:, h * DKP + LANES:(h + 1) * DKP] = k_rope
        v_ref[:, h * MLA_V:(h + 1) * MLA_V] = kv[:, lo + MLA_NOPE:lo + MLA_NOPE + MLA_V].astype(_BF)


def _mla_prep(proj, gq, gkv, wq, wkv, cos, sin, *, tm=512):
    s = proj.shape[0]
    const = lambda i: (0, 0)
    return pl.pallas_call(
        _mla_prep_kernel,
        out_shape=(
            jax.ShapeDtypeStruct((s, MLA_HEADS * DKP), _BF),
            jax.ShapeDtypeStruct((s, MLA_HEADS * DKP), _BF),
            jax.ShapeDtypeStruct((s, MLA_HEADS * MLA_V), _BF),
        ),
        grid=(s // tm,),
        in_specs=[
            pl.BlockSpec((tm, MLA_Q_RANK), lambda i: (i, COL_CQ // MLA_Q_RANK)),
            pl.BlockSpec((tm, MLA_KV_RANK), lambda i: (i, COL_CKV // MLA_KV_RANK)),
            pl.BlockSpec((tm, LANES), lambda i: (i, COL_KR // LANES)),
            pl.BlockSpec((tm, LANES), lambda i: (i, COL_KR_ROT // LANES)),
            pl.BlockSpec((1, MLA_Q_RANK), const),
            pl.BlockSpec((1, MLA_KV_RANK), const),
            pl.BlockSpec(wq.shape, const),
            pl.BlockSpec(wkv.shape, const),
            pl.BlockSpec((tm, LANES), lambda i: (i, 0)),
            pl.BlockSpec((tm, LANES), lambda i: (i, 0)),
        ],
        out_specs=(
            pl.BlockSpec((tm, MLA_HEADS * DKP), lambda i: (i, 0)),
            pl.BlockSpec((tm, MLA_HEADS * DKP), lambda i: (i, 0)),
            pl.BlockSpec((tm, MLA_HEADS * MLA_V), lambda i: (i, 0)),
        ),
        compiler_params=_params(("parallel",)),
        name="mla_prep",
    )(proj, proj, proj, proj, gq, gkv, wq, wkv, cos, sin)


def _dot_tn(a, b):
    return lax.dot_general(a, b, (((0,), (0,)), ((), ())), preferred_element_type=_F32)


def _online_update_t(st, vs, m, l, acc):
    t = vs[0].shape[0]
    m_new = jnp.maximum(m, jnp.max(st, axis=0, keepdims=True))
    alpha = jnp.exp2(m - m_new)
    pt = jnp.exp2(st - m_new)
    l = alpha * l + jnp.sum(pt, axis=0, keepdims=True)
    pb = pt.astype(_BF)
    pv = functools.reduce(jnp.add, [_dot_tn(v, pb[a * t:(a + 1) * t, :]) for a, v in enumerate(vs)])
    return m_new, l, alpha * acc + pv


def _mla_attn_kernel(q_ref, k_ref, v_ref, o_ref, m_ref, l_ref, acc_ref, *, n_sub):
    t = q_ref.shape[0]
    tk = n_sub * t
    hp = q_ref.shape[1] // DKP
    i = pl.program_id(1)
    qs = [q_ref[:, h * DKP:(h + 1) * DKP] for h in range(hp)]
    n_full = i // n_sub
    tail_b0 = jnp.maximum(i + 1 - n_sub, 0)

    m_ref[...] = jnp.full(m_ref.shape, -1e30, _F32)
    l_ref[...] = jnp.zeros(l_ref.shape, _F32)
    acc_ref[...] = jnp.zeros(acc_ref.shape, _F32)

    def first_tile(c):
        return jnp.where(c < n_full, c * n_sub, tail_b0)

    def logits(c, h):
        r0 = pl.multiple_of(first_tile(c) * t, t)
        return _dot_nt(k_ref[pl.ds(r0, tk), h * DKP:(h + 1) * DKP], qs[h])

    def update(c, h, st):
        b0 = first_tile(c)
        vs = [v_ref[pl.ds(pl.multiple_of((b0 + a) * t, t), t), h * MLA_V:(h + 1) * MLA_V] for a in range(n_sub)]
        m, l, acc = _online_update_t(st, vs, m_ref[h], l_ref[h], acc_ref[h])
        m_ref[h] = m
        l_ref[h] = l
        acc_ref[h] = acc

    def body(c, st):
        for h in range(hp):
            st_next = logits(c, h + 1) if h + 1 < hp else logits(c + 1, 0)
            update(c, h, st)
            st = st_next
        return st

    st = lax.fori_loop(0, n_full, body, logits(0, 0))
    kpos = tail_b0 * t + lax.broadcasted_iota(jnp.int32, (tk, t), 0)
    qpos = i * t + lax.broadcasted_iota(jnp.int32, (tk, t), 1)
    tail_mask = (kpos >= n_full * tk) & (kpos <= qpos)
    for h in range(hp):
        st_next = logits(n_full, h + 1) if h + 1 < hp else None
        update(n_full, h, jnp.where(tail_mask, st, _NEG_INF))
        st = st_next
    for h in range(hp):
        o_ref[:, h * MLA_V:(h + 1) * MLA_V] = (acc_ref[h] * (1.0 / l_ref[h])).T.astype(o_ref.dtype)


def _mla_attn(q, k, v, *, n_sub=1, hp=4):
    s = q.shape[0]
    t = min(MLA_TILE, s // n_sub)
    nb = s // t
    return pl.pallas_call(
        functools.partial(_mla_attn_kernel, n_sub=n_sub),
        out_shape=jax.ShapeDtypeStruct((s, MLA_HEADS * MLA_V), _BF),
        grid=(MLA_HEADS // hp, nb),
        in_specs=[
            pl.BlockSpec((t, hp * DKP), lambda h, i: (i, h)),
            pl.BlockSpec((s, hp * DKP), lambda h, i: (0, h), pipeline_mode=pl.Buffered(1)),
            pl.BlockSpec((s, hp * MLA_V), lambda h, i: (0, h), pipeline_mode=pl.Buffered(1)),
        ],
        out_specs=pl.BlockSpec((t, hp * MLA_V), lambda h, i: (i, h)),
        scratch_shapes=[
            pltpu.VMEM((hp, 1, t), _F32),
            pltpu.VMEM((hp, 1, t), _F32),
            pltpu.VMEM((hp, MLA_V, t), _F32),
        ],
        compiler_params=_params(("parallel", "arbitrary")),
        name="mla_attn",
    )(q, k, v)


def _t5_bucket_np(dist):
    n = np.maximum(dist, 0)
    max_exact = REL_BUCKETS // 2
    nf = np.maximum(n, 1).astype(np.float32)
    large = max_exact + (np.log(nf / np.float32(max_exact)) / np.float32(math.log(REL_MAX_DIST / max_exact))
                         * np.float32(REL_BUCKETS - max_exact)).astype(np.int32)
    large = np.minimum(large, REL_BUCKETS - 1)
    return np.where(n < max_exact, n, large).astype(np.int32)


def _bias_table_kernel(rb_ref, bk_ref, o_ref):
    h = pl.program_id(0)
    bk = bk_ref[0]
    base = rb_ref[REL_BUCKETS - 1, h]
    acc = jnp.zeros(bk.shape, _F32)
    for b in range(REL_BUCKETS - 1):
        acc = jnp.where(bk == b, (rb_ref[b, h] - base) * LOG2_E, acc)
    o_ref[0, 0] = acc


def _bias_tables(rel_bias, buckets):
    nt, r, c = buckets.shape
    return pl.pallas_call(
        _bias_table_kernel,
        out_shape=jax.ShapeDtypeStruct((NSA_HEADS, nt, r, c), _F32),
        grid=(NSA_HEADS, nt),
        in_specs=[
            pl.BlockSpec(memory_space=pltpu.SMEM),
            pl.BlockSpec((1, r, c), lambda h, t: (t, 0, 0)),
        ],
        out_specs=pl.BlockSpec((1, 1, r, c), lambda h, t: (h, t, 0, 0)),
        compiler_params=_params(("arbitrary", "arbitrary")),
        name="bias_tables",
    )(rel_bias, jnp.asarray(buckets))


def _window_buckets(t):
    r = np.arange(3 * t)[:, None]
    c = np.arange(t)[None, :]
    return _t5_bucket_np(c - r + 2 * t)[None]


def _cmp_buckets(t):
    c = np.arange(CMP_PAD)[:, None]
    r = np.arange(t)[None, :]
    return _t5_bucket_np(r - (CMP_LEN - 1) - t + CMP_STRIDE * (CMP_PAD - c))[None]


def _compress_kernel(ch_ref, pe_ref, w1_ref, w2_ref, o_ref):
    half = w1_ref.shape[0] // 2
    ch = ch_ref[0]
    a = _dot(ch, w1_ref[:half, :])
    b = _dot(ch, w1_ref[half:, :])
    n = ch.shape[0]
    b_next = pltpu.roll(b, n - 1, axis=0)
    c0 = _dot(pe_ref[...], w1_ref[...])[0:1, :]
    hid = a + b_next + c0
    hid = hid * jax.nn.sigmoid(hid)
    o_ref[0, :CMP_PAD, :] = jnp.zeros((CMP_PAD, o_ref.shape[2]), o_ref.dtype)
    o_ref[0, CMP_PAD:, :] = _dot(hid.astype(_BF), w2_ref[...]).astype(o_ref.dtype)


def _compress(chunks, pe_flat, w1, w2):
    g, n, kd = chunks.shape
    d = w2.shape[1]
    return pl.pallas_call(
        _compress_kernel,
        out_shape=jax.ShapeDtypeStruct((g, CMP_PAD + n, d), _BF),
        grid=(g,),
        in_specs=[
            pl.BlockSpec((1, n, kd), lambda i: (i, 0, 0)),
            pl.BlockSpec(pe_flat.shape, lambda i: (0, 0)),
            pl.BlockSpec(w1.shape, lambda i: (0, 0)),
            pl.BlockSpec(w2.shape, lambda i: (0, 0)),
        ],
        out_specs=pl.BlockSpec((1, CMP_PAD + n, d), lambda i: (i, 0, 0)),
        compiler_params=_params(("arbitrary",)),
        name="nsa_compress",
    )(chunks, pe_flat, w1, w2)


def _nsa_cmp_kernel(q_ref, kc_ref, vc_ref, afar_ref, tc_ref, oc_ref, sel_ref):
    t = q_ref.shape[0]
    nc = kc_ref.shape[1] - CMP_PAD
    per_tile = t // CMP_STRIDE
    i = pl.program_id(1)
    thr = (i + 1) * per_tile - CMP_PAD
    near0 = pl.multiple_of((i + 1) * per_tile, per_tile)

    k_ext = jnp.concatenate([kc_ref[0, CMP_PAD:, :], kc_ref[0, pl.ds(near0, CMP_PAD), :]], axis=0)
    vcn_t = vc_ref[0, pl.ds(near0, CMP_PAD), :].astype(_F32).T.astype(_BF)
    jn = lax.broadcasted_iota(jnp.int32, (N_SLC_PAD, CMP_PAD), 0) * SLC_LEN
    cn = (thr + lax.broadcasted_iota(jnp.int32, (N_SLC_PAD, CMP_PAD), 1)) * CMP_STRIDE
    ovn_t = jnp.where((cn < jn + SLC_LEN) & (cn + CMP_LEN > jn), 1.0, 0.0).astype(_BF)
    a_near = jnp.concatenate([vcn_t, ovn_t], axis=0)
    a_far = afar_ref[0]

    far_mask = lax.broadcasted_iota(jnp.int32, (nc, t), 0) < thr
    c = lax.broadcasted_iota(jnp.int32, (CMP_PAD, t), 0)
    r = lax.broadcasted_iota(jnp.int32, (CMP_PAD, t), 1)
    near_mask = (CMP_STRIDE * c <= r - (CMP_LEN - 1) - t + CMP_STRIDE * CMP_PAD) & (c >= -thr)

    def logits(h):
        return _dot_nt(k_ext, q_ref[:, h * DKP:(h + 1) * DKP])

    imp = jnp.zeros((N_SLC_PAD, t), _F32)
    st_next = logits(0)
    for h in range(NSA_HPG):
        st = st_next
        if h + 1 < NSA_HPG:
            st_next = logits(h + 1)
        sf = jnp.where(far_mask, st[:nc], _NEG_INF)
        sn = jnp.where(near_mask, st[nc:] + tc_ref[h, 0], _NEG_INF)
        m = jnp.maximum(jnp.max(sf, axis=0, keepdims=True), jnp.max(sn, axis=0, keepdims=True))
        m = jnp.where(m == _NEG_INF, 0.0, m)
        pf = jnp.exp2(sf - m)
        pn = jnp.exp2(sn - m)
        den = jnp.sum(pf, axis=0, keepdims=True) + jnp.sum(pn, axis=0, keepdims=True)
        inv = 1.0 / jnp.maximum(den, 1e-30)
        pf = pf * inv
        pn = pn * inv
        pf_hi = pf.astype(_BF)
        pn_hi = pn.astype(_BF)
        res = _dot(a_far, pf_hi) + _dot(a_near, pn_hi)
        oc_ref[h * NSA_DV:(h + 1) * NSA_DV, :] = res[:NSA_DV].astype(oc_ref.dtype)
        pf_lo = (pf - pf_hi.astype(_F32)).astype(_BF)
        pn_lo = (pn - pn_hi.astype(_F32)).astype(_BF)
        imp = imp + res[NSA_DV:] + (_dot(a_far[NSA_DV:], pf_lo) + _dot(a_near[NSA_DV:], pn_lo))

    jj = lax.broadcasted_iota(jnp.int32, (N_SLC_PAD, t), 0)
    tq = i * t + lax.broadcasted_iota(jnp.int32, (N_SLC_PAD, t), 1)
    cur = tq >> SLC_SHIFT
    valid = jj * SLC_LEN <= tq
    forced = (jj == 0) | ((jj <= cur) & (jj >= cur - 1))
    work = jnp.where(valid, jnp.where(forced, jnp.inf, imp), _NEG_INF)
    blk = jj.astype(_F32)
    sel = jnp.zeros((N_SLC_PAD, t), _F32)
    for _ in range(SLC_TOPN):
        mx = jnp.max(work, axis=0, keepdims=True)
        first = jnp.min(jnp.where(work == mx, blk, float(N_SLC_PAD)), axis=0, keepdims=True)
        hit = blk == first
        sel = jnp.where(hit & (mx > _NEG_INF), 1.0, sel)
        work = jnp.where(hit, _NEG_INF, work)
    sel_ref[0] = sel


def _nsa_cmp(proj, kc_pad, vc_pad, a_far, tc):
    s = proj.shape[0]
    t = tc.shape[3]
    gw = NSA_HPG * DKP
    return pl.pallas_call(
        _nsa_cmp_kernel,
        out_shape=(
            jax.ShapeDtypeStruct((NSA_HEADS * NSA_DV, s), _BF),
            jax.ShapeDtypeStruct((NSA_GROUPS, N_SLC_PAD, s), _F32),
        ),
        grid=(NSA_GROUPS, s // t),
        in_specs=[
            pl.BlockSpec((t, gw), lambda g, i: (i, COL_QN // gw + g)),
            pl.BlockSpec((1,) + kc_pad.shape[1:], lambda g, i: (g, 0, 0)),
            pl.BlockSpec((1,) + vc_pad.shape[1:], lambda g, i: (g, 0, 0)),
            pl.BlockSpec((1,) + a_far.shape[1:], lambda g, i: (g, 0, 0)),
            pl.BlockSpec((NSA_HPG, 1, CMP_PAD, t), lambda g, i: (g, 0, 0, 0)),
        ],
        out_specs=(
            pl.BlockSpec((NSA_HPG * NSA_DV, t), lambda g, i: (g, i)),
            pl.BlockSpec((1, N_SLC_PAD, t), lambda g, i: (g, 0, i)),
        ),
        compiler_params=_params(("parallel", "arbitrary")),
        name="nsa_cmp",
    )(proj, kc_pad, vc_pad, a_far, tc)


def _nsa_main_kernel(q_ref, ks_ref, vs_ref, kw_ref, vw_ref, selt_ref, tw_ref, oc_ref, gate_ref,
                     o_ref, m_ref, l_ref, acc_ref, *, n_sub):
    t = q_ref.shape[0]

    def value_tile(ref, tile):
        return ref[pl.ds(pl.multiple_of(tile * t, t), t), :]

    tk = n_sub * t
    g = pl.program_id(0)
    i = pl.program_id(1)
    blocks_per_tile = t // SLC_LEN
    qs = [q_ref[:, h * DKP:(h + 1) * DKP] for h in range(NSA_HPG)]

    m_ref[...] = jnp.full(m_ref.shape, -1e30, _F32)
    l_ref[...] = jnp.zeros(l_ref.shape, _F32)
    acc_ref[...] = jnp.zeros(acc_ref.shape, _F32)

    def selected(tile0, n_tiles):
        rows = [jnp.broadcast_to(selt_ref[0, pl.ds(tile0 * blocks_per_tile + b, 1), :], (SLC_LEN, t))
                for b in range(n_tiles * blocks_per_tile)]
        return jnp.concatenate(rows, axis=0) > 0.5

    def run_heads(k, vts, mask, bias):
        def logits(h):
            st = _dot_nt(k, qs[h])
            if bias is not None:
                st = st + bias(h)
            return jnp.where(mask, st, _NEG_INF)

        st = logits(0)
        for h in range(NSA_HPG):
            st_next = logits(h + 1) if h + 1 < NSA_HPG else None
            m, l, acc = _online_update_t(st, vts, m_ref[h], l_ref[h], acc_ref[h])
            m_ref[h] = m
            l_ref[h] = l
            acc_ref[h] = acc
            st = st_next

    def far_chunk(tile0, n_tiles, first_new):
        r0 = pl.multiple_of(tile0 * t, t)
        mask = selected(tile0, n_tiles)
        if first_new is not None:
            kpos = r0 + lax.broadcasted_iota(jnp.int32, (n_tiles * t, t), 0)
            mask = mask & (kpos >= first_new) & (kpos < n_far * t)
        run_heads(ks_ref[pl.ds(r0, n_tiles * t), :],
                  [value_tile(vs_ref, tile0 + a) for a in range(n_tiles)], mask, None)

    n_far = jnp.maximum(i - 1, 0)
    n_full = n_far // n_sub
    n_left = n_far - n_full * n_sub

    def far_body(c, carry):
        far_chunk(c * n_sub, n_sub, None)
        return carry

    lax.fori_loop(0, n_full, far_body, 0)

    half = n_sub // 2

    @pl.when((n_left > 0) & (n_left <= half))
    def _():
        far_chunk(jnp.maximum(n_far - half, 0), half, n_full * tk)

    @pl.when(n_left > half)
    def _():
        far_chunk(jnp.maximum(n_far - n_sub, 0), n_sub, n_full * tk)

    prev = jnp.maximum(i - 1, 0)
    row2 = lax.broadcasted_iota(jnp.int32, (2 * t, t), 0)
    col2 = lax.broadcasted_iota(jnp.int32, (2 * t, t), 1)
    near_mask = (jnp.concatenate([selected(prev, 1), selected(i, 1)], axis=0)
                 & (col2 - row2 + t >= 0) & (row2 >= jnp.where(i >= 1, 0, t)))
    k_near = jnp.concatenate([ks_ref[pl.ds(pl.multiple_of(prev * t, t), t), :],
                              ks_ref[pl.ds(pl.multiple_of(i * t, t), t), :]], axis=0)
    run_heads(k_near, [value_tile(vs_ref, prev), value_tile(vs_ref, i)], near_mask,
              lambda h: tw_ref[h, 0, t:, :])

    prev2 = jnp.maximum(i - 2, 0)
    w_tiles = (prev2, prev, i)
    k_win = jnp.concatenate([kw_ref[pl.ds(pl.multiple_of(a * t, t), t), :] for a in w_tiles], axis=0)
    row3 = lax.broadcasted_iota(jnp.int32, (3 * t, t), 0)
    col3 = lax.broadcasted_iota(jnp.int32, (3 * t, t), 1)
    dist = col3 - row3 + 2 * t
    first_row = jnp.where(i >= 2, 0, jnp.where(i >= 1, t, 2 * t))
    win_mask = (dist >= 0) & (dist < WINDOW) & (row3 >= first_row)

    def window_logits(h):
        return jnp.where(win_mask, _dot_nt(k_win, qs[h]) + tw_ref[h, 0], _NEG_INF)

    gates_t = jax.nn.sigmoid(gate_ref[...].astype(_F32)).T
    st_next = window_logits(0)
    for h in range(NSA_HPG):
        st = st_next
        if h + 1 < NSA_HPG:
            st_next = window_logits(h + 1)
        m = jnp.max(st, axis=0, keepdims=True)
        pt = jnp.exp2(st - m)
        den = jnp.sum(pt, axis=0, keepdims=True)
        pb = pt.astype(_BF)
        o_w = functools.reduce(jnp.add, [_dot_tn(value_tile(vw_ref, a), pb[n * t:(n + 1) * t, :])
                                         for n, a in enumerate(w_tiles)]) * (1.0 / den)
        o_s = acc_ref[h] * (1.0 / l_ref[h])
        o_c = oc_ref[h * NSA_DV:(h + 1) * NSA_DV, :].astype(_F32)
        gc, gs, gw = [jnp.where(g == 0, gates_t[h * N_BRANCH + b:h * N_BRANCH + b + 1, :],
                                gates_t[(NSA_HPG + h) * N_BRANCH + b:(NSA_HPG + h) * N_BRANCH + b + 1, :])
                      for b in range(N_BRANCH)]
        o_ref[:, h * NSA_DV:(h + 1) * NSA_DV] = (gc * o_c + gs * o_s + gw * o_w).T.astype(o_ref.dtype)


def _nsa_main(proj, sel_t, tw, oc_t, *, n_sub=4):
    s = proj.shape[0]
    t = ATT_TILE
    nb = s // t
    gw = NSA_HPG * DKP
    kn0 = COL_KN // DKP
    vn0 = COL_VN // NSA_DV
    return pl.pallas_call(
        functools.partial(_nsa_main_kernel, n_sub=n_sub),
        out_shape=jax.ShapeDtypeStruct((s, NSA_HEADS * NSA_DV), _BF),
        grid=(NSA_GROUPS, nb),
        in_specs=[
            pl.BlockSpec((t, gw), lambda g, i: (i, COL_QN // gw + g)),
            pl.BlockSpec((s, DKP), lambda g, i: (0, kn0 + 1 * NSA_GROUPS + g)),
            pl.BlockSpec((s, NSA_DV), lambda g, i: (0, vn0 + 1 * NSA_GROUPS + g)),
            pl.BlockSpec((s, DKP), lambda g, i: (0, kn0 + 2 * NSA_GROUPS + g)),
            pl.BlockSpec((s, NSA_DV), lambda g, i: (0, vn0 + 2 * NSA_GROUPS + g)),
            pl.BlockSpec((1, N_SLC_PAD, t), lambda g, i: (g, 0, i)),
            pl.BlockSpec((NSA_HPG, 1, 3 * t, t), lambda g, i: (g, 0, 0, 0)),
            pl.BlockSpec((NSA_HPG * NSA_DV, t), lambda g, i: (g, i)),
            pl.BlockSpec((t, LANES), lambda g, i: (i, COL_GATE // LANES)),
        ],
        out_specs=pl.BlockSpec((t, NSA_HPG * NSA_DV), lambda g, i: (i, g)),
        scratch_shapes=[
            pltpu.VMEM((NSA_HPG, 1, t), _F32),
            pltpu.VMEM((NSA_HPG, 1, t), _F32),
            pltpu.VMEM((NSA_HPG, NSA_DV, t), _F32),
        ],
        compiler_params=_params(("parallel", "arbitrary")),
        name="nsa_main",
    )(proj, proj, proj, proj, proj, sel_t, tw, oc_t, proj)


def _outproj_kernel(x_ref, mod_ref, om_ref, on_ref, w_ref, o_ref, *, mod_row):
    half = om_ref.shape[1]
    y = _dot(om_ref[...], w_ref[:half, :]) + _dot(on_ref[...], w_ref[half:, :])
    o_ref[...] = x_ref[...] + mod_ref[mod_row:mod_row + 1, :] * y


def _outproj(x, mod, o_mla, o_nsa, w_out, *, mod_row, tm=512):
    s = x.shape[0]
    return pl.pallas_call(
        functools.partial(_outproj_kernel, mod_row=mod_row),
        out_shape=jax.ShapeDtypeStruct((s, D_MODEL), _F32),
        grid=(s // tm,),
        in_specs=[
            pl.BlockSpec((tm, D_MODEL), lambda i: (i, 0)),
            pl.BlockSpec((9, D_MODEL), lambda i: (0, 0)),
            pl.BlockSpec((tm, o_mla.shape[1]), lambda i: (i, 0)),
            pl.BlockSpec((tm, o_nsa.shape[1]), lambda i: (i, 0)),
            pl.BlockSpec(w_out.shape, lambda i: (0, 0)),
        ],
        out_specs=pl.BlockSpec((tm, D_MODEL), lambda i: (i, 0)),
        compiler_params=_params(("parallel",)),
        name="outproj",
    )(x, mod, o_mla, o_nsa, w_out)


def _pad_cols(w, width):
    return jnp.pad(w, ((0, 0), (0, width - w.shape[1])))


def _pad_last(w, width):
    return jnp.pad(w, [(0, 0)] * (w.ndim - 1) + [(0, width - w.shape[-1])])


def _rotate_half_cols(w):
    half = MLA_ROPE // 2
    return jnp.concatenate([-w[..., half:], w[..., :half]], axis=-1)


def _w_in_padded(w_in):
    d = w_in.shape[0]
    w_in = w_in.astype(_BF)
    sizes = (MLA_Q_RANK, MLA_KV_RANK, MLA_ROPE, NSA_HEADS * NSA_DK,
             N_BRANCH * NSA_GROUPS * NSA_DK, N_BRANCH * NSA_GROUPS * NSA_DV, NSA_HEADS * N_BRANCH)
    offs = np.concatenate([[0], np.cumsum(sizes)])
    cq, ckv, kr, qn, kn, vn, gn = [w_in[:, offs[a]:offs[a + 1]] for a in range(len(sizes))]
    qn = _pad_last(qn.reshape(d, NSA_HEADS, NSA_DK), DKP).reshape(d, NSA_HEADS * DKP)
    kn = _pad_last(kn.reshape(d, N_BRANCH * NSA_GROUPS, NSA_DK), DKP).reshape(d, -1)
    cols = [cq, _pad_cols(gn, LANES), _pad_cols(kr, LANES), ckv,
            _pad_cols(_rotate_half_cols(kr), LANES),
            jnp.zeros((d, COL_QN - COL_KR_ROT - LANES), w_in.dtype), qn, kn, vn]
    w = jnp.concatenate(cols, axis=1)
    assert w.shape[1] == D_IN_PAD
    colscale = np.ones((1, D_IN_PAD), np.float32)
    colscale[:, COL_QN:COL_KN] = NSA_DK ** -0.5 * LOG2_E
    return w, jnp.asarray(colscale)


def _w_uq_padded(w_uq):
    r = w_uq.shape[0]
    w = w_uq.reshape(r, MLA_HEADS, MLA_NOPE + MLA_ROPE)
    nope = w[..., :MLA_NOPE].reshape(r, -1)
    rope = w[..., MLA_NOPE:]
    rope_p = _pad_last(rope, LANES).reshape(r, -1)
    rot_p = _pad_last(_rotate_half_cols(rope), LANES).reshape(r, -1)
    return jnp.concatenate([nope, rope_p, rot_p], axis=1).astype(_BF)


def _cmp_w1_padded(w1, d, dpad):
    return _pad_last(w1.reshape(CMP_LEN, d, CMP_HIDDEN).transpose(0, 2, 1), dpad) \
        .transpose(0, 2, 1).reshape(CMP_LEN * dpad, CMP_HIDDEN).astype(_BF)


def _rope_tables(s):
    inv = ROPE_THETA ** (-jnp.arange(0, MLA_ROPE, 2, dtype=_F32) / MLA_ROPE)
    ang = jnp.arange(s, dtype=_F32)[:, None] * inv[None, :]
    cos = _pad_cols(jnp.tile(jnp.cos(ang), (1, 2)), LANES)
    sin = _pad_cols(jnp.tile(jnp.sin(ang), (1, 2)), LANES)
    return cos, sin


def _overlap_t(s):
    s_start = np.arange(N_SLC_PAD)[:, None] * SLC_LEN
    c_start = np.arange(s // CMP_STRIDE)[None, :] * CMP_STRIDE
    ov = (c_start < s_start + SLC_LEN) & (c_start + CMP_LEN > s_start)
    return jnp.asarray(ov.astype(np.float32), _BF)


def _mixer_heads(x, mod, norm_mix, w_in, mla_q_norm, w_uq, mla_kv_norm, w_ukv, pe_cmp_k, w_cmp_k1,
                 w_cmp_k2, pe_cmp_v, w_cmp_v1, w_cmp_v2, rel_bias):
    s = x.shape[0]
    assert s % ATT_TILE == 0 and s // SLC_LEN <= N_SLC_PAD and s >= 3 * ATT_TILE
    w_pad, colscale = _w_in_padded(w_in)
    proj = _inproj(x, mod, norm_mix.reshape(1, -1), w_pad, colscale, mod_row=3)

    cos, sin = _rope_tables(s)
    q, k, v = _mla_prep(proj, mla_q_norm.reshape(1, -1), mla_kv_norm.reshape(1, -1),
                        _w_uq_padded(w_uq), w_ukv.astype(_BF), cos, sin)
    o_mla = _mla_attn(q, k, v)

    n_chunks = s // CMP_STRIDE
    kn0 = COL_KN
    k_cmp = jnp.stack([proj[:, kn0 + g * DKP:kn0 + (g + 1) * DKP] for g in range(NSA_GROUPS)])
    v_cmp = jnp.stack([proj[:, COL_VN + g * NSA_DV:COL_VN + (g + 1) * NSA_DV] for g in range(NSA_GROUPS)])
    pe_k = jnp.broadcast_to(_pad_last(pe_cmp_k, DKP).reshape(1, -1), (SUBLANES, CMP_LEN * DKP)).astype(_BF)
    pe_v = jnp.broadcast_to(pe_cmp_v.reshape(1, -1), (SUBLANES, CMP_LEN * NSA_DV)).astype(_BF)
    kc = _compress(k_cmp.reshape(NSA_GROUPS, n_chunks, CMP_STRIDE * DKP), pe_k,
                   _cmp_w1_padded(w_cmp_k1, NSA_DK, DKP), _pad_cols(w_cmp_k2, DKP).astype(_BF))
    vc = _compress(v_cmp.reshape(NSA_GROUPS, n_chunks, CMP_STRIDE * NSA_DV), pe_v,
                   w_cmp_v1.astype(_BF), w_cmp_v2.astype(_BF))
    tc = _bias_tables(rel_bias, _cmp_buckets(CMP_TILE))
    tw = _bias_tables(rel_bias, _window_buckets(ATT_TILE))
    ov_t = _overlap_t(s)
    a_far = jnp.stack([jnp.concatenate([vc[g, CMP_PAD:].T, ov_t], axis=0) for g in range(NSA_GROUPS)])
    oc_t, sel_t = _nsa_cmp(proj, kc, vc, a_far, tc)
    o_nsa = _nsa_main(proj, sel_t, tw, oc_t)
    return o_mla, o_nsa


def kernel(x, c, w_ada, b_ada, norm_ffn1, w1_gate, w1_up, w1_down, norm_mix, w_in, mla_q_norm, w_uq, mla_kv_norm, w_ukv, pe_cmp_k, w_cmp_k1, w_cmp_k2, pe_cmp_v, w_cmp_v1, w_cmp_v2, rel_bias, w_out, norm_ffn2, w2_gate, w2_up, w2_down, norm_final):
    assert x.shape[0] == 1 and w_ada.shape[0] == 1
    xs = x[0]
    fin = norm_final.reshape(1, -1)
    mod = _ada_mod(c, w_ada[0], b_ada[0]).reshape(9, D_MODEL)
    xs = _ffn(xs, mod, norm_ffn1[0].reshape(1, -1), w1_gate[0], w1_up[0], w1_down[0], fin,
              mod_row=0, final_norm=False)
    o_mla, o_nsa = _mixer_heads(xs, mod, norm_mix[0], w_in[0], mla_q_norm[0], w_uq[0], mla_kv_norm[0],
                                w_ukv[0], pe_cmp_k[0], w_cmp_k1[0], w_cmp_k2[0], pe_cmp_v[0],
                                w_cmp_v1[0], w_cmp_v2[0], rel_bias)
    xs = _outproj(xs, mod, o_mla, o_nsa, w_out[0].astype(_BF), mod_row=5)
    xs = _ffn(xs, mod, norm_ffn2[0].reshape(1, -1), w2_gate[0], w2_up[0], w2_down[0], fin,
              mod_row=6, final_norm=True)
    return xs[None]
```

```python
import functools
import math

import numpy as np
import jax
import jax.numpy as jnp
from jax import lax
from jax.experimental import pallas as pl
from jax.experimental.pallas import tpu as pltpu

D_MODEL = 2048
D_FF = 5632
EPS = 1e-6
MLA_HEADS = 8
MLA_Q_RANK = 768
MLA_KV_RANK = 512
MLA_NOPE = 128
MLA_ROPE = 64
MLA_V = 128
ROPE_THETA = 10000.0
NSA_HEADS = 8
NSA_GROUPS = 2
NSA_HPG = NSA_HEADS // NSA_GROUPS
NSA_DK = 192
NSA_DV = 128
CMP_LEN = 32
CMP_STRIDE = 16
CMP_HIDDEN = 256
SLC_LEN = 64
SLC_SHIFT = 6
SLC_TOPN = 16
WINDOW = 512
N_BRANCH = 3
REL_BUCKETS = 32
REL_MAX_DIST = 128

LANES = 128
SUBLANES = 8
VMEM_LIMIT = 60000 * 1024

DKP = 256
ATT_TILE = 256
MLA_TILE = 1024
CMP_TILE = 1024
CMP_PAD = 128
N_SLC_PAD = 128

_BF = jnp.bfloat16
_F32 = jnp.float32
_NEG_INF = float("-inf")
LOG2_E = math.log2(math.e)

COL_CQ = 0
COL_GATE = 768
COL_KR = 896
COL_CKV = 1024
COL_KR_ROT = 1536
COL_QN = 2048
COL_KN = COL_QN + NSA_HEADS * DKP
COL_VN = COL_KN + N_BRANCH * NSA_GROUPS * DKP
D_IN_PAD = COL_VN + N_BRANCH * NSA_GROUPS * NSA_DV


def _dot(a, b):
    return jnp.dot(a, b, preferred_element_type=_F32)


def _dot_nt(a, b):
    return lax.dot_general(a, b, (((1,), (1,)), ((), ())), preferred_element_type=_F32)


def _rms(x, gain):
    return x * lax.rsqrt(jnp.mean(x * x, axis=-1, keepdims=True) + EPS) * gain


def _params(semantics):
    return pltpu.CompilerParams(dimension_semantics=semantics, vmem_limit_bytes=VMEM_LIMIT)


ADA_ROWS = 256


def _ada_kernel(c_ref, w_ref, b_ref, o_ref):
    tn = o_ref.shape[1]

    def body(k, acc):
        r = pl.multiple_of(k * ADA_ROWS, ADA_ROWS)
        c = c_ref[pl.ds(r, ADA_ROWS), :]
        s = c * jax.nn.sigmoid(c)
        prod = w_ref[pl.ds(r, ADA_ROWS), :] * s
        return acc + jnp.sum(prod.reshape(ADA_ROWS // SUBLANES, SUBLANES, tn), axis=0)

    acc = lax.fori_loop(0, D_MODEL // ADA_ROWS, body, jnp.zeros((SUBLANES, tn), _F32))
    o_ref[...] = jnp.sum(acc, axis=0, keepdims=True) + b_ref[...]


def _ada_mod(c, w_ada, b_ada):
    n = w_ada.shape[1]
    tn = 2048
    return pl.pallas_call(
        _ada_kernel,
        out_shape=jax.ShapeDtypeStruct((1, n), _F32),
        grid=(n // tn,),
        in_specs=[
            pl.BlockSpec((D_MODEL, 1), lambda j: (0, 0)),
            pl.BlockSpec((D_MODEL, tn), lambda j: (0, j)),
            pl.BlockSpec((1, tn), lambda j: (0, j)),
        ],
        out_specs=pl.BlockSpec((1, tn), lambda j: (0, j)),
        compiler_params=_params(("arbitrary",)),
        name="ada_mod",
    )(c.reshape(D_MODEL, 1), w_ada, b_ada.reshape(1, n))


def _ffn_kernel(x_ref, mod_ref, gain_ref, wg_ref, wu_ref, wd_ref, fin_ref, o_ref, h_ref,
                *, mod_row, final_norm):
    j = pl.program_id(1)

    @pl.when(j == 0)
    def _():
        y = _rms(x_ref[...], gain_ref[...])
        h = y * (1.0 + mod_ref[mod_row + 1:mod_row + 2, :]) + mod_ref[mod_row:mod_row + 1, :]
        h_ref[...] = h.astype(_BF)
        o_ref[...] = jnp.zeros_like(o_ref)

    h = h_ref[...]
    g = _dot(h, wg_ref[...].astype(_BF))
    u = _dot(h, wu_ref[...].astype(_BF))
    a = (g * jax.nn.sigmoid(g)) * u
    o_ref[...] += _dot(a.astype(_BF), wd_ref[...].astype(_BF))

    @pl.when(j == pl.num_programs(1) - 1)
    def _():
        x2 = x_ref[...] + (0.5 * mod_ref[mod_row + 2:mod_row + 3, :]) * o_ref[...]
        if final_norm:
            x2 = _rms(x2, fin_ref[...])
        o_ref[...] = x2


def _ffn(x, mod, gain, wg, wu, wd, fin, *, mod_row, final_norm, tm=1024, tf=256):
    s = x.shape[0]
    tm = min(tm, s)
    kern = functools.partial(_ffn_kernel, mod_row=mod_row, final_norm=final_norm)
    return pl.pallas_call(
        kern,
        out_shape=jax.ShapeDtypeStruct((s, D_MODEL), _F32),
        grid=(s // tm, D_FF // tf),
        in_specs=[
            pl.BlockSpec((tm, D_MODEL), lambda i, j: (i, 0)),
            pl.BlockSpec((9, D_MODEL), lambda i, j: (0, 0)),
            pl.BlockSpec((1, D_MODEL), lambda i, j: (0, 0)),
            pl.BlockSpec((D_MODEL, tf), lambda i, j: (0, j)),
            pl.BlockSpec((D_MODEL, tf), lambda i, j: (0, j)),
            pl.BlockSpec((tf, D_MODEL), lambda i, j: (j, 0)),
            pl.BlockSpec((1, D_MODEL), lambda i, j: (0, 0)),
        ],
        out_specs=pl.BlockSpec((tm, D_MODEL), lambda i, j: (i, 0)),
        scratch_shapes=[pltpu.VMEM((tm, D_MODEL), _BF)],
        compiler_params=_params(("parallel", "arbitrary")),
        name="ffn_final" if final_norm else "ffn",
    )(x, mod, gain, wg, wu, wd, fin)


def _inproj_kernel(x_ref, mod_ref, gain_ref, w_ref, cs_ref, o_ref, h_ref, *, mod_row):
    @pl.when(pl.program_id(1) == 0)
    def _():
        y = _rms(x_ref[...], gain_ref[...])
        h = y * (1.0 + mod_ref[mod_row + 1:mod_row + 2, :]) + mod_ref[mod_row:mod_row + 1, :]
        h_ref[...] = h.astype(_BF)

    o_ref[...] = (_dot(h_ref[...], w_ref[...]) * cs_ref[...]).astype(o_ref.dtype)


def _inproj(x, mod, gain, w_pad, colscale, *, mod_row, tm=1024, tn=1280):
    s = x.shape[0]
    tm = min(tm, s)
    n = w_pad.shape[1]
    return pl.pallas_call(
        functools.partial(_inproj_kernel, mod_row=mod_row),
        out_shape=jax.ShapeDtypeStruct((s, n), _BF),
        grid=(s // tm, n // tn),
        in_specs=[
            pl.BlockSpec((tm, D_MODEL), lambda i, j: (i, 0)),
            pl.BlockSpec((9, D_MODEL), lambda i, j: (0, 0)),
            pl.BlockSpec((1, D_MODEL), lambda i, j: (0, 0)),
            pl.BlockSpec((D_MODEL, tn), lambda i, j: (0, j)),
            pl.BlockSpec((1, tn), lambda i, j: (0, j)),
        ],
        out_specs=pl.BlockSpec((tm, tn), lambda i, j: (i, j)),
        scratch_shapes=[pltpu.VMEM((tm, D_MODEL), _BF)],
        compiler_params=_params(("parallel", "arbitrary")),
        name="inproj",
    )(x, mod, gain, w_pad, colscale)


def _mla_prep_kernel(cq_ref, ckv_ref, kr_ref, krr_ref, gq_ref, gkv_ref, wq_ref, wkv_ref,
                     cos_ref, sin_ref, q_ref, k_ref, v_ref):
    scale = (MLA_NOPE + MLA_ROPE) ** -0.5 * LOG2_E
    cos = cos_ref[...]
    sin = sin_ref[...]
    hq = _rms(cq_ref[...].astype(_F32), gq_ref[...]).astype(_BF)
    qa = _dot(hq, wq_ref[...])
    nh = MLA_HEADS * LANES
    for h in range(MLA_HEADS):
        lo = h * LANES
        nope = qa[:, lo:lo + LANES]
        a = qa[:, nh + lo:nh + lo + LANES]
        b = qa[:, 2 * nh + lo:2 * nh + lo + LANES]
        q_ref[:, h * DKP:h * DKP + LANES] = (nope * scale).astype(_BF)
        q_ref[:, h * DKP + LANES:(h + 1) * DKP] = ((a * cos + b * sin) * scale).astype(_BF)
    hkv = _rms(ckv_ref[...].astype(_F32), gkv_ref[...]).astype(_BF)
    kv = _dot(hkv, wkv_ref[...])
    k_rope = (kr_ref[...].astype(_F32) * cos + krr_ref[...].astype(_F32) * sin).astype(_BF)
    for h in range(MLA_HEADS):
        lo = h * (MLA_NOPE + MLA_V)
        k_ref[:, h * DKP:h * DKP + LANES] = kv[:, lo:lo + MLA_NOPE].astype(_BF)
        k_ref[:, h * DKP + LANES:(h + 1) * DKP] = k_rope
        v_ref[:, h * MLA_V:(h + 1) * MLA_V] = kv[:, lo + MLA_NOPE:lo + MLA_NOPE + MLA_V].astype(_BF)


def _mla_prep(proj, gq, gkv, wq, wkv, cos, sin, *, tm=512):
    s = proj.shape[0]
    const = lambda i: (0, 0)
    return pl.pallas_call(
        _mla_prep_kernel,
        out_shape=(
            jax.ShapeDtypeStruct((s, MLA_HEADS * DKP), _BF),
            jax.ShapeDtypeStruct((s, MLA_HEADS * DKP), _BF),
            jax.ShapeDtypeStruct((s, MLA_HEADS * MLA_V), _BF),
        ),
        grid=(s // tm,),
        in_specs=[
            pl.BlockSpec((tm, MLA_Q_RANK), lambda i: (i, COL_CQ // MLA_Q_RANK)),
            pl.BlockSpec((tm, MLA_KV_RANK), lambda i: (i, COL_CKV // MLA_KV_RANK)),
            pl.BlockSpec((tm, LANES), lambda i: (i, COL_KR // LANES)),
            pl.BlockSpec((tm, LANES), lambda i: (i, COL_KR_ROT // LANES)),
            pl.BlockSpec((1, MLA_Q_RANK), const),
            pl.BlockSpec((1, MLA_KV_RANK), const),
            pl.BlockSpec(wq.shape, const),
            pl.BlockSpec(wkv.shape, const),
            pl.BlockSpec((tm, LANES), lambda i: (i, 0)),
            pl.BlockSpec((tm, LANES), lambda i: (i, 0)),
        ],
        out_specs=(
            pl.BlockSpec((tm, MLA_HEADS * DKP), lambda i: (i, 0)),
            pl.BlockSpec((tm, MLA_HEADS * DKP), lambda i: (i, 0)),
            pl.BlockSpec((tm, MLA_HEADS * MLA_V), lambda i: (i, 0)),
        ),
        compiler_params=_params(("parallel",)),
        name="mla_prep",
    )(proj, proj, proj, proj, gq, gkv, wq, wkv, cos, sin)


def _dot_tn(a, b):
    return lax.dot_general(a, b, (((0,), (0,)), ((), ())), preferred_element_type=_F32)


def _online_update_t(st, vs, m, l, acc):
    t = vs[0].shape[0]
    m_new = jnp.maximum(m, jnp.max(st, axis=0, keepdims=True))
    alpha = jnp.exp2(m - m_new)
    pt = jnp.exp2(st - m_new)
    l = alpha * l + jnp.sum(pt, axis=0, keepdims=True)
    pb = pt.astype(_BF)
    pv = functools.reduce(jnp.add, [_dot_tn(v, pb[a * t:(a + 1) * t, :]) for a, v in enumerate(vs)])
    return m_new, l, alpha * acc + pv


def _mla_attn_kernel(q_ref, k_ref, v_ref, o_ref, m_ref, l_ref, acc_ref, *, n_sub):
    t = q_ref.shape[0]
    tk = n_sub * t
    hp = q_ref.shape[1] // DKP
    i = pl.program_id(1)
    qs = [q_ref[:, h * DKP:(h + 1) * DKP] for h in range(hp)]
    n_full = i // n_sub
    tail_b0 = jnp.maximum(i + 1 - n_sub, 0)

    m_ref[...] = jnp.full(m_ref.shape, -1e30, _F32)
    l_ref[...] = jnp.zeros(l_ref.shape, _F32)
    acc_ref[...] = jnp.zeros(acc_ref.shape, _F32)

    def first_tile(c):
        return jnp.where(c < n_full, c * n_sub, tail_b0)

    def logits(c, h):
        r0 = pl.multiple_of(first_tile(c) * t, t)
        return _dot_nt(k_ref[pl.ds(r0, tk), h * DKP:(h + 1) * DKP], qs[h])

    def update(c, h, st):
        b0 = first_tile(c)
        vs = [v_ref[pl.ds(pl.multiple_of((b0 + a) * t, t), t), h * MLA_V:(h + 1) * MLA_V] for a in range(n_sub)]
        m, l, acc = _online_update_t(st, vs, m_ref[h], l_ref[h], acc_ref[h])
        m_ref[h] = m
        l_ref[h] = l
        acc_ref[h] = acc

    def body(c, st):
        for h in range(hp):
            st_next = logits(c, h + 1) if h + 1 < hp else logits(c + 1, 0)
            update(c, h, st)
            st = st_next
        return st

    st = lax.fori_loop(0, n_full, body, logits(0, 0))
    kpos = tail_b0 * t + lax.broadcasted_iota(jnp.int32, (tk, t), 0)
    qpos = i * t + lax.broadcasted_iota(jnp.int32, (tk, t), 1)
    tail_mask = (kpos >= n_full * tk) & (kpos <= qpos)
    for h in range(hp):
        st_next = logits(n_full, h + 1) if h + 1 < hp else None
        update(n_full, h, jnp.where(tail_mask, st, _NEG_INF))
        st = st_next
    for h in range(hp):
        o_ref[:, h * MLA_V:(h + 1) * MLA_V] = (acc_ref[h] * (1.0 / l_ref[h])).T.astype(o_ref.dtype)


def _mla_attn(q, k, v, *, n_sub=1, hp=4):
    s = q.shape[0]
    t = min(MLA_TILE, s // n_sub)
    nb = s // t
    return pl.pallas_call(
        functools.partial(_mla_attn_kernel, n_sub=n_sub),
        out_shape=jax.ShapeDtypeStruct((s, MLA_HEADS * MLA_V), _BF),
        grid=(MLA_HEADS // hp, nb),
        in_specs=[
            pl.BlockSpec((t, hp * DKP), lambda h, i: (i, h)),
            pl.BlockSpec((s, hp * DKP), lambda h, i: (0, h), pipeline_mode=pl.Buffered(1)),
            pl.BlockSpec((s, hp * MLA_V), lambda h, i: (0, h), pipeline_mode=pl.Buffered(1)),
        ],
        out_specs=pl.BlockSpec((t, hp * MLA_V), lambda h, i: (i, h)),
        scratch_shapes=[
            pltpu.VMEM((hp, 1, t), _F32),
            pltpu.VMEM((hp, 1, t), _F32),
            pltpu.VMEM((hp, MLA_V, t), _F32),
        ],
        compiler_params=_params(("parallel", "arbitrary")),
        name="mla_attn",
    )(q, k, v)


def _t5_bucket_np(dist):
    n = np.maximum(dist, 0)
    max_exact = REL_BUCKETS // 2
    nf = np.maximum(n, 1).astype(np.float32)
    large = max_exact + (np.log(nf / np.float32(max_exact)) / np.float32(math.log(REL_MAX_DIST / max_exact))
                         * np.float32(REL_BUCKETS - max_exact)).astype(np.int32)
    large = np.minimum(large, REL_BUCKETS - 1)
    return np.where(n < max_exact, n, large).astype(np.int32)


def _bias_table_kernel(rb_ref, bk_ref, o_ref):
    h = pl.program_id(0)
    bk = bk_ref[0]
    base = rb_ref[REL_BUCKETS - 1, h]
    acc = jnp.zeros(bk.shape, _F32)
    for b in range(REL_BUCKETS - 1):
        acc = jnp.where(bk == b, (rb_ref[b, h] - base) * LOG2_E, acc)
    o_ref[0, 0] = acc


def _bias_tables(rel_bias, buckets):
    nt, r, c = buckets.shape
    return pl.pallas_call(
        _bias_table_kernel,
        out_shape=jax.ShapeDtypeStruct((NSA_HEADS, nt, r, c), _F32),
        grid=(NSA_HEADS, nt),
        in_specs=[
            pl.BlockSpec(memory_space=pltpu.SMEM),
            pl.BlockSpec((1, r, c), lambda h, t: (t, 0, 0)),
        ],
        out_specs=pl.BlockSpec((1, 1, r, c), lambda h, t: (h, t, 0, 0)),
        compiler_params=_params(("arbitrary", "arbitrary")),
        name="bias_tables",
    )(rel_bias, jnp.asarray(buckets))


def _window_buckets(t):
    r = np.arange(3 * t)[:, None]
    c = np.arange(t)[None, :]
    return _t5_bucket_np(c - r + 2 * t)[None]


def _cmp_buckets(t):
    c = np.arange(CMP_PAD)[:, None]
    r = np.arange(t)[None, :]
    return _t5_bucket_np(r - (CMP_LEN - 1) - t + CMP_STRIDE * (CMP_PAD - c))[None]


def _compress_kernel(ch_ref, pe_ref, w1_ref, w2_ref, o_ref):
    half = w1_ref.shape[0] // 2
    ch = ch_ref[0]
    a = _dot(ch, w1_ref[:half, :])
    b = _dot(ch, w1_ref[half:, :])
    n = ch.shape[0]
    b_next = pltpu.roll(b, n - 1, axis=0)
    c0 = _dot(pe_ref[...], w1_ref[...])[0:1, :]
    hid = a + b_next + c0
    hid = hid * jax.nn.sigmoid(hid)
    o_ref[0, :CMP_PAD, :] = jnp.zeros((CMP_PAD, o_ref.shape[2]), o_ref.dtype)
    o_ref[0, CMP_PAD:, :] = _dot(hid.astype(_BF), w2_ref[...]).astype(o_ref.dtype)


def _compress(chunks, pe_flat, w1, w2):
    g, n, kd = chunks.shape
    d = w2.shape[1]
    return pl.pallas_call(
        _compress_kernel,
        out_shape=jax.ShapeDtypeStruct((g, CMP_PAD + n, d), _BF),
        grid=(g,),
        in_specs=[
            pl.BlockSpec((1, n, kd), lambda i: (i, 0, 0)),
            pl.BlockSpec(pe_flat.shape, lambda i: (0, 0)),
            pl.BlockSpec(w1.shape, lambda i: (0, 0)),
            pl.BlockSpec(w2.shape, lambda i: (0, 0)),
        ],
        out_specs=pl.BlockSpec((1, CMP_PAD + n, d), lambda i: (i, 0, 0)),
        compiler_params=_params(("arbitrary",)),
        name="nsa_compress",
    )(chunks, pe_flat, w1, w2)


def _nsa_cmp_kernel(q_ref, kc_ref, vc_ref, afar_ref, tc_ref, oc_ref, sel_ref):
    t = q_ref.shape[0]
    nc = kc_ref.shape[1] - CMP_PAD
    per_tile = t // CMP_STRIDE
    i = pl.program_id(1)
    thr = (i + 1) * per_tile - CMP_PAD
    near0 = pl.multiple_of((i + 1) * per_tile, per_tile)

    k_ext = jnp.concatenate([kc_ref[0, CMP_PAD:, :], kc_ref[0, pl.ds(near0, CMP_PAD), :]], axis=0)
    vcn_t = vc_ref[0, pl.ds(near0, CMP_PAD), :].astype(_F32).T.astype(_BF)
    jn = lax.broadcasted_iota(jnp.int32, (N_SLC_PAD, CMP_PAD), 0) * SLC_LEN
    cn = (thr + lax.broadcasted_iota(jnp.int32, (N_SLC_PAD, CMP_PAD), 1)) * CMP_STRIDE
    ovn_t = jnp.where((cn < jn + SLC_LEN) & (cn + CMP_LEN > jn), 1.0, 0.0).astype(_BF)
    a_near = jnp.concatenate([vcn_t, ovn_t], axis=0)
    a_far = afar_ref[0]

    far_mask = lax.broadcasted_iota(jnp.int32, (nc, t), 0) < thr
    c = lax.broadcasted_iota(jnp.int32, (CMP_PAD, t), 0)
    r = lax.broadcasted_iota(jnp.int32, (CMP_PAD, t), 1)
    near_mask = (CMP_STRIDE * c <= r - (CMP_LEN - 1) - t + CMP_STRIDE * CMP_PAD) & (c >= -thr)

    def logits(h):
        return _dot_nt(k_ext, q_ref[:, h * DKP:(h + 1) * DKP])

    imp = jnp.zeros((N_SLC_PAD, t), _F32)
    st_next = logits(0)
    for h in range(NSA_HPG):
        st = st_next
        if h + 1 < NSA_HPG:
            st_next = logits(h + 1)
        sf = jnp.where(far_mask, st[:nc], _NEG_INF)
        sn = jnp.where(near_mask, st[nc:] + tc_ref[h, 0], _NEG_INF)
        m = jnp.maximum(jnp.max(sf, axis=0, keepdims=True), jnp.max(sn, axis=0, keepdims=True))
        m = jnp.where(m == _NEG_INF, 0.0, m)
        pf = jnp.exp2(sf - m)
        pn = jnp.exp2(sn - m)
        den = jnp.sum(pf, axis=0, keepdims=True) + jnp.sum(pn, axis=0, keepdims=True)
        inv = 1.0 / jnp.maximum(den, 1e-30)
        pf = pf * inv
        pn = pn * inv
        pf_hi = pf.astype(_BF)
        pn_hi = pn.astype(_BF)
        res = _dot(a_far, pf_hi) + _dot(a_near, pn_hi)
        oc_ref[h * NSA_DV:(h + 1) * NSA_DV, :] = res[:NSA_DV].astype(oc_ref.dtype)
        pf_lo = (pf - pf_hi.astype(_F32)).astype(_BF)
        pn_lo = (pn - pn_hi.astype(_F32)).astype(_BF)
        imp = imp + res[NSA_DV:] + (_dot(a_far[NSA_DV:], pf_lo) + _dot(a_near[NSA_DV:], pn_lo))

    jj = lax.broadcasted_iota(jnp.int32, (N_SLC_PAD, t), 0)
    tq = i * t + lax.broadcasted_iota(jnp.int32, (N_SLC_PAD, t), 1)
    cur = tq >> SLC_SHIFT
    valid = jj * SLC_LEN <= tq
    forced = (jj == 0) | ((jj <= cur) & (jj >= cur - 1))
    work = jnp.where(valid, jnp.where(forced, jnp.inf, imp), _NEG_INF)
    blk = jj.astype(_F32)
    sel = jnp.zeros((N_SLC_PAD, t), _F32)
    for _ in range(SLC_TOPN):
        mx = jnp.max(work, axis=0, keepdims=True)
        first = jnp.min(jnp.where(work == mx, blk, float(N_SLC_PAD)), axis=0, keepdims=True)
        hit = blk == first
        sel = jnp.where(hit & (mx > _NEG_INF), 1.0, sel)
        work = jnp.where(hit, _NEG_INF, work)
    sel_ref[0] = sel


def _nsa_cmp(proj, kc_pad, vc_pad, a_far, tc):
    s = proj.shape[0]
    t = tc.shape[3]
    gw = NSA_HPG * DKP
    return pl.pallas_call(
        _nsa_cmp_kernel,
        out_shape=(
            jax.ShapeDtypeStruct((NSA_HEADS * NSA_DV, s), _BF),
            jax.ShapeDtypeStruct((NSA_GROUPS, N_SLC_PAD, s), _F32),
        ),
        grid=(NSA_GROUPS, s // t),
        in_specs=[
            pl.BlockSpec((t, gw), lambda g, i: (i, COL_QN // gw + g)),
            pl.BlockSpec((1,) + kc_pad.shape[1:], lambda g, i: (g, 0, 0)),
            pl.BlockSpec((1,) + vc_pad.shape[1:], lambda g, i: (g, 0, 0)),
            pl.BlockSpec((1,) + a_far.shape[1:], lambda g, i: (g, 0, 0)),
            pl.BlockSpec((NSA_HPG, 1, CMP_PAD, t), lambda g, i: (g, 0, 0, 0)),
        ],
        out_specs=(
            pl.BlockSpec((NSA_HPG * NSA_DV, t), lambda g, i: (g, i)),
            pl.BlockSpec((1, N_SLC_PAD, t), lambda g, i: (g, 0, i)),
        ),
        compiler_params=_params(("parallel", "arbitrary")),
        name="nsa_cmp",
    )(proj, kc_pad, vc_pad, a_far, tc)


def _nsa_main_kernel(q_ref, ks_ref, vs_ref, kw_ref, vw_ref, selt_ref, tw_ref, oc_ref, gate_ref,
                     o_ref, m_ref, l_ref, acc_ref, *, n_sub):
    t = q_ref.shape[0]

    def value_tile(ref, tile):
        return ref[pl.ds(pl.multiple_of(tile * t, t), t), :]

    tk = n_sub * t
    g = pl.program_id(0)
    i = pl.program_id(1)
    blocks_per_tile = t // SLC_LEN
    qs = [q_ref[:, h * DKP:(h + 1) * DKP] for h in range(NSA_HPG)]

    m_ref[...] = jnp.full(m_ref.shape, -1e30, _F32)
    l_ref[...] = jnp.zeros(l_ref.shape, _F32)
    acc_ref[...] = jnp.zeros(acc_ref.shape, _F32)

    def selected(tile0, n_tiles):
        rows = [jnp.broadcast_to(selt_ref[0, pl.ds(tile0 * blocks_per_tile + b, 1), :], (SLC_LEN, t))
                for b in range(n_tiles * blocks_per_tile)]
        return jnp.concatenate(rows, axis=0) > 0.5

    def run_heads(k, vts, mask, bias):
        def logits(h):
            st = _dot_nt(k, qs[h])
            if bias is not None:
                st = st + bias(h)
            return jnp.where(mask, st, _NEG_INF)

        st = logits(0)
        for h in range(NSA_HPG):
            st_next = logits(h + 1) if h + 1 < NSA_HPG else None
            m, l, acc = _online_update_t(st, vts, m_ref[h], l_ref[h], acc_ref[h])
            m_ref[h] = m
            l_ref[h] = l
            acc_ref[h] = acc
            st = st_next

    def far_chunk(tile0, n_tiles, first_new):
        r0 = pl.multiple_of(tile0 * t, t)
        mask = selected(tile0, n_tiles)
        if first_new is not None:
            kpos = r0 + lax.broadcasted_iota(jnp.int32, (n_tiles * t, t), 0)
            mask = mask & (kpos >= first_new) & (kpos < n_far * t)
        run_heads(ks_ref[pl.ds(r0, n_tiles * t), :],
                  [value_tile(vs_ref, tile0 + a) for a in range(n_tiles)], mask, None)

    n_far = jnp.maximum(i - 1, 0)
    n_full = n_far // n_sub
    n_left = n_far - n_full * n_sub

    def far_body(c, carry):
        far_chunk(c * n_sub, n_sub, None)
        return carry

    lax.fori_loop(0, n_full, far_body, 0)

    half = n_sub // 2

    @pl.when((n_left > 0) & (n_left <= half))
    def _():
        far_chunk(jnp.maximum(n_far - half, 0), half, n_full * tk)

    @pl.when(n_left > half)
    def _():
        far_chunk(jnp.maximum(n_far - n_sub, 0), n_sub, n_full * tk)

    prev = jnp.maximum(i - 1, 0)
    row2 = lax.broadcasted_iota(jnp.int32, (2 * t, t), 0)
    col2 = lax.broadcasted_iota(jnp.int32, (2 * t, t), 1)
    near_mask = (jnp.concatenate([selected(prev, 1), selected(i, 1)], axis=0)
                 & (col2 - row2 + t >= 0) & (row2 >= jnp.where(i >= 1, 0, t)))
    k_near = jnp.concatenate([ks_ref[pl.ds(pl.multiple_of(prev * t, t), t), :],
                              ks_ref[pl.ds(pl.multiple_of(i * t, t), t), :]], axis=0)
    run_heads(k_near, [value_tile(vs_ref, prev), value_tile(vs_ref, i)], near_mask,
              lambda h: tw_ref[h, 0, t:, :])

    prev2 = jnp.maximum(i - 2, 0)
    w_tiles = (prev2, prev, i)
    k_win = jnp.concatenate([kw_ref[pl.ds(pl.multiple_of(a * t, t), t), :] for a in w_tiles], axis=0)
    row3 = lax.broadcasted_iota(jnp.int32, (3 * t, t), 0)
    col3 = lax.broadcasted_iota(jnp.int32, (3 * t, t), 1)
    dist = col3 - row3 + 2 * t
    first_row = jnp.where(i >= 2, 0, jnp.where(i >= 1, t, 2 * t))
    win_mask = (dist >= 0) & (dist < WINDOW) & (row3 >= first_row)

    def window_logits(h):
        return jnp.where(win_mask, _dot_nt(k_win, qs[h]) + tw_ref[h, 0], _NEG_INF)

    gates_t = jax.nn.sigmoid(gate_ref[...].astype(_F32)).T
    st_next = window_logits(0)
    for h in range(NSA_HPG):
        st = st_next
        if h + 1 < NSA_HPG:
            st_next = window_logits(h + 1)
        m = jnp.max(st, axis=0, keepdims=True)
        pt = jnp.exp2(st - m)
        den = jnp.sum(pt, axis=0, keepdims=True)
        pb = pt.astype(_BF)
        o_w = functools.reduce(jnp.add, [_dot_tn(value_tile(vw_ref, a), pb[n * t:(n + 1) * t, :])
                                         for n, a in enumerate(w_tiles)]) * (1.0 / den)
        o_s = acc_ref[h] * (1.0 / l_ref[h])
        o_c = oc_ref[h * NSA_DV:(h + 1) * NSA_DV, :].astype(_F32)
        gc, gs, gw = [jnp.where(g == 0, gates_t[h * N_BRANCH + b:h * N_BRANCH + b + 1, :],
                                gates_t[(NSA_HPG + h) * N_BRANCH + b:(NSA_HPG + h) * N_BRANCH + b + 1, :])
                      for b in range(N_BRANCH)]
        o_ref[:, h * NSA_DV:(h + 1) * NSA_DV] = (gc * o_c + gs * o_s + gw * o_w).T.astype(o_ref.dtype)


def _nsa_main(proj, sel_t, tw, oc_t, *, n_sub=4):
    s = proj.shape[0]
    t = ATT_TILE
    nb = s // t
    gw = NSA_HPG * DKP
    kn0 = COL_KN // DKP
    vn0 = COL_VN // NSA_DV
    return pl.pallas_call(
        functools.partial(_nsa_main_kernel, n_sub=n_sub),
        out_shape=jax.ShapeDtypeStruct((s, NSA_HEADS * NSA_DV), _BF),
        grid=(NSA_GROUPS, nb),
        in_specs=[
            pl.BlockSpec((t, gw), lambda g, i: (i, COL_QN // gw + g)),
            pl.BlockSpec((s, DKP), lambda g, i: (0, kn0 + 1 * NSA_GROUPS + g)),
            pl.BlockSpec((s, NSA_DV), lambda g, i: (0, vn0 + 1 * NSA_GROUPS + g)),
            pl.BlockSpec((s, DKP), lambda g, i: (0, kn0 + 2 * NSA_GROUPS + g)),
            pl.BlockSpec((s, NSA_DV), lambda g, i: (0, vn0 + 2 * NSA_GROUPS + g)),
            pl.BlockSpec((1, N_SLC_PAD, t), lambda g, i: (g, 0, i)),
            pl.BlockSpec((NSA_HPG, 1, 3 * t, t), lambda g, i: (g, 0, 0, 0)),
            pl.BlockSpec((NSA_HPG * NSA_DV, t), lambda g, i: (g, i)),
            pl.BlockSpec((t, LANES), lambda g, i: (i, COL_GATE // LANES)),
        ],
        out_specs=pl.BlockSpec((t, NSA_HPG * NSA_DV), lambda g, i: (i, g)),
        scratch_shapes=[
            pltpu.VMEM((NSA_HPG, 1, t), _F32),
            pltpu.VMEM((NSA_HPG, 1, t), _F32),
            pltpu.VMEM((NSA_HPG, NSA_DV, t), _F32),
        ],
        compiler_params=_params(("parallel", "arbitrary")),
        name="nsa_main",
    )(proj, proj, proj, proj, proj, sel_t, tw, oc_t, proj)


def _outproj_kernel(x_ref, mod_ref, om_ref, on_ref, w_ref, o_ref, *, mod_row):
    half = om_ref.shape[1]
    y = _dot(om_ref[...], w_ref[:half, :]) + _dot(on_ref[...], w_ref[half:, :])
    o_ref[...] = x_ref[...] + mod_ref[mod_row:mod_row + 1, :] * y


def _outproj(x, mod, o_mla, o_nsa, w_out, *, mod_row, tm=512):
    s = x.shape[0]
    return pl.pallas_call(
        functools.partial(_outproj_kernel, mod_row=mod_row),
        out_shape=jax.ShapeDtypeStruct((s, D_MODEL), _F32),
        grid=(s // tm,),
        in_specs=[
            pl.BlockSpec((tm, D_MODEL), lambda i: (i, 0)),
            pl.BlockSpec((9, D_MODEL), lambda i: (0, 0)),
            pl.BlockSpec((tm, o_mla.shape[1]), lambda i: (i, 0)),
            pl.BlockSpec((tm, o_nsa.shape[1]), lambda i: (i, 0)),
            pl.BlockSpec(w_out.shape, lambda i: (0, 0)),
        ],
        out_specs=pl.BlockSpec((tm, D_MODEL), lambda i: (i, 0)),
        compiler_params=_params(("parallel",)),
        name="outproj",
    )(x, mod, o_mla, o_nsa, w_out)


def _pad_cols(w, width):
    return jnp.pad(w, ((0, 0), (0, width - w.shape[1])))


def _pad_last(w, width):
    return jnp.pad(w, [(0, 0)] * (w.ndim - 1) + [(0, width - w.shape[-1])])


def _rotate_half_cols(w):
    half = MLA_ROPE // 2
    return jnp.concatenate([-w[..., half:], w[..., :half]], axis=-1)


def _w_in_padded(w_in):
    d = w_in.shape[0]
    w_in = w_in.astype(_BF)
    sizes = (MLA_Q_RANK, MLA_KV_RANK, MLA_ROPE, NSA_HEADS * NSA_DK,
             N_BRANCH * NSA_GROUPS * NSA_DK, N_BRANCH * NSA_GROUPS * NSA_DV, NSA_HEADS * N_BRANCH)
    offs = np.concatenate([[0], np.cumsum(sizes)])
    cq, ckv, kr, qn, kn, vn, gn = [w_in[:, offs[a]:offs[a + 1]] for a in range(len(sizes))]
    qn = _pad_last(qn.reshape(d, NSA_HEADS, NSA_DK), DKP).reshape(d, NSA_HEADS * DKP)
    kn = _pad_last(kn.reshape(d, N_BRANCH * NSA_GROUPS, NSA_DK), DKP).reshape(d, -1)
    cols = [cq, _pad_cols(gn, LANES), _pad_cols(kr, LANES), ckv,
            _pad_cols(_rotate_half_cols(kr), LANES),
            jnp.zeros((d, COL_QN - COL_KR_ROT - LANES), w_in.dtype), qn, kn, vn]
    w = jnp.concatenate(cols, axis=1)
    assert w.shape[1] == D_IN_PAD
    colscale = np.ones((1, D_IN_PAD), np.float32)
    colscale[:, COL_QN:COL_KN] = NSA_DK ** -0.5 * LOG2_E
    return w, jnp.asarray(colscale)


def _w_uq_padded(w_uq):
    r = w_uq.shape[0]
    w = w_uq.reshape(r, MLA_HEADS, MLA_NOPE + MLA_ROPE)
    nope = w[..., :MLA_NOPE].reshape(r, -1)
    rope = w[..., MLA_NOPE:]
    rope_p = _pad_last(rope, LANES).reshape(r, -1)
    rot_p = _pad_last(_rotate_half_cols(rope), LANES).reshape(r, -1)
    return jnp.concatenate([nope, rope_p, rot_p], axis=1).astype(_BF)


def _cmp_w1_padded(w1, d, dpad):
    return _pad_last(w1.reshape(CMP_LEN, d, CMP_HIDDEN).transpose(0, 2, 1), dpad) \
        .transpose(0, 2, 1).reshape(CMP_LEN * dpad, CMP_HIDDEN).astype(_BF)


def _rope_tables(s):
    inv = ROPE_THETA ** (-jnp.arange(0, MLA_ROPE, 2, dtype=_F32) / MLA_ROPE)
    ang = jnp.arange(s, dtype=_F32)[:, None] * inv[None, :]
    cos = _pad_cols(jnp.tile(jnp.cos(ang), (1, 2)), LANES)
    sin = _pad_cols(jnp.tile(jnp.sin(ang), (1, 2)), LANES)
    return cos, sin


def _overlap_t(s):
    s_start = np.arange(N_SLC_PAD)[:, None] * SLC_LEN
    c_start = np.arange(s // CMP_STRIDE)[None, :] * CMP_STRIDE
    ov = (c_start < s_start + SLC_LEN) & (c_start + CMP_LEN > s_start)
    return jnp.asarray(ov.astype(np.float32), _BF)


def _mixer_heads(x, mod, norm_mix, w_in, mla_q_norm, w_uq, mla_kv_norm, w_ukv, pe_cmp_k, w_cmp_k1,
                 w_cmp_k2, pe_cmp_v, w_cmp_v1, w_cmp_v2, rel_bias):
    s = x.shape[0]
    assert s % ATT_TILE == 0 and s // SLC_LEN <= N_SLC_PAD and s >= 3 * ATT_TILE
    w_pad, colscale = _w_in_padded(w_in)
    proj = _inproj(x, mod, norm_mix.reshape(1, -1), w_pad, colscale, mod_row=3)

    cos, sin = _rope_tables(s)
    q, k, v = _mla_prep(proj, mla_q_norm.reshape(1, -1), mla_kv_norm.reshape(1, -1),
                        _w_uq_padded(w_uq), w_ukv.astype(_BF), cos, sin)
    o_mla = _mla_attn(q, k, v)

    n_chunks = s // CMP_STRIDE
    kn0 = COL_KN
    k_cmp = jnp.stack([proj[:, kn0 + g * DKP:kn0 + (g + 1) * DKP] for g in range(NSA_GROUPS)])
    v_cmp = jnp.stack([proj[:, COL_VN + g * NSA_DV:COL_VN + (g + 1) * NSA_DV] for g in range(NSA_GROUPS)])
    pe_k = jnp.broadcast_to(_pad_last(pe_cmp_k, DKP).reshape(1, -1), (SUBLANES, CMP_LEN * DKP)).astype(_BF)
    pe_v = jnp.broadcast_to(pe_cmp_v.reshape(1, -1), (SUBLANES, CMP_LEN * NSA_DV)).astype(_BF)
    kc = _compress(k_cmp.reshape(NSA_GROUPS, n_chunks, CMP_STRIDE * DKP), pe_k,
                   _cmp_w1_padded(w_cmp_k1, NSA_DK, DKP), _pad_cols(w_cmp_k2, DKP).astype(_BF))
    vc = _compress(v_cmp.reshape(NSA_GROUPS, n_chunks, CMP_STRIDE * NSA_DV), pe_v,
                   w_cmp_v1.astype(_BF), w_cmp_v2.astype(_BF))
    tc = _bias_tables(rel_bias, _cmp_buckets(CMP_TILE))
    tw = _bias_tables(rel_bias, _window_buckets(ATT_TILE))
    ov_t = _overlap_t(s)
    a_far = jnp.stack([jnp.concatenate([vc[g, CMP_PAD:].T, ov_t], axis=0) for g in range(NSA_GROUPS)])
    oc_t, sel_t = _nsa_cmp(proj, kc, vc, a_far, tc)
    o_nsa = _nsa_main(proj, sel_t, tw, oc_t)
    return o_mla, o_nsa


def kernel(x, c, w_ada, b_ada, norm_ffn1, w1_gate, w1_up, w1_down, norm_mix, w_in, mla_q_norm, w_uq, mla_kv_norm, w_ukv, pe_cmp_k, w_cmp_k1, w_cmp_k2, pe_cmp_v, w_cmp_v1, w_cmp_v2, rel_bias, w_out, norm_ffn2, w2_gate, w2_up, w2_down, norm_final):
    assert x.shape[0] == 1 and w_ada.shape[0] == 1
    xs = x[0]
    fin = norm_final.reshape(1, -1)
    mod = _ada_mod(c, w_ada[0], b_ada[0]).reshape(9, D_MODEL)
    xs = _ffn(xs, mod, norm_ffn1[0].reshape(1, -1), w1_gate[0], w1_up[0], w1_down[0], fin,
              mod_row=0, final_norm=False)
    o_mla, o_nsa = _mixer_heads(xs, mod, norm_mix[0], w_in[0], mla_q_norm[0], w_uq[0], mla_kv_norm[0],
                                w_ukv[0], pe_cmp_k[0], w_cmp_k1[0], w_cmp_k2[0], pe_cmp_v[0],
                                w_cmp_v1[0], w_cmp_v2[0], rel_bias)
    xs = _outproj(xs, mod, o_mla, o_nsa, w_out[0].astype(_BF), mod_row=5)
    xs = _ffn(xs, mod, norm_ffn2[0].reshape(1, -1), w2_gate[0], w2_up[0], w2_down[0], fin,
              mod_row=6, final_norm=True)
    return xs[None]
```
